```python
import jax, jax.numpy as jnp
from jax import lax
import numpy as np

D_MODEL = 2048
BATCH = 4
SEQ = 2048
DEPTH = 1
DEC_BATCH = 128
DEC_SEQ = 8
PAST_LEN = 16384
PAGE_SIZE = 128

D_HALF = D_MODEL // 2
G_A = 4
D_GROUP_A = D_HALF // G_A
CHUNK = 128
D_CONV = D_MODEL // 2
CONV_W = 3
N_GROUPS = 4
EXPERTS_PER_GROUP = 8
N_EXPERTS = N_GROUPS * EXPERTS_PER_GROUP
TOP_K_INNER = 2
D_EXPERT = D_MODEL // 4
PLE_DIM = 256
D_IN_TOTAL = 2 * D_HALF + 3 * D_CONV + 2 * D_MODEL
EPS = 1e-6

kernel_name = "hybrid_gmlp_shortconv_hmoe_step"


def rms_norm(x, g):
    xf = x.astype(jnp.float32)
    y = xf * lax.rsqrt(jnp.mean(xf * xf, axis=-1, keepdims=True) + EPS)
    return (y * g.astype(jnp.float32)).astype(x.dtype)


def layer_norm(x, g, b):
    xf = x.astype(jnp.float32)
    mu = jnp.mean(xf, axis=-1, keepdims=True)
    xc = xf - mu
    y = xc * lax.rsqrt(jnp.mean(xc * xc, axis=-1, keepdims=True) + EPS)
    return (y * g.astype(jnp.float32) + b.astype(jnp.float32)).astype(x.dtype)


def chunk_spatial_gate(u, v, w_s, b_s):
    n, L, _ = v.shape
    n_chunks = -(-L // CHUNK)
    pad = n_chunks * CHUNK - L
    vp = jnp.pad(v, ((0, 0), (0, pad), (0, 0))).reshape(n, n_chunks, CHUNK, G_A, D_GROUP_A)
    mask = jnp.tril(jnp.ones((CHUNK, CHUNK), dtype=bool))
    w = jnp.where(mask[None], w_s, 0).astype(v.dtype)
    s = jnp.einsum('gts,ncsgd->nctgd', w, vp) + b_s.T[None, None, :, :, None].astype(v.dtype)
    s = s.reshape(n, n_chunks * CHUNK, D_HALF)[:, :L]
    v_state = v[:, (n_chunks - 1) * CHUNK:]
    return u * s, v_state


def short_conv(q, conv_state, conv_w):
    L = q.shape[1]
    qp = jnp.concatenate([conv_state.astype(q.dtype), q], axis=1)
    y = conv_w[0] * qp[:, 0:L]
    for k in range(1, CONV_W):
        y = y + conv_w[k] * qp[:, k:k + L]
    return y, qp[:, L:]


def hier_moe(x, w_rg, b_rg, w_re, b_re, w_gate, w_up, w_down):
    shp = x.shape
    xf = x.reshape(-1, shp[-1])
    t = xf.shape[0]
    lg = (xf @ w_rg + b_rg).astype(jnp.float32)
    pg = jax.nn.softmax(lg, axis=-1)
    g_sel = jnp.argmax(lg, axis=-1)
    p_group = jnp.take_along_axis(pg, g_sel[:, None], axis=1)
    le = (xf @ w_re + b_re).astype(jnp.float32).reshape(t, N_GROUPS, EXPERTS_PER_GROUP)
    le_sel = jnp.take_along_axis(le, g_sel[:, None, None], axis=1)[:, 0]
    pe = jax.nn.softmax(le_sel, axis=-1)
    top_p, top_i = lax.top_k(pe, TOP_K_INNER)
    top_p = top_p / jnp.sum(top_p, axis=-1, keepdims=True)
    expert_id = g_sel[:, None] * EXPERTS_PER_GROUP + top_i
    weights = p_group * top_p
    combine = jnp.einsum('tk,tke->te', weights, jax.nn.one_hot(expert_id, N_EXPERTS, dtype=jnp.float32))
    hid = jax.nn.silu(jnp.einsum('td,edf->tef', xf, w_gate)) * jnp.einsum('td,edf->tef', xf, w_up)
    hid = hid * combine.astype(x.dtype)[:, :, None]
    y = jnp.einsum('tef,efd->td', hid, w_down)
    return y.reshape(shp)


def decoder_layer(h, p, conv_state, g_mix, w_in, ln_v_g, ln_v_b, w_spatial, b_spatial, conv_w,
                  w_br_a, w_br_b, w_out, g_moe, w_router_group, b_router_group,
                  w_router_expert, b_router_expert, w_exp_gate, w_exp_up, w_exp_down,
                  g_ple, w_ple_gate, w_ple_proj):
    n = rms_norm(h, g_mix)
    z = n @ w_in
    cuts = [D_HALF, 2 * D_HALF, 2 * D_HALF + D_CONV, 2 * D_HALF + 2 * D_CONV,
            2 * D_HALF + 3 * D_CONV, 2 * D_HALF + 3 * D_CONV + D_MODEL]
    u, v, bg, cg, xv, ga, gb = jnp.split(z, cuts, axis=-1)
    v = layer_norm(v, ln_v_g, ln_v_b)
    ya, v_state = chunk_spatial_gate(u, v, w_spatial, b_spatial)
    yc, conv_state_new = short_conv(cg * xv, conv_state, conv_w)
    yb = bg * yc
    merged = jax.nn.sigmoid(ga) * (ya @ w_br_a) + jax.nn.sigmoid(gb) * (yb @ w_br_b)
    h = h + merged @ w_out
    h = h + hier_moe(rms_norm(h, g_moe), w_router_group, b_router_group, w_router_expert,
                     b_router_expert, w_exp_gate, w_exp_up, w_exp_down)
    gate = jax.nn.sigmoid(rms_norm(h, g_ple) @ w_ple_gate)
    h = h + gate * (p @ w_ple_proj)
    return h, conv_state_new, v_state


def setup_inputs(seed: int = 0) -> dict:
    key = jax.random.key(seed)
    ks = jax.random.split(key, 32)
    f32 = jnp.float32

    def nrm(k, shape, scale):
        return jax.random.normal(k, shape, f32) * scale

    def gain(k, shape):
        return 1.0 + 0.05 * jax.random.normal(k, shape, f32)

    return {
        "x_prompt": nrm(ks[0], (BATCH, SEQ, D_MODEL), 1.0),
        "x_sample": nrm(ks[1], (DEC_BATCH, DEC_SEQ, D_MODEL), 1.0),
        "state_conv": nrm(ks[2], (DEPTH, DEC_BATCH, CONV_W - 1, D_CONV), 1.0),
        "p_prompt": nrm(ks[3], (DEPTH, BATCH, SEQ, PLE_DIM), 1.0),
        "p_sample": nrm(ks[4], (DEPTH, DEC_BATCH, DEC_SEQ, PLE_DIM), 1.0),
        "g_mix": gain(ks[5], (DEPTH, D_MODEL)),
        "w_in": nrm(ks[6], (DEPTH, D_MODEL, D_IN_TOTAL), D_MODEL ** -0.5),
        "ln_v_g": gain(ks[7], (DEPTH, D_HALF)),
        "ln_v_b": nrm(ks[8], (DEPTH, D_HALF), 0.02),
        "w_spatial": nrm(ks[9], (DEPTH, G_A, CHUNK, CHUNK), CHUNK ** -0.5),
        "b_spatial": gain(ks[10], (DEPTH, G_A, CHUNK)),
        "conv_w": nrm(ks[11], (DEPTH, CONV_W, D_CONV), CONV_W ** -0.5),
        "w_br_a": nrm(ks[12], (DEPTH, D_HALF, D_MODEL), D_HALF ** -0.5),
        "w_br_b": nrm(ks[13], (DEPTH, D_CONV, D_MODEL), D_CONV ** -0.5),
        "w_out": nrm(ks[14], (DEPTH, D_MODEL, D_MODEL), D_MODEL ** -0.5),
        "g_moe": gain(ks[15], (DEPTH, D_MODEL)),
        "w_router_group": nrm(ks[16], (DEPTH, D_MODEL, N_GROUPS), D_MODEL ** -0.5),
        "b_router_group": nrm(ks[17], (DEPTH, N_GROUPS), 0.01),
        "w_router_expert": nrm(ks[18], (DEPTH, D_MODEL, N_EXPERTS), D_MODEL ** -0.5),
        "b_router_expert": nrm(ks[19], (DEPTH, N_EXPERTS), 0.01),
        "w_exp_gate": nrm(ks[20], (DEPTH, N_EXPERTS, D_MODEL, D_EXPERT), D_MODEL ** -0.5),
        "w_exp_up": nrm(ks[21], (DEPTH, N_EXPERTS, D_MODEL, D_EXPERT), D_MODEL ** -0.5),
        "w_exp_down": nrm(ks[22], (DEPTH, N_EXPERTS, D_EXPERT, D_MODEL), D_EXPERT ** -0.5),
        "g_ple": gain(ks[23], (DEPTH, D_MODEL)),
        "w_ple_gate": nrm(ks[24], (DEPTH, D_MODEL, D_MODEL), D_MODEL ** -0.5),
        "w_ple_proj": nrm(ks[25], (DEPTH, PLE_DIM, D_MODEL), PLE_DIM ** -0.5),
        "g_final": gain(ks[26], (D_MODEL,)),
    }


def reference(x_prompt, x_sample, state_conv, p_prompt, p_sample, g_mix, w_in, ln_v_g, ln_v_b,
              w_spatial, b_spatial, conv_w, w_br_a, w_br_b, w_out, g_moe, w_router_group,
              b_router_group, w_router_expert, b_router_expert, w_exp_gate, w_exp_up, w_exp_down,
              g_ple, w_ple_gate, w_ple_proj, g_final):
    hp, hs = x_prompt, x_sample
    cs_p_list, cs_s_list, vs_p_list, vs_s_list = [], [], [], []
    for i in range(DEPTH):
        lw = (g_mix[i], w_in[i], ln_v_g[i], ln_v_b[i], w_spatial[i], b_spatial[i], conv_w[i],
              w_br_a[i], w_br_b[i], w_out[i], g_moe[i], w_router_group[i], b_router_group[i],
              w_router_expert[i], b_router_expert[i], w_exp_gate[i], w_exp_up[i], w_exp_down[i],
              g_ple[i], w_ple_gate[i], w_ple_proj[i])
        zero_state = jnp.zeros((hp.shape[0], CONV_W - 1, D_CONV), hp.dtype)
        hp, cs_p, vs_p = decoder_layer(hp, p_prompt[i], zero_state, *lw)
        hs, cs_s, vs_s = decoder_layer(hs, p_sample[i], state_conv[i], *lw)
        cs_p_list.append(cs_p)
        cs_s_list.append(cs_s)
        vs_p_list.append(vs_p)
        vs_s_list.append(vs_s)
    y_prompt = rms_norm(hp, g_final)
    y_sample = rms_norm(hs, g_final)
    conv_state_prompt = jnp.stack(cs_p_list, axis=0)
    conv_state_sample = jnp.stack(cs_s_list, axis=0)
    v_rows_prompt = jnp.stack(vs_p_list, axis=0)
    v_rows_sample = jnp.stack(vs_s_list, axis=0)
    return (y_prompt, y_sample, conv_state_prompt, conv_state_sample, v_rows_prompt, v_rows_sample)
```

```python
import functools

import jax
import jax.numpy as jnp
from jax import lax
from jax.experimental import pallas as pl
from jax.experimental.pallas import tpu as pltpu

D_MODEL = 2048
D_HALF = D_MODEL // 2
D_CONV = D_MODEL // 2
G_A = 4
D_GROUP_A = D_HALF // G_A
CHUNK = 128
CONV_W = 3
N_GROUPS = 4
EXPERTS_PER_GROUP = 8
N_EXPERTS = N_GROUPS * EXPERTS_PER_GROUP
D_EXPERT = D_MODEL // 4
D_IN_TOTAL = 2 * D_HALF + 3 * D_CONV + 2 * D_MODEL
EPS = 1e-6

BF16 = jnp.bfloat16
F32 = jnp.float32

LANES = 128
VMEM_LIMIT = 60 * 1024 * 1024

TM_NORM = 512
TN_IN = 1024
TM = 256
TM_E = 256


def _rms(x, g):
    return x * lax.rsqrt(jnp.mean(x * x, axis=-1, keepdims=True) + EPS) * g


def _sigmoid(x):
    return 1.0 / (1.0 + jnp.exp(-x))


def _norm_kernel(xp_ref, xs_ref, g_ref, o_ref, *, n_prompt_tiles):
    i = pl.program_id(0)

    def run(x_ref):
        o_ref[...] = _rms(x_ref[...], g_ref[...]).astype(BF16)

    @pl.when(i < n_prompt_tiles)
    def _():
        run(xp_ref)

    @pl.when(i >= n_prompt_tiles)
    def _():
        run(xs_ref)


def _norm(xp, xs, g):
    tp, ts = xp.shape[0], xs.shape[0]
    npt, nst = tp // TM_NORM, ts // TM_NORM
    return pl.pallas_call(
        functools.partial(_norm_kernel, n_prompt_tiles=npt),
        grid=(npt + nst,),
        in_specs=[
            pl.BlockSpec((TM_NORM, D_MODEL), lambda i: (jnp.minimum(i, npt - 1), 0)),
            pl.BlockSpec((TM_NORM, D_MODEL), lambda i: (jnp.maximum(i - npt, 0), 0)),
            pl.BlockSpec((1, D_MODEL), lambda i: (0, 0)),
        ],
        out_specs=pl.BlockSpec((TM_NORM, D_MODEL), lambda i: (i, 0)),
        out_shape=jax.ShapeDtypeStruct((tp + ts, D_MODEL), BF16),
        compiler_params=pltpu.CompilerParams(
            dimension_semantics=("arbitrary",), vmem_limit_bytes=VMEM_LIMIT),
        name="norm",
    )(xp, xs, g)


def _inproj_kernel(n_ref, w_ref, z_ref, wb_ref):
    @pl.when(pl.program_id(1) == 0)
    def _():
        wb_ref[...] = w_ref[...].astype(BF16)

    z_ref[...] = jnp.dot(n_ref[...], wb_ref[...], preferred_element_type=F32).astype(BF16)


def _inproj(n, w_in):
    t = n.shape[0]
    return pl.pallas_call(
        _inproj_kernel,
        grid=(D_IN_TOTAL // TN_IN, t // TM_NORM),
        in_specs=[
            pl.BlockSpec((TM_NORM, D_MODEL), lambda j, i: (i, 0)),
            pl.BlockSpec((None, D_MODEL, TN_IN), lambda j, i: (0, 0, j)),
        ],
        out_specs=pl.BlockSpec((TM_NORM, TN_IN), lambda j, i: (i, j)),
        out_shape=jax.ShapeDtypeStruct((t, D_IN_TOTAL), BF16),
        scratch_shapes=[pltpu.VMEM((D_MODEL, TN_IN), BF16)],
        compiler_params=pltpu.CompilerParams(
            dimension_semantics=("arbitrary", "arbitrary"), vmem_limit_bytes=VMEM_LIMIT),
        name="in_proj",
    )(n, w_in)


def _route(lg):
    shape = lg.shape
    lane = lax.broadcasted_iota(jnp.int32, shape, 1)
    lanef = lane.astype(F32)
    neg = jnp.float32(-jnp.inf)
    far = jnp.float32(LANES)
    is_g = lane < N_GROUPS
    lgm = jnp.where(is_g, lg, neg)
    gmax = jnp.max(lgm, axis=-1, keepdims=True)
    gsel = jnp.min(jnp.where(lgm == gmax, lanef, far), axis=-1, keepdims=True)
    p_group = 1.0 / jnp.sum(jnp.where(is_g, jnp.exp(lgm - gmax), 0.0), axis=-1, keepdims=True)
    lane_group = ((lane - N_GROUPS) >> 3).astype(F32)
    is_e = lane_group == gsel
    le = jnp.where(is_e, lg, neg)
    m1 = jnp.max(le, axis=-1, keepdims=True)
    i1 = jnp.min(jnp.where(le == m1, lanef, far), axis=-1, keepdims=True)
    le2 = jnp.where(lanef == i1, neg, le)
    m2 = jnp.max(le2, axis=-1, keepdims=True)
    i2 = jnp.min(jnp.where(le2 == m2, lanef, far), axis=-1, keepdims=True)
    e21 = jnp.exp(m2 - m1)
    w1 = 1.0 / (1.0 + e21)
    w2 = e21 * w1
    out = jnp.where(lane == 0, i1 - N_GROUPS,
                    jnp.where(lane == 1, i2 - N_GROUPS,
                              jnp.where(lane == 2, p_group * w1,
                                        jnp.where(lane == 3, p_group * w2, 0.0))))
    return out


def _mixer_kernel(z_ref, xp_ref, xs_ref, f_ref, lng_ref, lnb_ref, cw_ref, wsp_ref, bsp_ref,
                  wa_ref, wb_ref, wo_ref, gm_ref, wr_ref, br_ref,
                  h_ref, xm_ref, route_ref, vnp_ref, vns_ref, qp_ref, qs_ref,
                  qprev_ref, ya_ref, yb_ref, yc_ref, m_ref, *, n_prompt_tiles, tiles_per_seq):
    i = pl.program_id(0)
    is_prompt = i < n_prompt_tiles
    c_u, c_v, c_b, c_c, c_x = (k * D_HALF for k in range(5))
    c_ga = c_x + D_CONV
    c_gb = c_ga + D_MODEL

    v = z_ref[:, c_v:c_v + D_HALF].astype(F32)
    mu = jnp.mean(v, axis=-1, keepdims=True)
    vc = v - mu
    vn = vc * lax.rsqrt(jnp.mean(vc * vc, axis=-1, keepdims=True) + EPS) * lng_ref[...] + lnb_ref[...]
    vnb = vn.astype(BF16)
    for c in range(TM // CHUNK):
        rows = slice(c * CHUNK, (c + 1) * CHUNK)
        for g in range(G_A):
            cols = slice(g * D_GROUP_A, (g + 1) * D_GROUP_A)
            s = jnp.dot(wsp_ref[g], vnb[rows, cols], preferred_element_type=F32) + bsp_ref[:, cols]
            u = z_ref[rows, c_u + g * D_GROUP_A:c_u + (g + 1) * D_GROUP_A].astype(F32)
            ya_ref[rows, cols] = (u * s).astype(BF16)

    q = z_ref[:, c_c:c_c + D_CONV].astype(F32) * z_ref[:, c_x:c_x + D_CONV].astype(F32)
    q1 = pltpu.roll(q, 1, 0)
    q2 = pltpu.roll(q, 2, 0)
    row = lax.broadcasted_iota(jnp.int32, q.shape, 0)
    cw0, cw1, cw2 = cw_ref[0:1, :], cw_ref[1:2, :], cw_ref[2:3, :]

    @pl.when(is_prompt)
    def _():
        seq_start = i % tiles_per_seq == 0
        p6 = jnp.where(seq_start, 0.0, qprev_ref[6:7, :])
        p7 = jnp.where(seq_start, 0.0, qprev_ref[7:8, :])
        a1 = jnp.where(row == 0, p7, q1)
        a2 = jnp.where(row == 0, p6, jnp.where(row == 1, p7, q2))
        yc_ref[...] = cw0 * a2 + cw1 * a1 + cw2 * q
        qprev_ref[...] = q[TM - 8:TM, :]
        qp_ref[...] = q[TM - (CONV_W - 1):TM, :]
        vnp_ref[...] = vn[TM - CHUNK:TM, :]

    @pl.when(jnp.logical_not(is_prompt))
    def _():
        f = f_ref[...]
        f1 = pltpu.roll(f, TM - 1, 0)
        r8 = row & 7
        a1 = jnp.where(r8 == 0, f1, q1)
        a2 = jnp.where(r8 < 2, f, q2)
        yc_ref[...] = cw0 * a2 + cw1 * a1 + cw2 * q
        qs_ref[...] = q
        vns_ref[...] = vn

    yb_ref[...] = (z_ref[:, c_b:c_b + D_CONV].astype(F32) * yc_ref[...]).astype(BF16)

    half = D_MODEL // 2
    for hh in range(2):
        cols = slice(hh * half, (hh + 1) * half)
        a = jnp.dot(ya_ref[...], wa_ref[:, cols], preferred_element_type=F32)
        b = jnp.dot(yb_ref[...], wb_ref[:, cols], preferred_element_type=F32)
        ga = z_ref[:, c_ga + hh * half:c_ga + (hh + 1) * half].astype(F32)
        gb = z_ref[:, c_gb + hh * half:c_gb + (hh + 1) * half].astype(F32)
        m_ref[:, cols] = (_sigmoid(ga) * a + _sigmoid(gb) * b).astype(BF16)

    x = jnp.where(is_prompt, xp_ref[...], xs_ref[...])
    h = x + jnp.dot(m_ref[...], wo_ref[...], preferred_element_type=F32)
    h_ref[...] = h
    xm = _rms(h, gm_ref[...])
    xm_ref[...] = xm
    lg = jnp.dot(xm.astype(BF16), wr_ref[...], preferred_element_type=F32) + br_ref[...]
    route_ref[...] = _route(lg)


def _mixer(z, xp, xs, f, lng, lnb, cw, wsp, bsp, wa, wb, wo, gm, wr, br, *, n_batch):
    tp, ts = xp.shape[0], xs.shape[0]
    t = tp + ts
    npt, nst = tp // TM, ts // TM
    tiles_per_seq = tp // n_batch // TM
    once = pl.Buffered(1)

    def pidx(i):
        return jnp.minimum(i, npt - 1)

    def sidx(i):
        return jnp.maximum(i - npt, 0)

    const = lambda shape: pl.BlockSpec(shape, lambda i: (0,) * len(shape), pipeline_mode=once)
    in_specs = [
        pl.BlockSpec((TM, D_IN_TOTAL), lambda i: (i, 0)),
        pl.BlockSpec((TM, D_MODEL), lambda i: (pidx(i), 0)),
        pl.BlockSpec((TM, D_MODEL), lambda i: (sidx(i), 0)),
        pl.BlockSpec((TM, D_CONV), lambda i: (sidx(i), 0)),
        const((1, D_HALF)), const((1, D_HALF)), const((CONV_W, D_CONV)),
        pl.BlockSpec((None, G_A, CHUNK, CHUNK), lambda i: (jnp.where(i < npt, 0, 1), 0, 0, 0)),
        pl.BlockSpec((None, CHUNK, D_HALF), lambda i: (jnp.where(i < npt, 0, 1), 0, 0)),
        const((D_HALF, D_MODEL)), const((D_CONV, D_MODEL)), const((D_MODEL, D_MODEL)),
        const((1, D_MODEL)), const((D_MODEL, LANES)), const((1, LANES)),
    ]
    out_specs = [
        pl.BlockSpec((TM, D_MODEL), lambda i: (i, 0)),
        pl.BlockSpec((TM, D_MODEL), lambda i: (i, 0)),
        pl.BlockSpec((TM, LANES), lambda i: (i, 0)),
        pl.BlockSpec((None, CHUNK, D_HALF), lambda i: (pidx(i) // tiles_per_seq, 0, 0)),
        pl.BlockSpec((TM, D_HALF), lambda i: (sidx(i), 0)),
        pl.BlockSpec((None, CONV_W - 1, D_CONV), lambda i: (pidx(i) // tiles_per_seq, 0, 0)),
        pl.BlockSpec((TM, D_CONV), lambda i: (sidx(i), 0)),
    ]
    out_shape = [
        jax.ShapeDtypeStruct((t, D_MODEL), F32),
        jax.ShapeDtypeStruct((t, D_MODEL), F32),
        jax.ShapeDtypeStruct((t, LANES), F32),
        jax.ShapeDtypeStruct((n_batch, CHUNK, D_HALF), F32),
        jax.ShapeDtypeStruct((ts, D_HALF), F32),
        jax.ShapeDtypeStruct((n_batch, CONV_W - 1, D_CONV), F32),
        jax.ShapeDtypeStruct((ts, D_CONV), F32),
    ]
    return pl.pallas_call(
        functools.partial(_mixer_kernel, n_prompt_tiles=npt, tiles_per_seq=tiles_per_seq),
        grid=(npt + nst,),
        in_specs=in_specs,
        out_specs=out_specs,
        out_shape=out_shape,
        scratch_shapes=[
            pltpu.VMEM((8, D_CONV), F32),
            pltpu.VMEM((TM, D_HALF), BF16),
            pltpu.VMEM((TM, D_CONV), BF16),
            pltpu.VMEM((TM, D_CONV), F32),
            pltpu.VMEM((TM, D_MODEL), BF16),
        ],
        compiler_params=pltpu.CompilerParams(
            dimension_semantics=("arbitrary",), vmem_limit_bytes=VMEM_LIMIT),
        name="mixer",
    )(z, xp, xs, f, lng, lnb, cw, wsp, bsp, wa, wb, wo, gm, wr, br)


def _expert_kernel(te_ref, nused_ref, src_ref, xm_hbm, wrow_ref, wg_ref, wu_ref, wd_ref, y_ref,
                   xbuf, sem, wgb, wub, wdb):
    i = pl.program_id(0)
    nused = nused_ref[0]
    slot = i % 2

    def issue(tile, dst_slot):
        base = tile * TM_E

        def body(r, carry):
            tok = src_ref[base + r]
            pltpu.make_async_copy(xm_hbm.at[pl.ds(tok, 1), :],
                                  xbuf.at[dst_slot, pl.ds(r, 1), :], sem.at[dst_slot]).start()
            return carry

        lax.fori_loop(0, TM_E, body, 0, unroll=8)

    @pl.when(i == 0)
    def _():
        issue(0, 0)

    @pl.when(i + 1 < nused)
    def _():
        issue(i + 1, 1 - slot)

    @pl.when(i < nused)
    def _():
        new_expert = jnp.logical_or(i == 0, te_ref[i] != te_ref[jnp.maximum(i - 1, 0)])

        @pl.when(new_expert)
        def _():
            wgb[...] = wg_ref[...].astype(BF16)
            wub[...] = wu_ref[...].astype(BF16)
            wdb[...] = wd_ref[...].astype(BF16)

        pltpu.make_async_copy(xm_hbm.at[pl.ds(0, TM_E), :], xbuf.at[slot], sem.at[slot]).wait()
        x = xbuf[slot].astype(BF16)
        g = jnp.dot(x, wgb[...], preferred_element_type=F32)
        u = jnp.dot(x, wub[...], preferred_element_type=F32)
        hid = (g * _sigmoid(g)) * u * wrow_ref[...]
        y_ref[...] = jnp.dot(hid.astype(BF16), wdb[...], preferred_element_type=F32)

    @pl.when(i >= nused)
    def _():
        y_ref[...] = jnp.zeros_like(y_ref)


def _experts(te, nused, src, xm, wrow, wg, wu, wd, *, n_tiles):
    grid_spec = pltpu.PrefetchScalarGridSpec(
        num_scalar_prefetch=3,
        grid=(n_tiles,),
        in_specs=[
            pl.BlockSpec(memory_space=pl.ANY),
            pl.BlockSpec((TM_E, 1), lambda i, te, nu, sr: (i, 0)),
            pl.BlockSpec((None, None, D_MODEL, D_EXPERT), lambda i, te, nu, sr: (0, te[i], 0, 0)),
            pl.BlockSpec((None, None, D_MODEL, D_EXPERT), lambda i, te, nu, sr: (0, te[i], 0, 0)),
            pl.BlockSpec((None, None, D_EXPERT, D_MODEL), lambda i, te, nu, sr: (0, te[i], 0, 0)),
        ],
        out_specs=pl.BlockSpec((TM_E, D_MODEL), lambda i, te, nu, sr: (i, 0)),
        scratch_shapes=[
            pltpu.VMEM((2, TM_E, D_MODEL), F32),
            pltpu.SemaphoreType.DMA((2,)),
            pltpu.VMEM((D_MODEL, D_EXPERT), BF16),
            pltpu.VMEM((D_MODEL, D_EXPERT), BF16),
            pltpu.VMEM((D_EXPERT, D_MODEL), BF16),
        ],
    )
    return pl.pallas_call(
        _expert_kernel,
        grid_spec=grid_spec,
        out_shape=jax.ShapeDtypeStruct((n_tiles * TM_E, D_MODEL), F32),
        compiler_params=pltpu.CompilerParams(
            dimension_semantics=("arbitrary",), vmem_limit_bytes=VMEM_LIMIT),
        name="experts",
    )(te, nused, src, xm, wrow, wg, wu, wd)


def _tail_kernel(pos_ref, ys_hbm, h_ref, pp_ref, ps_ref, gp_ref, wpg_ref, wpp_ref, gf_ref,
                 yp_ref, ysm_ref, ybuf, sem, *, n_prompt_tiles, n_tiles):
    i = pl.program_id(0)
    slot = i % 2
    is_prompt = i < n_prompt_tiles

    def issue(tile, dst_slot):
        base = tile * TM * 2

        def body(r, carry):
            for k in range(2):
                p = pos_ref[base + 2 * r + k]
                pltpu.make_async_copy(ys_hbm.at[pl.ds(p, 1), :],
                                      ybuf.at[dst_slot, k, pl.ds(r, 1), :], sem.at[dst_slot]).start()
            return carry

        lax.fori_loop(0, TM, body, 0, unroll=4)

    @pl.when(i == 0)
    def _():
        issue(0, 0)

    @pl.when(i + 1 < n_tiles)
    def _():
        issue(i + 1, 1 - slot)

    for k in range(2):
        pltpu.make_async_copy(ys_hbm.at[pl.ds(0, TM), :], ybuf.at[slot, k], sem.at[slot]).wait()

    h = h_ref[...] + ybuf[slot, 0] + ybuf[slot, 1]
    a = _rms(h, gp_ref[...]).astype(BF16)
    gate = _sigmoid(jnp.dot(a, wpg_ref[...], preferred_element_type=F32))
    p = jnp.where(is_prompt, pp_ref[...], ps_ref[...]).astype(BF16)
    h = h + gate * jnp.dot(p, wpp_ref[...], preferred_element_type=F32)
    y = _rms(h, gf_ref[...])

    @pl.when(is_prompt)
    def _():
        yp_ref[...] = y

    @pl.when(jnp.logical_not(is_prompt))
    def _():
        ysm_ref[...] = y


def _tail(pos, ys, h, pp, ps, gp, wpg, wpp, gf):
    tp, ts = pp.shape[0], ps.shape[0]
    npt, nst = tp // TM, ts // TM
    ple = pp.shape[1]
    once = pl.Buffered(1)

    def pidx(i):
        return jnp.minimum(i, npt - 1)

    def sidx(i):
        return jnp.maximum(i - npt, 0)

    grid_spec = pltpu.PrefetchScalarGridSpec(
        num_scalar_prefetch=1,
        grid=(npt + nst,),
        in_specs=[
            pl.BlockSpec(memory_space=pl.ANY),
            pl.BlockSpec((TM, D_MODEL), lambda i, pos: (i, 0)),
            pl.BlockSpec((TM, ple), lambda i, pos: (pidx(i), 0)),
            pl.BlockSpec((TM, ple), lambda i, pos: (sidx(i), 0)),
            pl.BlockSpec((1, D_MODEL), lambda i, pos: (0, 0), pipeline_mode=once),
            pl.BlockSpec((D_MODEL, D_MODEL), lambda i, pos: (0, 0), pipeline_mode=once),
            pl.BlockSpec((ple, D_MODEL), lambda i, pos: (0, 0), pipeline_mode=once),
            pl.BlockSpec((1, D_MODEL), lambda i, pos: (0, 0), pipeline_mode=once),
        ],
        out_specs=[
            pl.BlockSpec((TM, D_MODEL), lambda i, pos: (pidx(i), 0)),
            pl.BlockSpec((TM, D_MODEL), lambda i, pos: (sidx(i), 0)),
        ],
        scratch_shapes=[
            pltpu.VMEM((2, 2, TM, D_MODEL), F32),
            pltpu.SemaphoreType.DMA((2,)),
        ],
    )
    return pl.pallas_call(
        functools.partial(_tail_kernel, n_prompt_tiles=npt, n_tiles=npt + nst),
        grid_spec=grid_spec,
        out_shape=[jax.ShapeDtypeStruct((tp, D_MODEL), F32), jax.ShapeDtypeStruct((ts, D_MODEL), F32)],
        compiler_params=pltpu.CompilerParams(
            dimension_semantics=("arbitrary",), vmem_limit_bytes=VMEM_LIMIT),
        name="tail",
    )(pos, ys, h, pp, ps, gp, wpg, wpp, gf)


def _route_tables(route, n_tiles):
    t = route.shape[0]
    e = route[:, 0:2].astype(jnp.int32).reshape(-1)
    w = route[:, 2:4].reshape(-1)
    onehot = (e[:, None] == jnp.arange(N_EXPERTS, dtype=jnp.int32)[None, :]).astype(jnp.int32)
    cum = jnp.cumsum(onehot, axis=0)
    counts = cum[-1]
    rank = jnp.sum((cum - onehot) * onehot, axis=1)
    padded = ((counts + TM_E - 1) // TM_E) * TM_E
    ends = jnp.cumsum(padded)
    offs = ends - padded
    pos = jnp.sum(onehot * offs[None, :], axis=1) + rank
    rows = n_tiles * TM_E
    tok = jnp.arange(2 * t, dtype=jnp.int32) // 2
    src = jnp.zeros((rows,), jnp.int32).at[pos].set(tok)
    wrow = jnp.zeros((rows,), F32).at[pos].set(w)
    nused = ends[-1] // TM_E
    tile_id = jnp.arange(n_tiles, dtype=jnp.int32)
    te = jnp.sum((tile_id[:, None] * TM_E >= ends[None, :]).astype(jnp.int32), axis=1)
    te_last = jnp.sum((((nused - 1) * TM_E) >= ends).astype(jnp.int32))
    te = jnp.where(tile_id < nused, te, te_last)
    return te, nused.reshape(1).astype(jnp.int32), src, wrow.reshape(rows, 1), pos.astype(jnp.int32)


def kernel(x_prompt, x_sample, state_conv, p_prompt, p_sample, g_mix, w_in, ln_v_g, ln_v_b, w_spatial, b_spatial, conv_w, w_br_a, w_br_b, w_out, g_moe, w_router_group, b_router_group, w_router_expert, b_router_expert, w_exp_gate, w_exp_up, w_exp_down, g_ple, w_ple_gate, w_ple_proj, g_final):
    depth = w_in.shape[0]
    assert depth == 1, "single-layer step only"
    nb, seq, _ = x_prompt.shape
    ns, dseq, _ = x_sample.shape
    assert dseq == 8 and seq % TM == 0 and (ns * dseq) % TM_NORM == 0 and (nb * seq) % TM_NORM == 0
    tp, ts = nb * seq, ns * dseq
    t = tp + ts
    xp = x_prompt.reshape(tp, D_MODEL)
    xs = x_sample.reshape(ts, D_MODEL)

    row = lambda a: a.reshape(1, -1)
    tril = jnp.tril(jnp.ones((CHUNK, CHUNK), dtype=bool))
    w_sp = jnp.where(tril[None], w_spatial[0], 0.0)
    eye = jnp.eye(CHUNK // dseq, dtype=F32)
    w_sp_s = jnp.einsum("ij,gts->gitjs", eye, w_sp[:, :dseq, :dseq]).reshape(G_A, CHUNK, CHUNK)
    wsp = jnp.stack([w_sp, w_sp_s]).astype(BF16)
    b_p = jnp.repeat(b_spatial[0].T, D_GROUP_A, axis=1)
    b_s = jnp.tile(b_p[:dseq], (CHUNK // dseq, 1))
    bsp = jnp.stack([b_p, b_s])
    wr = jnp.zeros((D_MODEL, LANES), F32)
    wr = wr.at[:, :N_GROUPS].set(w_router_group[0]).at[:, N_GROUPS:N_GROUPS + N_EXPERTS].set(w_router_expert[0])
    br = jnp.zeros((1, LANES), F32)
    br = br.at[0, :N_GROUPS].set(b_router_group[0]).at[0, N_GROUPS:N_GROUPS + N_EXPERTS].set(b_router_expert[0])
    f = jnp.pad(state_conv[0], ((0, 0), (0, dseq - (CONV_W - 1)), (0, 0))).reshape(ts, D_CONV)

    n = _norm(xp, xs, row(g_mix[0]))
    z = _inproj(n, w_in)
    h, xm, route, vnp, vns, qp, qs = _mixer(
        z, xp, xs, f, row(ln_v_g[0]), row(ln_v_b[0]), conv_w[0], wsp, bsp,
        w_br_a[0].astype(BF16), w_br_b[0].astype(BF16), w_out[0].astype(BF16),
        row(g_moe[0]), wr.astype(BF16), br, n_batch=nb)

    n_tiles = (2 * t) // TM_E + N_EXPERTS
    te, nused, src, wrow, pos = _route_tables(route, n_tiles)
    ys = _experts(te, nused, src, xm, wrow, w_exp_gate, w_exp_up, w_exp_down, n_tiles=n_tiles)

    yp, ysm = _tail(pos, ys, h, p_prompt[0].reshape(tp, -1), p_sample[0].reshape(ts, -1),
                    row(g_ple[0]), w_ple_gate[0].astype(BF16), w_ple_proj[0].astype(BF16), row(g_final))

    y_prompt = yp.reshape(nb, seq, D_MODEL)
    y_sample = ysm.reshape(ns, dseq, D_MODEL)
    conv_state_prompt = qp[None]
    conv_state_sample = qs.reshape(ns, dseq, D_CONV)[:, dseq - (CONV_W - 1):][None]
    v_rows_prompt = vnp[None]
    v_rows_sample = vns.reshape(ns, dseq, D_HALF)[None]
    return (y_prompt, y_sample, conv_state_prompt, conv_state_sample, v_rows_prompt, v_rows_sample)
```

```python
import functools

import jax
import jax.numpy as jnp
from jax import lax
from jax.experimental import pallas as pl
from jax.experimental.pallas import tpu as pltpu

D_MODEL = 2048
D_HALF = D_MODEL // 2
D_CONV = D_MODEL // 2
G_A = 4
D_GROUP_A = D_HALF // G_A
CHUNK = 128
CONV_W = 3
N_GROUPS = 4
EXPERTS_PER_GROUP = 8
N_EXPERTS = N_GROUPS * EXPERTS_PER_GROUP
D_EXPERT = D_MODEL // 4
D_IN_TOTAL = 2 * D_HALF + 3 * D_CONV + 2 * D_MODEL
EPS = 1e-6

BF16 = jnp.bfloat16
F32 = jnp.float32

LANES = 128
VMEM_LIMIT = 60 * 1024 * 1024

TM_NORM = 512
TN_IN = 1024
TM = 256
TM_E = 256
ROWS = D_MODEL // LANES
COPY_CHUNK = 512


def _rms(x, g):
    return x * lax.rsqrt(jnp.mean(x * x, axis=-1, keepdims=True) + EPS) * g


def _sigmoid(x):
    return 1.0 / (1.0 + jnp.exp(-x))


def _store_token_major(ref, x):
    m = x.shape[0]
    for c in range(ROWS):
        ref[pl.ds(c, m, stride=ROWS), :] = x[:, c * LANES:(c + 1) * LANES]


def _load_token_major(ref, m):
    return jnp.concatenate([ref[pl.ds(c, m, stride=ROWS), :] for c in range(ROWS)], axis=1)


def _norm_kernel(xp_ref, xs_ref, g_ref, o_ref, *, n_prompt_tiles):
    i = pl.program_id(0)

    def run(x_ref):
        o_ref[...] = _rms(x_ref[...], g_ref[...]).astype(BF16)

    @pl.when(i < n_prompt_tiles)
    def _():
        run(xp_ref)

    @pl.when(i >= n_prompt_tiles)
    def _():
        run(xs_ref)


def _norm(xp, xs, g):
    tp, ts = xp.shape[0], xs.shape[0]
    npt, nst = tp // TM_NORM, ts // TM_NORM
    return pl.pallas_call(
        functools.partial(_norm_kernel, n_prompt_tiles=npt),
        grid=(npt + nst,),
        in_specs=[
            pl.BlockSpec((TM_NORM, D_MODEL), lambda i: (jnp.minimum(i, npt - 1), 0)),
            pl.BlockSpec((TM_NORM, D_MODEL), lambda i: (jnp.maximum(i - npt, 0), 0)),
            pl.BlockSpec((1, D_MODEL), lambda i: (0, 0)),
        ],
        out_specs=pl.BlockSpec((TM_NORM, D_MODEL), lambda i: (i, 0)),
        out_shape=jax.ShapeDtypeStruct((tp + ts, D_MODEL), BF16),
        compiler_params=pltpu.CompilerParams(
            dimension_semantics=("arbitrary",), vmem_limit_bytes=VMEM_LIMIT),
        name="norm",
    )(xp, xs, g)


def _inproj_kernel(n_ref, w_ref, z_ref, wb_ref):
    @pl.when(pl.program_id(1) == 0)
    def _():
        wb_ref[...] = w_ref[...].astype(BF16)

    z_ref[...] = jnp.dot(n_ref[...], wb_ref[...], preferred_element_type=F32).astype(BF16)


def _inproj(n, w_in):
    t = n.shape[0]
    return pl.pallas_call(
        _inproj_kernel,
        grid=(D_IN_TOTAL // TN_IN, t // TM_NORM),
        in_specs=[
            pl.BlockSpec((TM_NORM, D_MODEL), lambda j, i: (i, 0)),
            pl.BlockSpec((None, D_MODEL, TN_IN), lambda j, i: (0, 0, j)),
        ],
        out_specs=pl.BlockSpec((TM_NORM, TN_IN), lambda j, i: (i, j)),
        out_shape=jax.ShapeDtypeStruct((t, D_IN_TOTAL), BF16),
        scratch_shapes=[pltpu.VMEM((D_MODEL, TN_IN), BF16)],
        compiler_params=pltpu.CompilerParams(
            dimension_semantics=("arbitrary", "arbitrary"), vmem_limit_bytes=VMEM_LIMIT),
        name="in_proj",
    )(n, w_in)


def _route(lg):
    shape = lg.shape
    lane = lax.broadcasted_iota(jnp.int32, shape, 1)
    lanef = lane.astype(F32)
    neg = jnp.float32(-jnp.inf)
    far = jnp.float32(LANES)
    is_g = lane < N_GROUPS
    lgm = jnp.where(is_g, lg, neg)
    gmax = jnp.max(lgm, axis=-1, keepdims=True)
    gsel = jnp.min(jnp.where(lgm == gmax, lanef, far), axis=-1, keepdims=True)
    p_group = 1.0 / jnp.sum(jnp.where(is_g, jnp.exp(lgm - gmax), 0.0), axis=-1, keepdims=True)
    lane_group = ((lane - N_GROUPS) >> 3).astype(F32)
    is_e = lane_group == gsel
    le = jnp.where(is_e, lg, neg)
    m1 = jnp.max(le, axis=-1, keepdims=True)
    i1 = jnp.min(jnp.where(le == m1, lanef, far), axis=-1, keepdims=True)
    le2 = jnp.where(lanef == i1, neg, le)
    m2 = jnp.max(le2, axis=-1, keepdims=True)
    i2 = jnp.min(jnp.where(le2 == m2, lanef, far), axis=-1, keepdims=True)
    e21 = jnp.exp(m2 - m1)
    w1 = 1.0 / (1.0 + e21)
    w2 = e21 * w1
    out = jnp.where(lane == 0, i1 - N_GROUPS,
                    jnp.where(lane == 1, i2 - N_GROUPS,
                              jnp.where(lane == 2, p_group * w1,
                                        jnp.where(lane == 3, p_group * w2, 0.0))))
    return out


def _mixer_kernel(z_ref, xp_ref, xs_ref, f_ref, lng_ref, lnb_ref, cw_ref, wsp_ref, bsp_ref,
                  wa_ref, wb_ref, wo_ref, gm_ref, wr_ref, br_ref,
                  h_ref, xm_ref, route_ref, vnp_ref, vns_ref, qp_ref, qs_ref,
                  qprev_ref, ya_ref, yb_ref, yc_ref, m_ref, *, n_prompt_tiles, tiles_per_seq):
    i = pl.program_id(0)
    is_prompt = i < n_prompt_tiles
    c_u, c_v, c_b, c_c, c_x = (k * D_HALF for k in range(5))
    c_ga = c_x + D_CONV
    c_gb = c_ga + D_MODEL

    v = z_ref[:, c_v:c_v + D_HALF].astype(F32)
    mu = jnp.mean(v, axis=-1, keepdims=True)
    vc = v - mu
    vn = vc * lax.rsqrt(jnp.mean(vc * vc, axis=-1, keepdims=True) + EPS) * lng_ref[...] + lnb_ref[...]
    vnb = vn.astype(BF16)
    for c in range(TM // CHUNK):
        rows = slice(c * CHUNK, (c + 1) * CHUNK)
        for g in range(G_A):
            cols = slice(g * D_GROUP_A, (g + 1) * D_GROUP_A)
            s = jnp.dot(wsp_ref[g], vnb[rows, cols], preferred_element_type=F32) + bsp_ref[:, cols]
            u = z_ref[rows, c_u + g * D_GROUP_A:c_u + (g + 1) * D_GROUP_A].astype(F32)
            ya_ref[rows, cols] = (u * s).astype(BF16)

    q = z_ref[:, c_c:c_c + D_CONV].astype(F32) * z_ref[:, c_x:c_x + D_CONV].astype(F32)
    q1 = pltpu.roll(q, 1, 0)
    q2 = pltpu.roll(q, 2, 0)
    row = lax.broadcasted_iota(jnp.int32, q.shape, 0)
    cw0, cw1, cw2 = cw_ref[0:1, :], cw_ref[1:2, :], cw_ref[2:3, :]

    @pl.when(is_prompt)
    def _():
        seq_start = i % tiles_per_seq == 0
        p6 = jnp.where(seq_start, 0.0, qprev_ref[6:7, :])
        p7 = jnp.where(seq_start, 0.0, qprev_ref[7:8, :])
        a1 = jnp.where(row == 0, p7, q1)
        a2 = jnp.where(row == 0, p6, jnp.where(row == 1, p7, q2))
        yc_ref[...] = cw0 * a2 + cw1 * a1 + cw2 * q
        qprev_ref[...] = q[TM - 8:TM, :]
        qp_ref[...] = q[TM - (CONV_W - 1):TM, :]
        vnp_ref[...] = vn[TM - CHUNK:TM, :]

    @pl.when(jnp.logical_not(is_prompt))
    def _():
        f = f_ref[...]
        f1 = pltpu.roll(f, TM - 1, 0)
        r8 = row & 7
        a1 = jnp.where(r8 == 0, f1, q1)
        a2 = jnp.where(r8 < 2, f, q2)
        yc_ref[...] = cw0 * a2 + cw1 * a1 + cw2 * q
        qs_ref[...] = q
        vns_ref[...] = vn

    yb_ref[...] = (z_ref[:, c_b:c_b + D_CONV].astype(F32) * yc_ref[...]).astype(BF16)

    half = D_MODEL // 2
    for hh in range(2):
        cols = slice(hh * half, (hh + 1) * half)
        a = jnp.dot(ya_ref[...], wa_ref[:, cols], preferred_element_type=F32)
        b = jnp.dot(yb_ref[...], wb_ref[:, cols], preferred_element_type=F32)
        ga = z_ref[:, c_ga + hh * half:c_ga + (hh + 1) * half].astype(F32)
        gb = z_ref[:, c_gb + hh * half:c_gb + (hh + 1) * half].astype(F32)
        m_ref[:, cols] = (_sigmoid(ga) * a + _sigmoid(gb) * b).astype(BF16)

    x = jnp.where(is_prompt, xp_ref[...], xs_ref[...])
    h = x + jnp.dot(m_ref[...], wo_ref[...], preferred_element_type=F32)
    h_ref[...] = h
    xm = _rms(h, gm_ref[...])
    _store_token_major(xm_ref, xm)
    lg =jnp.dot(xm.astype(BF16), wr_ref[...], preferred_element_type=F32) + br_ref[...]
    route_ref[...] = _route(lg)


def _mixer(z, xp, xs, f, lng, lnb, cw, wsp, bsp, wa, wb, wo, gm, wr, br, *, n_batch):
    tp, ts = xp.shape[0], xs.shape[0]
    t = tp + ts
    npt, nst = tp // TM, ts // TM
    tiles_per_seq = tp // n_batch // TM
    once = pl.Buffered(1)

    def pidx(i):
        return jnp.minimum(i, npt - 1)

    def sidx(i):
        return jnp.maximum(i - npt, 0)

    const = lambda shape: pl.BlockSpec(shape, lambda i: (0,) * len(shape), pipeline_mode=once)
    in_specs = [
        pl.BlockSpec((TM, D_IN_TOTAL), lambda i: (i, 0)),
        pl.BlockSpec((TM, D_MODEL), lambda i: (pidx(i), 0)),
        pl.BlockSpec((TM, D_MODEL), lambda i: (sidx(i), 0)),
        pl.BlockSpec((TM, D_CONV), lambda i: (sidx(i), 0)),
        const((1, D_HALF)), const((1, D_HALF)), const((CONV_W, D_CONV)),
        pl.BlockSpec((None, G_A, CHUNK, CHUNK), lambda i: (jnp.where(i < npt, 0, 1), 0, 0, 0)),
        pl.BlockSpec((None, CHUNK, D_HALF), lambda i: (jnp.where(i < npt, 0, 1), 0, 0)),
        const((D_HALF, D_MODEL)), const((D_CONV, D_MODEL)), const((D_MODEL, D_MODEL)),
        const((1, D_MODEL)), const((D_MODEL, LANES)), const((1, LANES)),
    ]
    out_specs = [
        pl.BlockSpec((TM, D_MODEL), lambda i: (i, 0)),
        pl.BlockSpec((TM * ROWS, LANES), lambda i: (i, 0)),
        pl.BlockSpec((TM, LANES), lambda i: (i, 0)),
        pl.BlockSpec((None, CHUNK, D_HALF), lambda i: (pidx(i) // tiles_per_seq, 0, 0)),
        pl.BlockSpec((TM, D_HALF), lambda i: (sidx(i), 0)),
        pl.BlockSpec((None, CONV_W - 1, D_CONV), lambda i: (pidx(i) // tiles_per_seq, 0, 0)),
        pl.BlockSpec((TM, D_CONV), lambda i: (sidx(i), 0)),
    ]
    out_shape = [
        jax.ShapeDtypeStruct((t, D_MODEL), F32),
        jax.ShapeDtypeStruct((t * ROWS, LANES), F32),
        jax.ShapeDtypeStruct((t, LANES), F32),
        jax.ShapeDtypeStruct((n_batch, CHUNK, D_HALF), F32),
        jax.ShapeDtypeStruct((ts, D_HALF), F32),
        jax.ShapeDtypeStruct((n_batch, CONV_W - 1, D_CONV), F32),
        jax.ShapeDtypeStruct((ts, D_CONV), F32),
    ]
    return pl.pallas_call(
        functools.partial(_mixer_kernel, n_prompt_tiles=npt, tiles_per_seq=tiles_per_seq),
        grid=(npt + nst,),
        in_specs=in_specs,
        out_specs=out_specs,
        out_shape=out_shape,
        scratch_shapes=[
            pltpu.VMEM((8, D_CONV), F32),
            pltpu.VMEM((TM, D_HALF), BF16),
            pltpu.VMEM((TM, D_CONV), BF16),
            pltpu.VMEM((TM, D_CONV), F32),
            pltpu.VMEM((TM, D_MODEL), BF16),
        ],
        compiler_params=pltpu.CompilerParams(
            dimension_semantics=("arbitrary",), vmem_limit_bytes=VMEM_LIMIT),
        name="mixer",
    )(z, xp, xs, f, lng, lnb, cw, wsp, bsp, wa, wb, wo, gm, wr, br)


def _dispatch_kernel(pos_ref, cnt_ref, off_ref, nused_ref, xm_hbm, xs_hbm, zbuf, sem, zsem,
                     *, n_assign, n_tiles):
    zbuf[...] = jnp.zeros_like(zbuf)
    nused = nused_ref[0]
    pad_bits = TM_E.bit_length() - 1

    def pad_copies(e, fn):
        cnt = cnt_ref[e]
        npad = (TM_E - cnt % TM_E) % TM_E
        start = off_ref[e] + cnt
        for b in range(pad_bits):
            size = 1 << b
            at = start + ((npad >> (b + 1)) << (b + 1))

            @pl.when(((npad >> b) & 1) == 1)
            def _():
                fn(pltpu.make_async_copy(zbuf.at[pl.ds(0, size * ROWS), :],
                                         xs_hbm.at[pl.ds(at * ROWS, size * ROWS), :], zsem))

    def tile_copy(tile):
        return pltpu.make_async_copy(zbuf, xs_hbm.at[pl.ds(tile * TM_E * ROWS, TM_E * ROWS), :], zsem)

    def loop(lo, hi, body):
        lax.fori_loop(lo, hi, lambda k, c: (body(k), c)[1], 0)

    loop(0, N_EXPERTS, lambda e: pad_copies(e, lambda cp: cp.start()))
    loop(nused, n_tiles, lambda tile: tile_copy(tile).start())

    def issue(chunk):
        base = chunk * COPY_CHUNK

        def body(j, carry):
            a = base + j
            tok = a >> 1
            p = pos_ref[a]
            pltpu.make_async_copy(xm_hbm.at[pl.ds(tok * ROWS, ROWS), :],
                                  xs_hbm.at[pl.ds(p * ROWS, ROWS), :], sem).start()
            return carry

        lax.fori_loop(0, COPY_CHUNK, body, 0, unroll=8)

    def wait_chunk():
        pltpu.make_async_copy(xm_hbm.at[pl.ds(0, COPY_CHUNK * ROWS), :],
                              xs_hbm.at[pl.ds(0, COPY_CHUNK * ROWS), :], sem).wait()

    n_chunks = n_assign // COPY_CHUNK

    def chunk_body(c):
        issue(c)

        @pl.when(c > 0)
        def _():
            wait_chunk()

    loop(0, n_chunks, chunk_body)
    wait_chunk()
    loop(0, N_EXPERTS, lambda e: pad_copies(e, lambda cp: cp.wait()))
    loop(nused, n_tiles, lambda tile: tile_copy(tile).wait())


def _dispatch(pos, cnt, off, nused, xm, *, n_tiles):
    n_assign = pos.shape[0]
    assert n_assign % COPY_CHUNK == 0
    smem = pl.BlockSpec(memory_space=pltpu.SMEM)
    return pl.pallas_call(
        functools.partial(_dispatch_kernel, n_assign=n_assign, n_tiles=n_tiles),
        in_specs=[smem, smem, smem, smem, pl.BlockSpec(memory_space=pl.ANY)],
        out_specs=pl.BlockSpec(memory_space=pl.ANY),
        out_shape=jax.ShapeDtypeStruct((n_tiles * TM_E * ROWS, LANES), F32),
        scratch_shapes=[
            pltpu.VMEM((TM_E * ROWS, LANES), F32),
            pltpu.SemaphoreType.DMA(()),
            pltpu.SemaphoreType.DMA(()),
        ],
        compiler_params=pltpu.CompilerParams(vmem_limit_bytes=VMEM_LIMIT),
        name="dispatch",
    )(pos, cnt, off, nused, xm)


def _expert_kernel(te_ref, nused_ref, xs_ref, wg_ref, wu_ref, wd_ref, y_ref, wgb, wub, wdb):
    i = pl.program_id(0)
    nused = nused_ref[0]

    @pl.when(i < nused)
    def _():
        new_expert = jnp.logical_or(i == 0, te_ref[i] != te_ref[jnp.maximum(i - 1, 0)])

        @pl.when(new_expert)
        def _():
            wgb[...] = wg_ref[...].astype(BF16)
            wub[...] = wu_ref[...].astype(BF16)
            wdb[...] = wd_ref[...].astype(BF16)

        x = _load_token_major(xs_ref, TM_E).astype(BF16)
        g = jnp.dot(x, wgb[...], preferred_element_type=F32)
        u = jnp.dot(x, wub[...], preferred_element_type=F32)
        hid = (g * _sigmoid(g)) * u
        _store_token_major(y_ref, jnp.dot(hid.astype(BF16), wdb[...], preferred_element_type=F32))

    @pl.when(i >= nused)
    def _():
        y_ref[...] = jnp.zeros_like(y_ref)


def _experts(te, nused, xs, wg, wu, wd, *, n_tiles):
    grid_spec = pltpu.PrefetchScalarGridSpec(
        num_scalar_prefetch=2,
        grid=(n_tiles,),
        in_specs=[
            pl.BlockSpec((TM_E * ROWS, LANES), lambda i, te, nu: (jnp.minimum(i, nu[0] - 1), 0)),
            pl.BlockSpec((None, None, D_MODEL, D_EXPERT), lambda i, te, nu: (0, te[i], 0, 0)),
            pl.BlockSpec((None, None, D_MODEL, D_EXPERT), lambda i, te, nu: (0, te[i], 0, 0)),
            pl.BlockSpec((None, None, D_EXPERT, D_MODEL), lambda i, te, nu: (0, te[i], 0, 0)),
        ],
        out_specs=pl.BlockSpec((TM_E * ROWS, LANES), lambda i, te, nu: (i, 0)),
        scratch_shapes=[
            pltpu.VMEM((D_MODEL, D_EXPERT), BF16),
            pltpu.VMEM((D_MODEL, D_EXPERT), BF16),
            pltpu.VMEM((D_EXPERT, D_MODEL), BF16),
        ],
    )
    return pl.pallas_call(
        _expert_kernel,
        grid_spec=grid_spec,
        out_shape=jax.ShapeDtypeStruct((n_tiles * TM_E * ROWS, LANES), F32),
        compiler_params=pltpu.CompilerParams(
            dimension_semantics=("arbitrary",), vmem_limit_bytes=VMEM_LIMIT),
        name="experts",
    )(te, nused, xs, wg, wu, wd)


def _tail_kernel(pos_ref, ys_hbm, h_ref, route_ref, pp_ref, ps_ref, gp_ref, wpg_ref, wpp_ref, gf_ref,
                 yp_ref, ysm_ref, ybuf, sem, h2_ref, *, n_prompt_tiles, n_tiles):
    i = pl.program_id(0)
    slot = i % 2
    is_prompt = i < n_prompt_tiles

    def issue(tile, dst_slot):
        base = tile * TM * 2

        def body(r, carry):
            for k in range(2):
                p = pos_ref[base + 2 * r + k]
                pltpu.make_async_copy(ys_hbm.at[pl.ds(p * ROWS, ROWS), :],
                                      ybuf.at[dst_slot, k, pl.ds(r * ROWS, ROWS), :],
                                      sem.at[dst_slot]).start()
            return carry

        lax.fori_loop(0, TM, body, 0, unroll=4)

    @pl.when(i == 0)
    def _():
        issue(0, 0)

    @pl.when(i + 1 < n_tiles)
    def _():
        issue(i + 1, 1 - slot)

    for k in range(2):
        pltpu.make_async_copy(ys_hbm.at[pl.ds(0, TM * ROWS), :], ybuf.at[slot, k], sem.at[slot]).wait()

    w0 = jnp.broadcast_to(route_ref[:, 2:3], (TM, LANES))
    w1 = jnp.broadcast_to(route_ref[:, 3:4], (TM, LANES))
    for c in range(ROWS):
        cols = slice(c * LANES, (c + 1) * LANES)
        y0 = ybuf[slot, 0, pl.ds(c, TM, stride=ROWS), :]
        y1 = ybuf[slot, 1, pl.ds(c, TM, stride=ROWS), :]
        h2_ref[:, cols] = h_ref[:, cols] + (w0 * y0 + w1 * y1)
    h = h2_ref[...]
    a = _rms(h, gp_ref[...]).astype(BF16)
    gate = _sigmoid(jnp.dot(a, wpg_ref[...], preferred_element_type=F32))
    p = jnp.where(is_prompt, pp_ref[...], ps_ref[...]).astype(BF16)
    h = h + gate * jnp.dot(p, wpp_ref[...], preferred_element_type=F32)
    y = _rms(h, gf_ref[...])

    @pl.when(is_prompt)
    def _():
        yp_ref[...] = y

    @pl.when(jnp.logical_not(is_prompt))
    def _():
        ysm_ref[...] = y


def _tail(pos, ys, h, route, pp, ps, gp, wpg, wpp, gf):
    tp, ts = pp.shape[0], ps.shape[0]
    npt, nst = tp // TM, ts // TM
    ple = pp.shape[1]
    once = pl.Buffered(1)

    def pidx(i):
        return jnp.minimum(i, npt - 1)

    def sidx(i):
        return jnp.maximum(i - npt, 0)

    grid_spec = pltpu.PrefetchScalarGridSpec(
        num_scalar_prefetch=1,
        grid=(npt + nst,),
        in_specs=[
            pl.BlockSpec(memory_space=pl.ANY),
            pl.BlockSpec((TM, D_MODEL), lambda i, pos: (i, 0)),
            pl.BlockSpec((TM, LANES), lambda i, pos: (i, 0)),
            pl.BlockSpec((TM, ple), lambda i, pos: (pidx(i), 0)),
            pl.BlockSpec((TM, ple), lambda i, pos: (sidx(i), 0)),
            pl.BlockSpec((1, D_MODEL), lambda i, pos: (0, 0), pipeline_mode=once),
            pl.BlockSpec((D_MODEL, D_MODEL), lambda i, pos: (0, 0), pipeline_mode=once),
            pl.BlockSpec((ple, D_MODEL), lambda i, pos: (0, 0), pipeline_mode=once),
            pl.BlockSpec((1, D_MODEL), lambda i, pos: (0, 0), pipeline_mode=once),
        ],
        out_specs=[
            pl.BlockSpec((TM, D_MODEL), lambda i, pos: (pidx(i), 0)),
            pl.BlockSpec((TM, D_MODEL), lambda i, pos: (sidx(i), 0)),
        ],
        scratch_shapes=[
            pltpu.VMEM((2, 2, TM * ROWS, LANES), F32),
            pltpu.SemaphoreType.DMA((2,)),
            pltpu.VMEM((TM, D_MODEL), F32),
        ],
    )
    return pl.pallas_call(
        functools.partial(_tail_kernel, n_prompt_tiles=npt, n_tiles=npt + nst),
        grid_spec=grid_spec,
        out_shape=[jax.ShapeDtypeStruct((tp, D_MODEL), F32), jax.ShapeDtypeStruct((ts, D_MODEL), F32)],
        compiler_params=pltpu.CompilerParams(
            dimension_semantics=("arbitrary",), vmem_limit_bytes=VMEM_LIMIT),
        name="tail",
    )(pos, ys, h, route, pp, ps, gp, wpg, wpp, gf)


def _route_tables(route, n_tiles):
    e = route[:, 0:2].astype(jnp.int32).reshape(-1)
    onehot = (e[:, None] == jnp.arange(N_EXPERTS, dtype=jnp.int32)[None, :]).astype(jnp.int32)
    cum = jnp.cumsum(onehot, axis=0)
    counts = cum[-1]
    rank = jnp.sum((cum - onehot) * onehot, axis=1)
    padded = ((counts + TM_E - 1) // TM_E) * TM_E
    ends = jnp.cumsum(padded)
    offs = ends - padded
    pos = jnp.sum(onehot * offs[None, :], axis=1) + rank
    nused = ends[-1] // TM_E
    tile_id = jnp.arange(n_tiles, dtype=jnp.int32)
    te = jnp.sum((tile_id[:, None] * TM_E >= ends[None, :]).astype(jnp.int32), axis=1)
    te_last = jnp.sum((((nused - 1) * TM_E) >= ends).astype(jnp.int32))
    te = jnp.where(tile_id < nused, te, te_last)
    i32 = lambda a: a.astype(jnp.int32)
    return i32(te), i32(nused).reshape(1), i32(pos), i32(counts), i32(offs)


def kernel(x_prompt, x_sample, state_conv, p_prompt, p_sample, g_mix, w_in, ln_v_g, ln_v_b, w_spatial, b_spatial, conv_w, w_br_a, w_br_b, w_out, g_moe, w_router_group, b_router_group, w_router_expert, b_router_expert, w_exp_gate, w_exp_up, w_exp_down, g_ple, w_ple_gate, w_ple_proj, g_final):
    depth = w_in.shape[0]
    assert depth == 1, "single-layer step only"
    nb, seq, _ = x_prompt.shape
    ns, dseq, _ = x_sample.shape
    assert dseq == 8 and seq % TM == 0 and (ns * dseq) % TM_NORM == 0 and (nb * seq) % TM_NORM == 0
    tp, ts = nb * seq, ns * dseq
    t = tp + ts
    xp = x_prompt.reshape(tp, D_MODEL)
    xs = x_sample.reshape(ts, D_MODEL)

    row = lambda a: a.reshape(1, -1)
    tril = jnp.tril(jnp.ones((CHUNK, CHUNK), dtype=bool))
    w_sp = jnp.where(tril[None], w_spatial[0], 0.0)
    eye = jnp.eye(CHUNK // dseq, dtype=F32)
    w_sp_s = jnp.einsum("ij,gts->gitjs", eye, w_sp[:, :dseq, :dseq]).reshape(G_A, CHUNK, CHUNK)
    wsp = jnp.stack([w_sp, w_sp_s]).astype(BF16)
    b_p = jnp.repeat(b_spatial[0].T, D_GROUP_A, axis=1)
    b_s = jnp.tile(b_p[:dseq], (CHUNK // dseq, 1))
    bsp = jnp.stack([b_p, b_s])
    wr = jnp.zeros((D_MODEL, LANES), F32)
    wr = wr.at[:, :N_GROUPS].set(w_router_group[0]).at[:, N_GROUPS:N_GROUPS + N_EXPERTS].set(w_router_expert[0])
    br = jnp.zeros((1, LANES), F32)
    br = br.at[0, :N_GROUPS].set(b_router_group[0]).at[0, N_GROUPS:N_GROUPS + N_EXPERTS].set(b_router_expert[0])
    f = jnp.pad(state_conv[0], ((0, 0), (0, dseq - (CONV_W - 1)), (0, 0))).reshape(ts, D_CONV)

    n = _norm(xp, xs, row(g_mix[0]))
    z = _inproj(n, w_in)
    h, xm, route, vnp, vns, qp, qs = _mixer(
        z, xp, xs, f, row(ln_v_g[0]), row(ln_v_b[0]), conv_w[0], wsp, bsp,
        w_br_a[0].astype(BF16), w_br_b[0].astype(BF16), w_out[0].astype(BF16),
        row(g_moe[0]), wr.astype(BF16), br, n_batch=nb)

    n_tiles = (2 * t) // TM_E + N_EXPERTS
    te, nused, pos, counts, offs = _route_tables(route, n_tiles)
    xsort = _dispatch(pos, counts, offs, nused, xm, n_tiles=n_tiles)
    ys = _experts(te, nused, xsort, w_exp_gate, w_exp_up, w_exp_down, n_tiles=n_tiles)

    yp, ysm = _tail(pos, ys, h, route, p_prompt[0].reshape(tp, -1), p_sample[0].reshape(ts, -1),
                    row(g_ple[0]), w_ple_gate[0].astype(BF16), w_ple_proj[0].astype(BF16), row(g_final))

    y_prompt = yp.reshape(nb, seq, D_MODEL)
    y_sample = ysm.reshape(ns, dseq, D_MODEL)
    conv_state_prompt = qp[None]
    conv_state_sample = qs.reshape(ns, dseq, D_CONV)[:, dseq - (CONV_W - 1):][None]
    v_rows_prompt = vnp[None]
    v_rows_sample = vns.reshape(ns, dseq, D_HALF)[None]
    return (y_prompt, y_sample, conv_state_prompt, conv_state_sample, v_rows_prompt, v_rows_sample)
```

```python
import functools

import jax
import jax.numpy as jnp
from jax import lax
from jax.experimental import pallas as pl
from jax.experimental.pallas import tpu as pltpu

D_MODEL = 2048
D_HALF = D_MODEL // 2
D_CONV = D_MODEL // 2
G_A = 4
D_GROUP_A = D_HALF // G_A
CHUNK = 128
CONV_W = 3
N_GROUPS = 4
EXPERTS_PER_GROUP = 8
N_EXPERTS = N_GROUPS * EXPERTS_PER_GROUP
D_EXPERT = D_MODEL // 4
D_IN_TOTAL = 2 * D_HALF + 3 * D_CONV + 2 * D_MODEL
EPS = 1e-6

BF16 = jnp.bfloat16
F32 = jnp.float32

LANES = 128
VMEM_LIMIT = 60 * 1024 * 1024

TM_NORM = 512
TN_IN = 1024
TM = 256
TM_E = 256
ROWS = D_MODEL // LANES
COPY_TILE = 256
DISPATCH_SLOTS = 3


def _rms(x, g):
    return x * lax.rsqrt(jnp.mean(x * x, axis=-1, keepdims=True) + EPS) * g


def _sigmoid(x):
    return 1.0 / (1.0 + jnp.exp(-x))


def _store_token_major(ref, x):
    m = x.shape[0]
    for c in range(ROWS):
        ref[pl.ds(c, m, stride=ROWS), :] = x[:, c * LANES:(c + 1) * LANES]


def _load_token_major(ref, m):
    return jnp.concatenate([ref[pl.ds(c, m, stride=ROWS), :] for c in range(ROWS)], axis=1)


def _norm_kernel(xp_ref, xs_ref, g_ref, o_ref, *, n_prompt_tiles):
    i = pl.program_id(0)

    def run(x_ref):
        o_ref[...] = _rms(x_ref[...], g_ref[...]).astype(BF16)

    @pl.when(i < n_prompt_tiles)
    def _():
        run(xp_ref)

    @pl.when(i >= n_prompt_tiles)
    def _():
        run(xs_ref)


def _norm(xp, xs, g):
    tp, ts = xp.shape[0], xs.shape[0]
    npt, nst = tp // TM_NORM, ts // TM_NORM
    return pl.pallas_call(
        functools.partial(_norm_kernel, n_prompt_tiles=npt),
        grid=(npt + nst,),
        in_specs=[
            pl.BlockSpec((TM_NORM, D_MODEL), lambda i: (jnp.minimum(i, npt - 1), 0)),
            pl.BlockSpec((TM_NORM, D_MODEL), lambda i: (jnp.maximum(i - npt, 0), 0)),
            pl.BlockSpec((1, D_MODEL), lambda i: (0, 0)),
        ],
        out_specs=pl.BlockSpec((TM_NORM, D_MODEL), lambda i: (i, 0)),
        out_shape=jax.ShapeDtypeStruct((tp + ts, D_MODEL), BF16),
        compiler_params=pltpu.CompilerParams(
            dimension_semantics=("arbitrary",), vmem_limit_bytes=VMEM_LIMIT),
        name="norm",
    )(xp, xs, g)


def _inproj_kernel(n_ref, w_ref, z_ref, wb_ref):
    @pl.when(pl.program_id(1) == 0)
    def _():
        wb_ref[...] = w_ref[...].astype(BF16)

    z_ref[...] = jnp.dot(n_ref[...], wb_ref[...], preferred_element_type=F32).astype(BF16)


def _inproj(n, w_in):
    t = n.shape[0]
    return pl.pallas_call(
        _inproj_kernel,
        grid=(D_IN_TOTAL // TN_IN, t // TM_NORM),
        in_specs=[
            pl.BlockSpec((TM_NORM, D_MODEL), lambda j, i: (i, 0)),
            pl.BlockSpec((None, D_MODEL, TN_IN), lambda j, i: (0, 0, j)),
        ],
        out_specs=pl.BlockSpec((TM_NORM, TN_IN), lambda j, i: (i, j)),
        out_shape=jax.ShapeDtypeStruct((t, D_IN_TOTAL), BF16),
        scratch_shapes=[pltpu.VMEM((D_MODEL, TN_IN), BF16)],
        compiler_params=pltpu.CompilerParams(
            dimension_semantics=("arbitrary", "arbitrary"), vmem_limit_bytes=VMEM_LIMIT),
        name="in_proj",
    )(n, w_in)


def _route(lg):
    shape = lg.shape
    lane = lax.broadcasted_iota(jnp.int32, shape, 1)
    lanef = lane.astype(F32)
    neg = jnp.float32(-jnp.inf)
    far = jnp.float32(LANES)
    is_g = lane < N_GROUPS
    lgm = jnp.where(is_g, lg, neg)
    gmax = jnp.max(lgm, axis=-1, keepdims=True)
    gsel = jnp.min(jnp.where(lgm == gmax, lanef, far), axis=-1, keepdims=True)
    p_group = 1.0 / jnp.sum(jnp.where(is_g, jnp.exp(lgm - gmax), 0.0), axis=-1, keepdims=True)
    lane_group = ((lane - N_GROUPS) >> 3).astype(F32)
    is_e = lane_group == gsel
    le = jnp.where(is_e, lg, neg)
    m1 = jnp.max(le, axis=-1, keepdims=True)
    i1 = jnp.min(jnp.where(le == m1, lanef, far), axis=-1, keepdims=True)
    le2 = jnp.where(lanef == i1, neg, le)
    m2 = jnp.max(le2, axis=-1, keepdims=True)
    i2 = jnp.min(jnp.where(le2 == m2, lanef, far), axis=-1, keepdims=True)
    e21 = jnp.exp(m2 - m1)
    w1 = 1.0 / (1.0 + e21)
    w2 = e21 * w1
    out = jnp.where(lane == 0, i1 - N_GROUPS,
                    jnp.where(lane == 1, i2 - N_GROUPS,
                              jnp.where(lane == 2, p_group * w1,
                                        jnp.where(lane == 3, p_group * w2, 0.0))))
    return out


def _mixer_kernel(z_ref, xp_ref, xs_ref, f_ref, lng_ref, lnb_ref, cw_ref, wsp_ref, bsp_ref,
                  wa_ref, wb_ref, wo_ref, gm_ref, wr_ref, br_ref,
                  h_ref, xm_ref, route_ref, vnp_ref, vns_ref, qp_ref, qs_ref,
                  qprev_ref, ya_ref, yb_ref, yc_ref, m_ref, *, n_prompt_tiles, tiles_per_seq):
    i = pl.program_id(0)
    is_prompt = i < n_prompt_tiles
    c_u, c_v, c_b, c_c, c_x = (k * D_HALF for k in range(5))
    c_ga = c_x + D_CONV
    c_gb = c_ga + D_MODEL

    v = z_ref[:, c_v:c_v + D_HALF].astype(F32)
    mu = jnp.mean(v, axis=-1, keepdims=True)
    vc = v - mu
    vn = vc * lax.rsqrt(jnp.mean(vc * vc, axis=-1, keepdims=True) + EPS) * lng_ref[...] + lnb_ref[...]
    vnb = vn.astype(BF16)
    for c in range(TM // CHUNK):
        rows = slice(c * CHUNK, (c + 1) * CHUNK)
        for g in range(G_A):
            cols = slice(g * D_GROUP_A, (g + 1) * D_GROUP_A)
            s = jnp.dot(wsp_ref[g], vnb[rows, cols], preferred_element_type=F32) + bsp_ref[:, cols]
            u = z_ref[rows, c_u + g * D_GROUP_A:c_u + (g + 1) * D_GROUP_A].astype(F32)
            ya_ref[rows, cols] = (u * s).astype(BF16)

    q = z_ref[:, c_c:c_c + D_CONV].astype(F32) * z_ref[:, c_x:c_x + D_CONV].astype(F32)
    q1 = pltpu.roll(q, 1, 0)
    q2 = pltpu.roll(q, 2, 0)
    row = lax.broadcasted_iota(jnp.int32, q.shape, 0)
    cw0, cw1, cw2 = cw_ref[0:1, :], cw_ref[1:2, :], cw_ref[2:3, :]

    @pl.when(is_prompt)
    def _():
        seq_start = i % tiles_per_seq == 0
        p6 = jnp.where(seq_start, 0.0, qprev_ref[6:7, :])
        p7 = jnp.where(seq_start, 0.0, qprev_ref[7:8, :])
        a1 = jnp.where(row == 0, p7, q1)
        a2 = jnp.where(row == 0, p6, jnp.where(row == 1, p7, q2))
        yc_ref[...] = cw0 * a2 + cw1 * a1 + cw2 * q
        qprev_ref[...] = q[TM - 8:TM, :]
        qp_ref[...] = q[TM - (CONV_W - 1):TM, :]
        vnp_ref[...] = vn[TM - CHUNK:TM, :]

    @pl.when(jnp.logical_not(is_prompt))
    def _():
        f = f_ref[...]
        f1 = pltpu.roll(f, TM - 1, 0)
        r8 = row & 7
        a1 = jnp.where(r8 == 0, f1, q1)
        a2 = jnp.where(r8 < 2, f, q2)
        yc_ref[...] = cw0 * a2 + cw1 * a1 + cw2 * q
        qs_ref[...] = q
        vns_ref[...] = vn

    yb_ref[...] = (z_ref[:, c_b:c_b + D_CONV].astype(F32) * yc_ref[...]).astype(BF16)

    half = D_MODEL // 2
    for hh in range(2):
        cols = slice(hh * half, (hh + 1) * half)
        a = jnp.dot(ya_ref[...], wa_ref[:, cols], preferred_element_type=F32)
        b = jnp.dot(yb_ref[...], wb_ref[:, cols], preferred_element_type=F32)
        ga = z_ref[:, c_ga + hh * half:c_ga + (hh + 1) * half].astype(F32)
        gb = z_ref[:, c_gb + hh * half:c_gb + (hh + 1) * half].astype(F32)
        m_ref[:, cols] = (_sigmoid(ga) * a + _sigmoid(gb) * b).astype(BF16)

    x = jnp.where(is_prompt, xp_ref[...], xs_ref[...])
    h = x + jnp.dot(m_ref[...], wo_ref[...], preferred_element_type=F32)
    h_ref[...] = h
    xm = _rms(h, gm_ref[...])
    _store_token_major(xm_ref, xm)
    lg =jnp.dot(xm.astype(BF16), wr_ref[...], preferred_element_type=F32) + br_ref[...]
    route_ref[...] = _route(lg)


def _mixer(z, xp, xs, f, lng, lnb, cw, wsp, bsp, wa, wb, wo, gm, wr, br, *, n_batch):
    tp, ts = xp.shape[0], xs.shape[0]
    t = tp + ts
    npt, nst = tp // TM, ts // TM
    tiles_per_seq = tp // n_batch // TM
    once = pl.Buffered(1)

    def pidx(i):
        return jnp.minimum(i, npt - 1)

    def sidx(i):
        return jnp.maximum(i - npt, 0)

    const = lambda shape: pl.BlockSpec(shape, lambda i: (0,) * len(shape), pipeline_mode=once)
    in_specs = [
        pl.BlockSpec((TM, D_IN_TOTAL), lambda i: (i, 0)),
        pl.BlockSpec((TM, D_MODEL), lambda i: (pidx(i), 0)),
        pl.BlockSpec((TM, D_MODEL), lambda i: (sidx(i), 0)),
        pl.BlockSpec((TM, D_CONV), lambda i: (sidx(i), 0)),
        const((1, D_HALF)), const((1, D_HALF)), const((CONV_W, D_CONV)),
        pl.BlockSpec((None, G_A, CHUNK, CHUNK), lambda i: (jnp.where(i < npt, 0, 1), 0, 0, 0)),
        pl.BlockSpec((None, CHUNK, D_HALF), lambda i: (jnp.where(i < npt, 0, 1), 0, 0)),
        const((D_HALF, D_MODEL)), const((D_CONV, D_MODEL)), const((D_MODEL, D_MODEL)),
        const((1, D_MODEL)), const((D_MODEL, LANES)), const((1, LANES)),
    ]
    out_specs = [
        pl.BlockSpec((TM, D_MODEL), lambda i: (i, 0)),
        pl.BlockSpec((TM * ROWS, LANES), lambda i: (i, 0)),
        pl.BlockSpec((TM, LANES), lambda i: (i, 0)),
        pl.BlockSpec((None, CHUNK, D_HALF), lambda i: (pidx(i) // tiles_per_seq, 0, 0)),
        pl.BlockSpec((TM, D_HALF), lambda i: (sidx(i), 0)),
        pl.BlockSpec((None, CONV_W - 1, D_CONV), lambda i: (pidx(i) // tiles_per_seq, 0, 0)),
        pl.BlockSpec((TM, D_CONV), lambda i: (sidx(i), 0)),
    ]
    out_shape = [
        jax.ShapeDtypeStruct((t, D_MODEL), F32),
        jax.ShapeDtypeStruct((t * ROWS, LANES), F32),
        jax.ShapeDtypeStruct((t, LANES), F32),
        jax.ShapeDtypeStruct((n_batch, CHUNK, D_HALF), F32),
        jax.ShapeDtypeStruct((ts, D_HALF), F32),
        jax.ShapeDtypeStruct((n_batch, CONV_W - 1, D_CONV), F32),
        jax.ShapeDtypeStruct((ts, D_CONV), F32),
    ]
    return pl.pallas_call(
        functools.partial(_mixer_kernel, n_prompt_tiles=npt, tiles_per_seq=tiles_per_seq),
        grid=(npt + nst,),
        in_specs=in_specs,
        out_specs=out_specs,
        out_shape=out_shape,
        scratch_shapes=[
            pltpu.VMEM((8, D_CONV), F32),
            pltpu.VMEM((TM, D_HALF), BF16),
            pltpu.VMEM((TM, D_CONV), BF16),
            pltpu.VMEM((TM, D_CONV), F32),
            pltpu.VMEM((TM, D_MODEL), BF16),
        ],
        compiler_params=pltpu.CompilerParams(
            dimension_semantics=("arbitrary",), vmem_limit_bytes=VMEM_LIMIT),
        name="mixer",
    )(z, xp, xs, f, lng, lnb, cw, wsp, bsp, wa, wb, wo, gm, wr, br)


def _dispatch_kernel(pos_ref, cnt_ref, off_ref, nused_ref, xm_hbm, xs_hbm, inbuf, zbuf, isem, rsem, zsem,
                     *, n_steps, n_tiles):
    zbuf[...] = jnp.zeros_like(zbuf)
    nused = nused_ref[0]
    pad_bits = TM_E.bit_length() - 1

    def pad_copies(e, fn):
        cnt = cnt_ref[e]
        npad = (TM_E - cnt % TM_E) % TM_E
        start = off_ref[e] + cnt
        for b in range(pad_bits):
            size = 1 << b
            at = start + ((npad >> (b + 1)) << (b + 1))

            @pl.when(((npad >> b) & 1) == 1)
            def _():
                fn(pltpu.make_async_copy(zbuf.at[pl.ds(0, size * ROWS), :],
                                         xs_hbm.at[pl.ds(at * ROWS, size * ROWS), :], zsem))

    def tile_copy(tile):
        return pltpu.make_async_copy(zbuf, xs_hbm.at[pl.ds(tile * TM_E * ROWS, TM_E * ROWS), :], zsem)

    def loop(lo, hi, body):
        lax.fori_loop(lo, hi, lambda k, c: (body(k), c)[1], 0)

    loop(0, N_EXPERTS, lambda e: pad_copies(e, lambda cp: cp.start()))
    loop(nused, n_tiles, lambda tile: tile_copy(tile).start())

    def in_copy(step, slot):
        return pltpu.make_async_copy(xm_hbm.at[pl.ds(step * COPY_TILE * ROWS, COPY_TILE * ROWS), :],
                                     inbuf.at[slot], isem.at[slot])

    def rows_wait(slot):
        for _ in range(2):
            pltpu.make_async_copy(inbuf.at[slot], xs_hbm.at[pl.ds(0, COPY_TILE * ROWS), :], rsem.at[slot]).wait()

    in_copy(0, 0).start()

    def step(s):
        slot = s % DISPATCH_SLOTS
        nxt = (s + 1) % DISPATCH_SLOTS

        @pl.when(s >= DISPATCH_SLOTS - 1)
        def _():
            rows_wait(nxt)

        @pl.when(s + 1 < n_steps)
        def _():
            in_copy(s + 1, nxt).start()

        in_copy(s, slot).wait()
        base = s * COPY_TILE * 2

        def body(r, carry):
            for k in range(2):
                p = pos_ref[base + 2 * r + k]
                pltpu.make_async_copy(inbuf.at[slot, pl.ds(r * ROWS, ROWS), :],
                                      xs_hbm.at[pl.ds(p * ROWS, ROWS), :], rsem.at[slot]).start()
            return carry

        lax.fori_loop(0, COPY_TILE, body, 0, unroll=4)

    loop(0, n_steps, step)
    for back in range(DISPATCH_SLOTS - 1, 0, -1):
        rows_wait((n_steps - back) % DISPATCH_SLOTS)
    loop(0, N_EXPERTS, lambda e: pad_copies(e, lambda cp: cp.wait()))
    loop(nused, n_tiles, lambda tile: tile_copy(tile).wait())


def _dispatch(pos, cnt, off, nused, xm, *, n_tiles):
    n_tokens = xm.shape[0] // ROWS
    assert n_tokens % COPY_TILE == 0 and n_tokens // COPY_TILE >= DISPATCH_SLOTS
    smem = pl.BlockSpec(memory_space=pltpu.SMEM)
    return pl.pallas_call(
        functools.partial(_dispatch_kernel, n_steps=n_tokens // COPY_TILE, n_tiles=n_tiles),
        in_specs=[smem, smem, smem, smem, pl.BlockSpec(memory_space=pl.ANY)],
        out_specs=pl.BlockSpec(memory_space=pl.ANY),
        out_shape=jax.ShapeDtypeStruct((n_tiles * TM_E * ROWS, LANES), F32),
        scratch_shapes=[
            pltpu.VMEM((DISPATCH_SLOTS, COPY_TILE * ROWS, LANES), F32),
            pltpu.VMEM((TM_E * ROWS, LANES), F32),
            pltpu.SemaphoreType.DMA((DISPATCH_SLOTS,)),
            pltpu.SemaphoreType.DMA((DISPATCH_SLOTS,)),
            pltpu.SemaphoreType.DMA(()),
        ],
        compiler_params=pltpu.CompilerParams(vmem_limit_bytes=VMEM_LIMIT),
        name="dispatch",
    )(pos, cnt, off, nused, xm)


def _expert_kernel(te_ref, nused_ref, xs_ref, wg_ref, wu_ref, wd_ref, y_ref, wgb, wub, wdb):
    i = pl.program_id(0)
    nused = nused_ref[0]

    @pl.when(i < nused)
    def _():
        new_expert = jnp.logical_or(i == 0, te_ref[i] != te_ref[jnp.maximum(i - 1, 0)])

        @pl.when(new_expert)
        def _():
            wgb[...] = wg_ref[...].astype(BF16)
            wub[...] = wu_ref[...].astype(BF16)
            wdb[...] = wd_ref[...].astype(BF16)

        x = _load_token_major(xs_ref, TM_E).astype(BF16)
        g = jnp.dot(x, wgb[...], preferred_element_type=F32)
        u = jnp.dot(x, wub[...], preferred_element_type=F32)
        hid = (g * _sigmoid(g)) * u
        _store_token_major(y_ref, jnp.dot(hid.astype(BF16), wdb[...], preferred_element_type=F32))

    @pl.when(i >= nused)
    def _():
        y_ref[...] = jnp.zeros_like(y_ref)


def _experts(te, nused, xs, wg, wu, wd, *, n_tiles):
    grid_spec = pltpu.PrefetchScalarGridSpec(
        num_scalar_prefetch=2,
        grid=(n_tiles,),
        in_specs=[
            pl.BlockSpec((TM_E * ROWS, LANES), lambda i, te, nu: (jnp.minimum(i, nu[0] - 1), 0)),
            pl.BlockSpec((None, None, D_MODEL, D_EXPERT), lambda i, te, nu: (0, te[i], 0, 0)),
            pl.BlockSpec((None, None, D_MODEL, D_EXPERT), lambda i, te, nu: (0, te[i], 0, 0)),
            pl.BlockSpec((None, None, D_EXPERT, D_MODEL), lambda i, te, nu: (0, te[i], 0, 0)),
        ],
        out_specs=pl.BlockSpec((TM_E * ROWS, LANES), lambda i, te, nu: (i, 0)),
        scratch_shapes=[
            pltpu.VMEM((D_MODEL, D_EXPERT), BF16),
            pltpu.VMEM((D_MODEL, D_EXPERT), BF16),
            pltpu.VMEM((D_EXPERT, D_MODEL), BF16),
        ],
    )
    return pl.pallas_call(
        _expert_kernel,
        grid_spec=grid_spec,
        out_shape=jax.ShapeDtypeStruct((n_tiles * TM_E * ROWS, LANES), F32),
        compiler_params=pltpu.CompilerParams(
            dimension_semantics=("arbitrary",), vmem_limit_bytes=VMEM_LIMIT),
        name="experts",
    )(te, nused, xs, wg, wu, wd)


def _tail_kernel(pos_ref, ys_hbm, h_ref, route_ref, pp_ref, ps_ref, gp_ref, wpg_ref, wpp_ref, gf_ref,
                 yp_ref, ysm_ref, ybuf, sem, h2_ref, *, n_prompt_tiles, n_tiles):
    i = pl.program_id(0)
    slot = i % 2
    is_prompt = i < n_prompt_tiles

    def issue(tile, dst_slot):
        base = tile * TM * 2

        def body(r, carry):
            for k in range(2):
                p = pos_ref[base + 2 * r + k]
                pltpu.make_async_copy(ys_hbm.at[pl.ds(p * ROWS, ROWS), :],
                                      ybuf.at[dst_slot, k, pl.ds(r * ROWS, ROWS), :],
                                      sem.at[dst_slot]).start()
            return carry

        lax.fori_loop(0, TM, body, 0, unroll=4)

    @pl.when(i == 0)
    def _():
        issue(0, 0)

    @pl.when(i + 1 < n_tiles)
    def _():
        issue(i + 1, 1 - slot)

    for k in range(2):
        pltpu.make_async_copy(ys_hbm.at[pl.ds(0, TM * ROWS), :], ybuf.at[slot, k], sem.at[slot]).wait()

    w0 = jnp.broadcast_to(route_ref[:, 2:3], (TM, LANES))
    w1 = jnp.broadcast_to(route_ref[:, 3:4], (TM, LANES))
    for c in range(ROWS):
        cols = slice(c * LANES, (c + 1) * LANES)
        y0 = ybuf[slot, 0, pl.ds(c, TM, stride=ROWS), :]
        y1 = ybuf[slot, 1, pl.ds(c, TM, stride=ROWS), :]
        h2_ref[:, cols] = h_ref[:, cols] + (w0 * y0 + w1 * y1)
    h = h2_ref[...]
    a = _rms(h, gp_ref[...]).astype(BF16)
    gate = _sigmoid(jnp.dot(a, wpg_ref[...], preferred_element_type=F32))
    p = jnp.where(is_prompt, pp_ref[...], ps_ref[...]).astype(BF16)
    h = h + gate * jnp.dot(p, wpp_ref[...], preferred_element_type=F32)
    y = _rms(h, gf_ref[...])

    @pl.when(is_prompt)
    def _():
        yp_ref[...] = y

    @pl.when(jnp.logical_not(is_prompt))
    def _():
        ysm_ref[...] = y


def _tail(pos, ys, h, route, pp, ps, gp, wpg, wpp, gf):
    tp, ts = pp.shape[0], ps.shape[0]
    npt, nst = tp // TM, ts // TM
    ple = pp.shape[1]
    once = pl.Buffered(1)

    def pidx(i):
        return jnp.minimum(i, npt - 1)

    def sidx(i):
        return jnp.maximum(i - npt, 0)

    grid_spec = pltpu.PrefetchScalarGridSpec(
        num_scalar_prefetch=1,
        grid=(npt + nst,),
        in_specs=[
            pl.BlockSpec(memory_space=pl.ANY),
            pl.BlockSpec((TM, D_MODEL), lambda i, pos: (i, 0)),
            pl.BlockSpec((TM, LANES), lambda i, pos: (i, 0)),
            pl.BlockSpec((TM, ple), lambda i, pos: (pidx(i), 0)),
            pl.BlockSpec((TM, ple), lambda i, pos: (sidx(i), 0)),
            pl.BlockSpec((1, D_MODEL), lambda i, pos: (0, 0), pipeline_mode=once),
            pl.BlockSpec((D_MODEL, D_MODEL), lambda i, pos: (0, 0), pipeline_mode=once),
            pl.BlockSpec((ple, D_MODEL), lambda i, pos: (0, 0), pipeline_mode=once),
            pl.BlockSpec((1, D_MODEL), lambda i, pos: (0, 0), pipeline_mode=once),
        ],
        out_specs=[
            pl.BlockSpec((TM, D_MODEL), lambda i, pos: (pidx(i), 0)),
            pl.BlockSpec((TM, D_MODEL), lambda i, pos: (sidx(i), 0)),
        ],
        scratch_shapes=[
            pltpu.VMEM((2, 2, TM * ROWS, LANES), F32),
            pltpu.SemaphoreType.DMA((2,)),
            pltpu.VMEM((TM, D_MODEL), F32),
        ],
    )
    return pl.pallas_call(
        functools.partial(_tail_kernel, n_prompt_tiles=npt, n_tiles=npt + nst),
        grid_spec=grid_spec,
        out_shape=[jax.ShapeDtypeStruct((tp, D_MODEL), F32), jax.ShapeDtypeStruct((ts, D_MODEL), F32)],
        compiler_params=pltpu.CompilerParams(
            dimension_semantics=("arbitrary",), vmem_limit_bytes=VMEM_LIMIT),
        name="tail",
    )(pos, ys, h, route, pp, ps, gp, wpg, wpp, gf)


def _route_tables(route, n_tiles):
    e = route[:, 0:2].astype(jnp.int32).reshape(-1)
    onehot = (e[:, None] == jnp.arange(N_EXPERTS, dtype=jnp.int32)[None, :]).astype(jnp.int32)
    cum = jnp.cumsum(onehot, axis=0)
    counts = cum[-1]
    rank = jnp.sum((cum - onehot) * onehot, axis=1)
    padded = ((counts + TM_E - 1) // TM_E) * TM_E
    ends = jnp.cumsum(padded)
    offs = ends - padded
    pos = jnp.sum(onehot * offs[None, :], axis=1) + rank
    nused = ends[-1] // TM_E
    tile_id = jnp.arange(n_tiles, dtype=jnp.int32)
    te = jnp.sum((tile_id[:, None] * TM_E >= ends[None, :]).astype(jnp.int32), axis=1)
    te_last = jnp.sum((((nused - 1) * TM_E) >= ends).astype(jnp.int32))
    te = jnp.where(tile_id < nused, te, te_last)
    i32 = lambda a: a.astype(jnp.int32)
    return i32(te), i32(nused).reshape(1), i32(pos), i32(counts), i32(offs)


def kernel(x_prompt, x_sample, state_conv, p_prompt, p_sample, g_mix, w_in, ln_v_g, ln_v_b, w_spatial, b_spatial, conv_w, w_br_a, w_br_b, w_out, g_moe, w_router_group, b_router_group, w_router_expert, b_router_expert, w_exp_gate, w_exp_up, w_exp_down, g_ple, w_ple_gate, w_ple_proj, g_final):
    depth = w_in.shape[0]
    assert depth == 1, "single-layer step only"
    nb, seq, _ = x_prompt.shape
    ns, dseq, _ = x_sample.shape
    assert dseq == 8 and seq % TM == 0 and (ns * dseq) % TM_NORM == 0 and (nb * seq) % TM_NORM == 0
    tp, ts = nb * seq, ns * dseq
    t = tp + ts
    xp = x_prompt.reshape(tp, D_MODEL)
    xs = x_sample.reshape(ts, D_MODEL)

    row = lambda a: a.reshape(1, -1)
    tril = jnp.tril(jnp.ones((CHUNK, CHUNK), dtype=bool))
    w_sp = jnp.where(tril[None], w_spatial[0], 0.0)
    eye = jnp.eye(CHUNK // dseq, dtype=F32)
    w_sp_s = jnp.einsum("ij,gts->gitjs", eye, w_sp[:, :dseq, :dseq]).reshape(G_A, CHUNK, CHUNK)
    wsp = jnp.stack([w_sp, w_sp_s]).astype(BF16)
    b_p = jnp.repeat(b_spatial[0].T, D_GROUP_A, axis=1)
    b_s = jnp.tile(b_p[:dseq], (CHUNK // dseq, 1))
    bsp = jnp.stack([b_p, b_s])
    wr = jnp.zeros((D_MODEL, LANES), F32)
    wr = wr.at[:, :N_GROUPS].set(w_router_group[0]).at[:, N_GROUPS:N_GROUPS + N_EXPERTS].set(w_router_expert[0])
    br = jnp.zeros((1, LANES), F32)
    br = br.at[0, :N_GROUPS].set(b_router_group[0]).at[0, N_GROUPS:N_GROUPS + N_EXPERTS].set(b_router_expert[0])
    f = jnp.pad(state_conv[0], ((0, 0), (0, dseq - (CONV_W - 1)), (0, 0))).reshape(ts, D_CONV)

    n = _norm(xp, xs, row(g_mix[0]))
    z = _inproj(n, w_in)
    h, xm, route, vnp, vns, qp, qs = _mixer(
        z, xp, xs, f, row(ln_v_g[0]), row(ln_v_b[0]), conv_w[0], wsp, bsp,
        w_br_a[0].astype(BF16), w_br_b[0].astype(BF16), w_out[0].astype(BF16),
        row(g_moe[0]), wr.astype(BF16), br, n_batch=nb)

    n_tiles = (2 * t) // TM_E + N_EXPERTS
    te, nused, pos, counts, offs = _route_tables(route, n_tiles)
    xsort = _dispatch(pos, counts, offs, nused, xm, n_tiles=n_tiles)
    ys = _experts(te, nused, xsort, w_exp_gate, w_exp_up, w_exp_down, n_tiles=n_tiles)

    yp, ysm = _tail(pos, ys, h, route, p_prompt[0].reshape(tp, -1), p_sample[0].reshape(ts, -1),
                    row(g_ple[0]), w_ple_gate[0].astype(BF16), w_ple_proj[0].astype(BF16), row(g_final))

    y_prompt = yp.reshape(nb, seq, D_MODEL)
    y_sample = ysm.reshape(ns, dseq, D_MODEL)
    conv_state_prompt = qp[None]
    conv_state_sample = qs.reshape(ns, dseq, D_CONV)[:, dseq - (CONV_W - 1):][None]
    v_rows_prompt = vnp[None]
    v_rows_sample = vns.reshape(ns, dseq, D_HALF)[None]
    return (y_prompt, y_sample, conv_state_prompt, conv_state_sample, v_rows_prompt, v_rows_sample)
```

```python
import functools

import jax
import jax.numpy as jnp
from jax import lax
from jax.experimental import pallas as pl
from jax.experimental.pallas import tpu as pltpu

D_MODEL = 2048
D_HALF = D_MODEL // 2
D_CONV = D_MODEL // 2
G_A = 4
D_GROUP_A = D_HALF // G_A
CHUNK = 128
CONV_W = 3
N_GROUPS = 4
EXPERTS_PER_GROUP = 8
N_EXPERTS = N_GROUPS * EXPERTS_PER_GROUP
D_EXPERT = D_MODEL // 4
D_IN_TOTAL = 2 * D_HALF + 3 * D_CONV + 2 * D_MODEL
EPS = 1e-6

BF16 = jnp.bfloat16
F32 = jnp.float32

LANES = 128
VMEM_LIMIT = 60 * 1024 * 1024

TM_NORM = 512
TN_IN = 1024
TM = 256
TM_E = 256
ROWS = D_MODEL // (2 * LANES)
PACKED = jnp.uint32
COPY_TILE = 256
DISPATCH_SLOTS = 3


def _rms(x, g):
    return x * lax.rsqrt(jnp.mean(x * x, axis=-1, keepdims=True) + EPS) * g


def _sigmoid(x):
    return 1.0 / (1.0 + jnp.exp(-x))


def _store_token_major(ref, x):
    m = x.shape[0]
    for s in range(ROWS):
        lo = x[:, s * LANES:(s + 1) * LANES]
        hi = x[:, (ROWS + s) * LANES:(ROWS + s + 1) * LANES]
        ref[pl.ds(s, m, stride=ROWS), :] = pltpu.pack_elementwise([lo, hi], packed_dtype=BF16)


def _unpack_words(words):
    half = lambda idx: [pltpu.unpack_elementwise(w, index=idx, packed_dtype=BF16, unpacked_dtype=F32)
                        for w in words]
    return jnp.concatenate(half(0) + half(1), axis=1)


def _load_token_major(ref, m):
    return _unpack_words([ref[pl.ds(s, m, stride=ROWS), :] for s in range(ROWS)])


def _norm_kernel(xp_ref, xs_ref, g_ref, o_ref, *, n_prompt_tiles):
    i = pl.program_id(0)

    def run(x_ref):
        o_ref[...] = _rms(x_ref[...], g_ref[...]).astype(BF16)

    @pl.when(i < n_prompt_tiles)
    def _():
        run(xp_ref)

    @pl.when(i >= n_prompt_tiles)
    def _():
        run(xs_ref)


def _norm(xp, xs, g):
    tp, ts = xp.shape[0], xs.shape[0]
    npt, nst = tp // TM_NORM, ts // TM_NORM
    return pl.pallas_call(
        functools.partial(_norm_kernel, n_prompt_tiles=npt),
        grid=(npt + nst,),
        in_specs=[
            pl.BlockSpec((TM_NORM, D_MODEL), lambda i: (jnp.minimum(i, npt - 1), 0)),
            pl.BlockSpec((TM_NORM, D_MODEL), lambda i: (jnp.maximum(i - npt, 0), 0)),
            pl.BlockSpec((1, D_MODEL), lambda i: (0, 0)),
        ],
        out_specs=pl.BlockSpec((TM_NORM, D_MODEL), lambda i: (i, 0)),
        out_shape=jax.ShapeDtypeStruct((tp + ts, D_MODEL), BF16),
        compiler_params=pltpu.CompilerParams(
            dimension_semantics=("arbitrary",), vmem_limit_bytes=VMEM_LIMIT),
        name="norm",
    )(xp, xs, g)


def _inproj_kernel(n_ref, w_ref, z_ref, wb_ref):
    @pl.when(pl.program_id(1) == 0)
    def _():
        wb_ref[...] = w_ref[...].astype(BF16)

    z_ref[...] = jnp.dot(n_ref[...], wb_ref[...], preferred_element_type=F32).astype(BF16)


def _inproj(n, w_in):
    t = n.shape[0]
    return pl.pallas_call(
        _inproj_kernel,
        grid=(D_IN_TOTAL // TN_IN, t // TM_NORM),
        in_specs=[
            pl.BlockSpec((TM_NORM, D_MODEL), lambda j, i: (i, 0)),
            pl.BlockSpec((None, D_MODEL, TN_IN), lambda j, i: (0, 0, j)),
        ],
        out_specs=pl.BlockSpec((TM_NORM, TN_IN), lambda j, i: (i, j)),
        out_shape=jax.ShapeDtypeStruct((t, D_IN_TOTAL), BF16),
        scratch_shapes=[pltpu.VMEM((D_MODEL, TN_IN), BF16)],
        compiler_params=pltpu.CompilerParams(
            dimension_semantics=("arbitrary", "arbitrary"), vmem_limit_bytes=VMEM_LIMIT),
        name="in_proj",
    )(n, w_in)


def _route(lg):
    shape = lg.shape
    lane = lax.broadcasted_iota(jnp.int32, shape, 1)
    lanef = lane.astype(F32)
    neg = jnp.float32(-jnp.inf)
    far = jnp.float32(LANES)
    is_g = lane < N_GROUPS
    lgm = jnp.where(is_g, lg, neg)
    gmax = jnp.max(lgm, axis=-1, keepdims=True)
    gsel = jnp.min(jnp.where(lgm == gmax, lanef, far), axis=-1, keepdims=True)
    p_group = 1.0 / jnp.sum(jnp.where(is_g, jnp.exp(lgm - gmax), 0.0), axis=-1, keepdims=True)
    lane_group = ((lane - N_GROUPS) >> 3).astype(F32)
    is_e = lane_group == gsel
    le = jnp.where(is_e, lg, neg)
    m1 = jnp.max(le, axis=-1, keepdims=True)
    i1 = jnp.min(jnp.where(le == m1, lanef, far), axis=-1, keepdims=True)
    le2 = jnp.where(lanef == i1, neg, le)
    m2 = jnp.max(le2, axis=-1, keepdims=True)
    i2 = jnp.min(jnp.where(le2 == m2, lanef, far), axis=-1, keepdims=True)
    e21 = jnp.exp(m2 - m1)
    w1 = 1.0 / (1.0 + e21)
    w2 = e21 * w1
    out = jnp.where(lane == 0, i1 - N_GROUPS,
                    jnp.where(lane == 1, i2 - N_GROUPS,
                              jnp.where(lane == 2, p_group * w1,
                                        jnp.where(lane == 3, p_group * w2, 0.0))))
    return out


def _mixer_kernel(z_ref, xp_ref, xs_ref, f_ref, lng_ref, lnb_ref, cw_ref, wsp_ref, bsp_ref,
                  wa_ref, wb_ref, wo_ref, gm_ref, wr_ref, br_ref,
                  h_ref, xm_ref, route_ref, vnp_ref, vns_ref, qp_ref, qs_ref,
                  qprev_ref, ya_ref, yb_ref, yc_ref, m_ref, *, n_prompt_tiles, tiles_per_seq):
    i = pl.program_id(0)
    is_prompt = i < n_prompt_tiles
    c_u, c_v, c_b, c_c, c_x = (k * D_HALF for k in range(5))
    c_ga = c_x + D_CONV
    c_gb = c_ga + D_MODEL

    v = z_ref[:, c_v:c_v + D_HALF].astype(F32)
    mu = jnp.mean(v, axis=-1, keepdims=True)
    vc = v - mu
    vn = vc * lax.rsqrt(jnp.mean(vc * vc, axis=-1, keepdims=True) + EPS) * lng_ref[...] + lnb_ref[...]
    vnb = vn.astype(BF16)
    for c in range(TM // CHUNK):
        rows = slice(c * CHUNK, (c + 1) * CHUNK)
        for g in range(G_A):
            cols = slice(g * D_GROUP_A, (g + 1) * D_GROUP_A)
            s = jnp.dot(wsp_ref[g], vnb[rows, cols], preferred_element_type=F32) + bsp_ref[:, cols]
            u = z_ref[rows, c_u + g * D_GROUP_A:c_u + (g + 1) * D_GROUP_A].astype(F32)
            ya_ref[rows, cols] = (u * s).astype(BF16)

    q = z_ref[:, c_c:c_c + D_CONV].astype(F32) * z_ref[:, c_x:c_x + D_CONV].astype(F32)
    q1 = pltpu.roll(q, 1, 0)
    q2 = pltpu.roll(q, 2, 0)
    row = lax.broadcasted_iota(jnp.int32, q.shape, 0)
    cw0, cw1, cw2 = cw_ref[0:1, :], cw_ref[1:2, :], cw_ref[2:3, :]

    @pl.when(is_prompt)
    def _():
        seq_start = i % tiles_per_seq == 0
        p6 = jnp.where(seq_start, 0.0, qprev_ref[6:7, :])
        p7 = jnp.where(seq_start, 0.0, qprev_ref[7:8, :])
        a1 = jnp.where(row == 0, p7, q1)
        a2 = jnp.where(row == 0, p6, jnp.where(row == 1, p7, q2))
        yc_ref[...] = cw0 * a2 + cw1 * a1 + cw2 * q
        qprev_ref[...] = q[TM - 8:TM, :]
        qp_ref[...] = q[TM - (CONV_W - 1):TM, :]
        vnp_ref[...] = vn[TM - CHUNK:TM, :]

    @pl.when(jnp.logical_not(is_prompt))
    def _():
        f = f_ref[...]
        f1 = pltpu.roll(f, TM - 1, 0)
        r8 = row & 7
        a1 = jnp.where(r8 == 0, f1, q1)
        a2 = jnp.where(r8 < 2, f, q2)
        yc_ref[...] = cw0 * a2 + cw1 * a1 + cw2 * q
        qs_ref[...] = q
        vns_ref[...] = vn

    yb_ref[...] = (z_ref[:, c_b:c_b + D_CONV].astype(F32) * yc_ref[...]).astype(BF16)

    half = D_MODEL // 2
    for hh in range(2):
        cols = slice(hh * half, (hh + 1) * half)
        a = jnp.dot(ya_ref[...], wa_ref[:, cols], preferred_element_type=F32)
        b = jnp.dot(yb_ref[...], wb_ref[:, cols], preferred_element_type=F32)
        ga = z_ref[:, c_ga + hh * half:c_ga + (hh + 1) * half].astype(F32)
        gb = z_ref[:, c_gb + hh * half:c_gb + (hh + 1) * half].astype(F32)
        m_ref[:, cols] = (_sigmoid(ga) * a + _sigmoid(gb) * b).astype(BF16)

    x = jnp.where(is_prompt, xp_ref[...], xs_ref[...])
    h = x + jnp.dot(m_ref[...], wo_ref[...], preferred_element_type=F32)
    h_ref[...] = h
    xm = _rms(h, gm_ref[...])
    _store_token_major(xm_ref, xm)
    lg =jnp.dot(xm.astype(BF16), wr_ref[...], preferred_element_type=F32) + br_ref[...]
    route_ref[...] = _route(lg)


def _mixer(z, xp, xs, f, lng, lnb, cw, wsp, bsp, wa, wb, wo, gm, wr, br, *, n_batch):
    tp, ts = xp.shape[0], xs.shape[0]
    t = tp + ts
    npt, nst = tp // TM, ts // TM
    tiles_per_seq = tp // n_batch // TM
    once = pl.Buffered(1)

    def pidx(i):
        return jnp.minimum(i, npt - 1)

    def sidx(i):
        return jnp.maximum(i - npt, 0)

    const = lambda shape: pl.BlockSpec(shape, lambda i: (0,) * len(shape), pipeline_mode=once)
    in_specs = [
        pl.BlockSpec((TM, D_IN_TOTAL), lambda i: (i, 0)),
        pl.BlockSpec((TM, D_MODEL), lambda i: (pidx(i), 0)),
        pl.BlockSpec((TM, D_MODEL), lambda i: (sidx(i), 0)),
        pl.BlockSpec((TM, D_CONV), lambda i: (sidx(i), 0)),
        const((1, D_HALF)), const((1, D_HALF)), const((CONV_W, D_CONV)),
        pl.BlockSpec((None, G_A, CHUNK, CHUNK), lambda i: (jnp.where(i < npt, 0, 1), 0, 0, 0)),
        pl.BlockSpec((None, CHUNK, D_HALF), lambda i: (jnp.where(i < npt, 0, 1), 0, 0)),
        const((D_HALF, D_MODEL)), const((D_CONV, D_MODEL)), const((D_MODEL, D_MODEL)),
        const((1, D_MODEL)), const((D_MODEL, LANES)), const((1, LANES)),
    ]
    out_specs = [
        pl.BlockSpec((TM, D_MODEL), lambda i: (i, 0)),
        pl.BlockSpec((TM * ROWS, LANES), lambda i: (i, 0)),
        pl.BlockSpec((TM, LANES), lambda i: (i, 0)),
        pl.BlockSpec((None, CHUNK, D_HALF), lambda i: (pidx(i) // tiles_per_seq, 0, 0)),
        pl.BlockSpec((TM, D_HALF), lambda i: (sidx(i), 0)),
        pl.BlockSpec((None, CONV_W - 1, D_CONV), lambda i: (pidx(i) // tiles_per_seq, 0, 0)),
        pl.BlockSpec((TM, D_CONV), lambda i: (sidx(i), 0)),
    ]
    out_shape = [
        jax.ShapeDtypeStruct((t, D_MODEL), F32),
        jax.ShapeDtypeStruct((t * ROWS, LANES), PACKED),
        jax.ShapeDtypeStruct((t, LANES), F32),
        jax.ShapeDtypeStruct((n_batch, CHUNK, D_HALF), F32),
        jax.ShapeDtypeStruct((ts, D_HALF), F32),
        jax.ShapeDtypeStruct((n_batch, CONV_W - 1, D_CONV), F32),
        jax.ShapeDtypeStruct((ts, D_CONV), F32),
    ]
    return pl.pallas_call(
        functools.partial(_mixer_kernel, n_prompt_tiles=npt, tiles_per_seq=tiles_per_seq),
        grid=(npt + nst,),
        in_specs=in_specs,
        out_specs=out_specs,
        out_shape=out_shape,
        scratch_shapes=[
            pltpu.VMEM((8, D_CONV), F32),
            pltpu.VMEM((TM, D_HALF), BF16),
            pltpu.VMEM((TM, D_CONV), BF16),
            pltpu.VMEM((TM, D_CONV), F32),
            pltpu.VMEM((TM, D_MODEL), BF16),
        ],
        compiler_params=pltpu.CompilerParams(
            dimension_semantics=("arbitrary",), vmem_limit_bytes=VMEM_LIMIT),
        name="mixer",
    )(z, xp, xs, f, lng, lnb, cw, wsp, bsp, wa, wb, wo, gm, wr, br)


def _dispatch_kernel(pos_ref, cnt_ref, off_ref, nused_ref, xm_hbm, xs_hbm, inbuf, zbuf, isem, rsem, zsem,
                     *, n_steps, n_tiles):
    zbuf[...] = jnp.zeros_like(zbuf)
    nused = nused_ref[0]
    pad_bits = TM_E.bit_length() - 1

    def pad_copies(e, fn):
        cnt = cnt_ref[e]
        npad = (TM_E - cnt % TM_E) % TM_E
        start = off_ref[e] + cnt
        for b in range(pad_bits):
            size = 1 << b
            at = start + ((npad >> (b + 1)) << (b + 1))

            @pl.when(((npad >> b) & 1) == 1)
            def _():
                fn(pltpu.make_async_copy(zbuf.at[pl.ds(0, size * ROWS), :],
                                         xs_hbm.at[pl.ds(at * ROWS, size * ROWS), :], zsem))

    def tile_copy(tile):
        return pltpu.make_async_copy(zbuf, xs_hbm.at[pl.ds(tile * TM_E * ROWS, TM_E * ROWS), :], zsem)

    def loop(lo, hi, body):
        lax.fori_loop(lo, hi, lambda k, c: (body(k), c)[1], 0)

    loop(0, N_EXPERTS, lambda e: pad_copies(e, lambda cp: cp.start()))
    loop(nused, n_tiles, lambda tile: tile_copy(tile).start())

    def in_copy(step, slot):
        return pltpu.make_async_copy(xm_hbm.at[pl.ds(step * COPY_TILE * ROWS, COPY_TILE * ROWS), :],
                                     inbuf.at[slot], isem.at[slot])

    def rows_wait(slot):
        for _ in range(2):
            pltpu.make_async_copy(inbuf.at[slot], xs_hbm.at[pl.ds(0, COPY_TILE * ROWS), :], rsem.at[slot]).wait()

    in_copy(0, 0).start()

    def step(s):
        slot = s % DISPATCH_SLOTS
        nxt = (s + 1) % DISPATCH_SLOTS

        @pl.when(s >= DISPATCH_SLOTS - 1)
        def _():
            rows_wait(nxt)

        @pl.when(s + 1 < n_steps)
        def _():
            in_copy(s + 1, nxt).start()

        in_copy(s, slot).wait()
        base = s * COPY_TILE * 2

        def body(r, carry):
            for k in range(2):
                p = pos_ref[base + 2 * r + k]
                pltpu.make_async_copy(inbuf.at[slot, pl.ds(r * ROWS, ROWS), :],
                                      xs_hbm.at[pl.ds(p * ROWS, ROWS), :], rsem.at[slot]).start()
            return carry

        lax.fori_loop(0, COPY_TILE, body, 0, unroll=4)

    loop(0, n_steps, step)
    for back in range(DISPATCH_SLOTS - 1, 0, -1):
        rows_wait((n_steps - back) % DISPATCH_SLOTS)
    loop(0, N_EXPERTS, lambda e: pad_copies(e, lambda cp: cp.wait()))
    loop(nused, n_tiles, lambda tile: tile_copy(tile).wait())


def _dispatch(pos, cnt, off, nused, xm, *, n_tiles):
    n_tokens = xm.shape[0] // ROWS
    assert n_tokens % COPY_TILE == 0 and n_tokens // COPY_TILE >= DISPATCH_SLOTS
    smem = pl.BlockSpec(memory_space=pltpu.SMEM)
    return pl.pallas_call(
        functools.partial(_dispatch_kernel, n_steps=n_tokens // COPY_TILE, n_tiles=n_tiles),
        in_specs=[smem, smem, smem, smem, pl.BlockSpec(memory_space=pl.ANY)],
        out_specs=pl.BlockSpec(memory_space=pl.ANY),
        out_shape=jax.ShapeDtypeStruct((n_tiles * TM_E * ROWS, LANES), PACKED),
        scratch_shapes=[
            pltpu.VMEM((DISPATCH_SLOTS, COPY_TILE * ROWS, LANES), PACKED),
            pltpu.VMEM((TM_E * ROWS, LANES), PACKED),
            pltpu.SemaphoreType.DMA((DISPATCH_SLOTS,)),
            pltpu.SemaphoreType.DMA((DISPATCH_SLOTS,)),
            pltpu.SemaphoreType.DMA(()),
        ],
        compiler_params=pltpu.CompilerParams(vmem_limit_bytes=VMEM_LIMIT),
        name="dispatch",
    )(pos, cnt, off, nused, xm)


def _expert_kernel(te_ref, nused_ref, xs_ref, wg_hbm, wu_hbm, wd_hbm, y_ref,
                   wgf, wuf, wdf, wgb, wub, wdb, wsem, seen_ref):
    i = pl.program_id(0)
    nused = nused_ref[0]

    def weight_copies(e, slot):
        return [pltpu.make_async_copy(src.at[0, e], dst.at[slot], wsem.at[slot, k])
                for k, (src, dst) in enumerate(((wg_hbm, wgf), (wu_hbm, wuf), (wd_hbm, wdf)))]

    @pl.when(i == 0)
    def _():
        seen_ref[0] = 0
        for cp in weight_copies(te_ref[0], 0):
            cp.start()

    @pl.when(i < nused)
    def _():
        e = te_ref[i]
        new_expert = jnp.logical_or(i == 0, e != te_ref[jnp.maximum(i - 1, 0)])

        @pl.when(new_expert)
        def _():
            slot = seen_ref[0] % 2
            seen_ref[0] = seen_ref[0] + 1
            nxt = lax.while_loop(lambda j: jnp.logical_and(j < nused, te_ref[jnp.minimum(j, nused - 1)] == e),
                                 lambda j: j + 1, i + 1)

            @pl.when(nxt < nused)
            def _():
                for cp in weight_copies(te_ref[jnp.minimum(nxt, nused - 1)], 1 - slot):
                    cp.start()

            for cp in weight_copies(e, slot):
                cp.wait()
            wgb[...] = wgf[slot].astype(BF16)
            wub[...] = wuf[slot].astype(BF16)
            wdb[...] = wdf[slot].astype(BF16)

        x = _load_token_major(xs_ref, TM_E).astype(BF16)
        g = jnp.dot(x, wgb[...], preferred_element_type=F32)
        u = jnp.dot(x, wub[...], preferred_element_type=F32)
        hid = (g * _sigmoid(g)) * u
        _store_token_major(y_ref, jnp.dot(hid.astype(BF16), wdb[...], preferred_element_type=F32))

    @pl.when(i >= nused)
    def _():
        y_ref[...] = jnp.zeros_like(y_ref)


def _experts(te, nused, xs, wg, wu, wd, *, n_tiles):
    grid_spec = pltpu.PrefetchScalarGridSpec(
        num_scalar_prefetch=2,
        grid=(n_tiles,),
        in_specs=[
            pl.BlockSpec((TM_E * ROWS, LANES), lambda i, te, nu: (jnp.minimum(i, nu[0] - 1), 0)),
            pl.BlockSpec(memory_space=pl.ANY),
            pl.BlockSpec(memory_space=pl.ANY),
            pl.BlockSpec(memory_space=pl.ANY),
        ],
        out_specs=pl.BlockSpec((TM_E * ROWS, LANES), lambda i, te, nu: (i, 0)),
        scratch_shapes=[
            pltpu.VMEM((2, D_MODEL, D_EXPERT), F32),
            pltpu.VMEM((2, D_MODEL, D_EXPERT), F32),
            pltpu.VMEM((2, D_EXPERT, D_MODEL), F32),
            pltpu.VMEM((D_MODEL, D_EXPERT), BF16),
            pltpu.VMEM((D_MODEL, D_EXPERT), BF16),
            pltpu.VMEM((D_EXPERT, D_MODEL), BF16),
            pltpu.SemaphoreType.DMA((2, 3)),
            pltpu.SMEM((1,), jnp.int32),
        ],
    )
    return pl.pallas_call(
        _expert_kernel,
        grid_spec=grid_spec,
        out_shape=jax.ShapeDtypeStruct((n_tiles * TM_E * ROWS, LANES), PACKED),
        compiler_params=pltpu.CompilerParams(
            dimension_semantics=("arbitrary",), vmem_limit_bytes=VMEM_LIMIT),
        name="experts",
    )(te, nused, xs, wg, wu, wd)


def _tail_kernel(pos_ref, ys_hbm, h_ref, route_ref, pp_ref, ps_ref, gp_ref, wpg_ref, wpp_ref, gf_ref,
                 yp_ref, ysm_ref, ybuf, sem, h2_ref, *, n_prompt_tiles, n_tiles):
    i = pl.program_id(0)
    slot = i % 2
    is_prompt = i < n_prompt_tiles

    def issue(tile, dst_slot):
        base = tile * TM * 2

        def body(r, carry):
            for k in range(2):
                p = pos_ref[base + 2 * r + k]
                pltpu.make_async_copy(ys_hbm.at[pl.ds(p * ROWS, ROWS), :],
                                      ybuf.at[dst_slot, k, pl.ds(r * ROWS, ROWS), :],
                                      sem.at[dst_slot]).start()
            return carry

        lax.fori_loop(0, TM, body, 0, unroll=4)

    @pl.when(i == 0)
    def _():
        issue(0, 0)

    @pl.when(i + 1 < n_tiles)
    def _():
        issue(i + 1, 1 - slot)

    for k in range(2):
        pltpu.make_async_copy(ys_hbm.at[pl.ds(0, TM * ROWS), :], ybuf.at[slot, k], sem.at[slot]).wait()

    w0 = jnp.broadcast_to(route_ref[:, 2:3], (TM, LANES))
    w1 = jnp.broadcast_to(route_ref[:, 3:4], (TM, LANES))
    for s in range(ROWS):
        word0 = ybuf[slot, 0, pl.ds(s, TM, stride=ROWS), :]
        word1 = ybuf[slot, 1, pl.ds(s, TM, stride=ROWS), :]
        for idx in range(2):
            cols = slice((idx * ROWS + s) * LANES, (idx * ROWS + s + 1) * LANES)
            y0 = pltpu.unpack_elementwise(word0, index=idx, packed_dtype=BF16, unpacked_dtype=F32)
            y1 = pltpu.unpack_elementwise(word1, index=idx, packed_dtype=BF16, unpacked_dtype=F32)
            h2_ref[:, cols] = h_ref[:, cols] + (w0 * y0 + w1 * y1)
    h = h2_ref[...]
    a = _rms(h, gp_ref[...]).astype(BF16)
    gate = _sigmoid(jnp.dot(a, wpg_ref[...], preferred_element_type=F32))
    p = jnp.where(is_prompt, pp_ref[...], ps_ref[...]).astype(BF16)
    h = h + gate * jnp.dot(p, wpp_ref[...], preferred_element_type=F32)
    y = _rms(h, gf_ref[...])

    @pl.when(is_prompt)
    def _():
        yp_ref[...] = y

    @pl.when(jnp.logical_not(is_prompt))
    def _():
        ysm_ref[...] = y


def _tail(pos, ys, h, route, pp, ps, gp, wpg, wpp, gf):
    tp, ts = pp.shape[0], ps.shape[0]
    npt, nst = tp // TM, ts // TM
    ple = pp.shape[1]
    once = pl.Buffered(1)

    def pidx(i):
        return jnp.minimum(i, npt - 1)

    def sidx(i):
        return jnp.maximum(i - npt, 0)

    grid_spec = pltpu.PrefetchScalarGridSpec(
        num_scalar_prefetch=1,
        grid=(npt + nst,),
        in_specs=[
            pl.BlockSpec(memory_space=pl.ANY),
            pl.BlockSpec((TM, D_MODEL), lambda i, pos: (i, 0)),
            pl.BlockSpec((TM, LANES), lambda i, pos: (i, 0)),
            pl.BlockSpec((TM, ple), lambda i, pos: (pidx(i), 0)),
            pl.BlockSpec((TM, ple), lambda i, pos: (sidx(i), 0)),
            pl.BlockSpec((1, D_MODEL), lambda i, pos: (0, 0), pipeline_mode=once),
            pl.BlockSpec((D_MODEL, D_MODEL), lambda i, pos: (0, 0), pipeline_mode=once),
            pl.BlockSpec((ple, D_MODEL), lambda i, pos: (0, 0), pipeline_mode=once),
            pl.BlockSpec((1, D_MODEL), lambda i, pos: (0, 0), pipeline_mode=once),
        ],
        out_specs=[
            pl.BlockSpec((TM, D_MODEL), lambda i, pos: (pidx(i), 0)),
            pl.BlockSpec((TM, D_MODEL), lambda i, pos: (sidx(i), 0)),
        ],
        scratch_shapes=[
            pltpu.VMEM((2, 2, TM * ROWS, LANES), PACKED),
            pltpu.SemaphoreType.DMA((2,)),
            pltpu.VMEM((TM, D_MODEL), F32),
        ],
    )
    return pl.pallas_call(
        functools.partial(_tail_kernel, n_prompt_tiles=npt, n_tiles=npt + nst),
        grid_spec=grid_spec,
        out_shape=[jax.ShapeDtypeStruct((tp, D_MODEL), F32), jax.ShapeDtypeStruct((ts, D_MODEL), F32)],
        compiler_params=pltpu.CompilerParams(
            dimension_semantics=("arbitrary",), vmem_limit_bytes=VMEM_LIMIT),
        name="tail",
    )(pos, ys, h, route, pp, ps, gp, wpg, wpp, gf)


def _route_tables(route, n_tiles):
    e = route[:, 0:2].astype(jnp.int32).reshape(-1)
    onehot = (e[:, None] == jnp.arange(N_EXPERTS, dtype=jnp.int32)[None, :]).astype(jnp.int32)
    cum = jnp.cumsum(onehot, axis=0)
    counts = cum[-1]
    rank = jnp.sum((cum - onehot) * onehot, axis=1)
    padded = ((counts + TM_E - 1) // TM_E) * TM_E
    ends = jnp.cumsum(padded)
    offs = ends - padded
    pos = jnp.sum(onehot * offs[None, :], axis=1) + rank
    nused = ends[-1] // TM_E
    tile_id = jnp.arange(n_tiles, dtype=jnp.int32)
    te = jnp.sum((tile_id[:, None] * TM_E >= ends[None, :]).astype(jnp.int32), axis=1)
    te_last = jnp.sum((((nused - 1) * TM_E) >= ends).astype(jnp.int32))
    te = jnp.where(tile_id < nused, te, te_last)
    i32 = lambda a: a.astype(jnp.int32)
    return i32(te), i32(nused).reshape(1), i32(pos), i32(counts), i32(offs)


def kernel(x_prompt, x_sample, state_conv, p_prompt, p_sample, g_mix, w_in, ln_v_g, ln_v_b, w_spatial, b_spatial, conv_w, w_br_a, w_br_b, w_out, g_moe, w_router_group, b_router_group, w_router_expert, b_router_expert, w_exp_gate, w_exp_up, w_exp_down, g_ple, w_ple_gate, w_ple_proj, g_final):
    depth = w_in.shape[0]
    assert depth == 1, "single-layer step only"
    nb, seq, _ = x_prompt.shape
    ns, dseq, _ = x_sample.shape
    assert dseq == 8 and seq % TM == 0 and (ns * dseq) % TM_NORM == 0 and (nb * seq) % TM_NORM == 0
    tp, ts = nb * seq, ns * dseq
    t = tp + ts
    xp = x_prompt.reshape(tp, D_MODEL)
    xs = x_sample.reshape(ts, D_MODEL)

    row = lambda a: a.reshape(1, -1)
    tril = jnp.tril(jnp.ones((CHUNK, CHUNK), dtype=bool))
    w_sp = jnp.where(tril[None], w_spatial[0], 0.0)
    eye = jnp.eye(CHUNK // dseq, dtype=F32)
    w_sp_s = jnp.einsum("ij,gts->gitjs", eye, w_sp[:, :dseq, :dseq]).reshape(G_A, CHUNK, CHUNK)
    wsp = jnp.stack([w_sp, w_sp_s]).astype(BF16)
    b_p = jnp.repeat(b_spatial[0].T, D_GROUP_A, axis=1)
    b_s = jnp.tile(b_p[:dseq], (CHUNK // dseq, 1))
    bsp = jnp.stack([b_p, b_s])
    wr = jnp.zeros((D_MODEL, LANES), F32)
    wr = wr.at[:, :N_GROUPS].set(w_router_group[0]).at[:, N_GROUPS:N_GROUPS + N_EXPERTS].set(w_router_expert[0])
    br = jnp.zeros((1, LANES), F32)
    br = br.at[0, :N_GROUPS].set(b_router_group[0]).at[0, N_GROUPS:N_GROUPS + N_EXPERTS].set(b_router_expert[0])
    f = jnp.pad(state_conv[0], ((0, 0), (0, dseq - (CONV_W - 1)), (0, 0))).reshape(ts, D_CONV)

    n = _norm(xp, xs, row(g_mix[0]))
    z = _inproj(n, w_in)
    h, xm, route, vnp, vns, qp, qs = _mixer(
        z, xp, xs, f, row(ln_v_g[0]), row(ln_v_b[0]), conv_w[0], wsp, bsp,
        w_br_a[0].astype(BF16), w_br_b[0].astype(BF16), w_out[0].astype(BF16),
        row(g_moe[0]), wr.astype(BF16), br, n_batch=nb)

    n_tiles = (2 * t) // TM_E + N_EXPERTS
    te, nused, pos, counts, offs = _route_tables(route, n_tiles)
    xsort = _dispatch(pos, counts, offs, nused, xm, n_tiles=n_tiles)
    ys = _experts(te, nused, xsort, w_exp_gate, w_exp_up, w_exp_down, n_tiles=n_tiles)

    yp, ysm = _tail(pos, ys, h, route, p_prompt[0].reshape(tp, -1), p_sample[0].reshape(ts, -1),
                    row(g_ple[0]), w_ple_gate[0].astype(BF16), w_ple_proj[0].astype(BF16), row(g_final))

    y_prompt = yp.reshape(nb, seq, D_MODEL)
    y_sample = ysm.reshape(ns, dseq, D_MODEL)
    conv_state_prompt = qp[None]
    conv_state_sample = qs.reshape(ns, dseq, D_CONV)[:, dseq - (CONV_W - 1):][None]
    v_rows_prompt = vnp[None]
    v_rows_sample = vns.reshape(ns, dseq, D_HALF)[None]
    return (y_prompt, y_sample, conv_state_prompt, conv_state_sample, v_rows_prompt, v_rows_sample)
```

```python
import functools

import jax
import jax.numpy as jnp
from jax import lax
from jax.experimental import pallas as pl
from jax.experimental.pallas import tpu as pltpu

D_MODEL = 2048
D_HALF = D_MODEL // 2
D_CONV = D_MODEL // 2
G_A = 4
D_GROUP_A = D_HALF // G_A
CHUNK = 128
CONV_W = 3
N_GROUPS = 4
EXPERTS_PER_GROUP = 8
N_EXPERTS = N_GROUPS * EXPERTS_PER_GROUP
D_EXPERT = D_MODEL // 4
D_IN_TOTAL = 2 * D_HALF + 3 * D_CONV + 2 * D_MODEL
EPS = 1e-6

BF16 = jnp.bfloat16
F32 = jnp.float32

LANES = 128
VMEM_LIMIT = 60 * 1024 * 1024

TM_NORM = 512
TN_IN = 1024
TM = 256
TM_E = 256
ROWS = D_MODEL // (2 * LANES)
PACKED = jnp.uint32
COPY_TILE = 256
DISPATCH_SLOTS = 3
TAIL_CHUNKS = 8


def _rms(x, g):
    return x * lax.rsqrt(jnp.mean(x * x, axis=-1, keepdims=True) + EPS) * g


def _sigmoid(x):
    return 1.0 / (1.0 + jnp.exp(-x))


def _store_token_major(ref, x):
    m = x.shape[0]
    for s in range(ROWS):
        lo = x[:, s * LANES:(s + 1) * LANES]
        hi = x[:, (ROWS + s) * LANES:(ROWS + s + 1) * LANES]
        ref[pl.ds(s, m, stride=ROWS), :] = pltpu.pack_elementwise([lo, hi], packed_dtype=BF16)


def _packed_zeros(shape):
    zero = jnp.zeros(shape, F32)
    return pltpu.pack_elementwise([zero, zero], packed_dtype=BF16)


def _unpack_words(words):
    half = lambda idx: [pltpu.unpack_elementwise(w, index=idx, packed_dtype=BF16, unpacked_dtype=F32)
                        for w in words]
    return jnp.concatenate(half(0) + half(1), axis=1)


def _load_token_major(ref, m):
    return _unpack_words([ref[pl.ds(s, m, stride=ROWS), :] for s in range(ROWS)])


def _norm_kernel(xp_ref, xs_ref, g_ref, o_ref, *, n_prompt_tiles):
    i = pl.program_id(0)

    def run(x_ref):
        o_ref[...] = _rms(x_ref[...], g_ref[...]).astype(BF16)

    @pl.when(i < n_prompt_tiles)
    def _():
        run(xp_ref)

    @pl.when(i >= n_prompt_tiles)
    def _():
        run(xs_ref)


def _norm(xp, xs, g):
    tp, ts = xp.shape[0], xs.shape[0]
    npt, nst = tp // TM_NORM, ts // TM_NORM
    return pl.pallas_call(
        functools.partial(_norm_kernel, n_prompt_tiles=npt),
        grid=(npt + nst,),
        in_specs=[
            pl.BlockSpec((TM_NORM, D_MODEL), lambda i: (jnp.minimum(i, npt - 1), 0)),
            pl.BlockSpec((TM_NORM, D_MODEL), lambda i: (jnp.maximum(i - npt, 0), 0)),
            pl.BlockSpec((1, D_MODEL), lambda i: (0, 0)),
        ],
        out_specs=pl.BlockSpec((TM_NORM, D_MODEL), lambda i: (i, 0)),
        out_shape=jax.ShapeDtypeStruct((tp + ts, D_MODEL), BF16),
        compiler_params=pltpu.CompilerParams(
            dimension_semantics=("arbitrary",), vmem_limit_bytes=VMEM_LIMIT),
        name="norm",
    )(xp, xs, g)


def _inproj_kernel(n_ref, w_ref, z_ref, wb_ref):
    @pl.when(pl.program_id(1) == 0)
    def _():
        wb_ref[...] = w_ref[...].astype(BF16)

    z_ref[...] = jnp.dot(n_ref[...], wb_ref[...], preferred_element_type=F32).astype(BF16)


def _inproj(n, w_in):
    t = n.shape[0]
    return pl.pallas_call(
        _inproj_kernel,
        grid=(D_IN_TOTAL // TN_IN, t // TM_NORM),
        in_specs=[
            pl.BlockSpec((TM_NORM, D_MODEL), lambda j, i: (i, 0)),
            pl.BlockSpec((None, D_MODEL, TN_IN), lambda j, i: (0, 0, j)),
        ],
        out_specs=pl.BlockSpec((TM_NORM, TN_IN), lambda j, i: (i, j)),
        out_shape=jax.ShapeDtypeStruct((t, D_IN_TOTAL), BF16),
        scratch_shapes=[pltpu.VMEM((D_MODEL, TN_IN), BF16)],
        compiler_params=pltpu.CompilerParams(
            dimension_semantics=("arbitrary", "arbitrary"), vmem_limit_bytes=VMEM_LIMIT),
        name="in_proj",
    )(n, w_in)


def _route(lg):
    shape = lg.shape
    lane = lax.broadcasted_iota(jnp.int32, shape, 1)
    lanef = lane.astype(F32)
    neg = jnp.float32(-jnp.inf)
    far = jnp.float32(LANES)
    is_g = lane < N_GROUPS
    lgm = jnp.where(is_g, lg, neg)
    gmax = jnp.max(lgm, axis=-1, keepdims=True)
    gsel = jnp.min(jnp.where(lgm == gmax, lanef, far), axis=-1, keepdims=True)
    p_group = 1.0 / jnp.sum(jnp.where(is_g, jnp.exp(lgm - gmax), 0.0), axis=-1, keepdims=True)
    lane_group = ((lane - N_GROUPS) >> 3).astype(F32)
    is_e = lane_group == gsel
    le = jnp.where(is_e, lg, neg)
    m1 = jnp.max(le, axis=-1, keepdims=True)
    i1 = jnp.min(jnp.where(le == m1, lanef, far), axis=-1, keepdims=True)
    le2 = jnp.where(lanef == i1, neg, le)
    m2 = jnp.max(le2, axis=-1, keepdims=True)
    i2 = jnp.min(jnp.where(le2 == m2, lanef, far), axis=-1, keepdims=True)
    e21 = jnp.exp(m2 - m1)
    w1 = 1.0 / (1.0 + e21)
    w2 = e21 * w1
    out = jnp.where(lane == 0, i1 - N_GROUPS,
                    jnp.where(lane == 1, i2 - N_GROUPS,
                              jnp.where(lane == 2, p_group * w1,
                                        jnp.where(lane == 3, p_group * w2, 0.0))))
    return out


def _mixer_kernel(z_ref, xp_ref, xs_ref, f_ref, lng_ref, lnb_ref, cw_ref, wsp_ref, bsp_ref,
                  wa_ref, wb_ref, wo_ref, gm_ref, wr_ref, br_ref,
                  h_ref, xm_ref, route_ref, vnp_ref, vns_ref, qp_ref, qs_ref,
                  qprev_ref, ya_ref, yb_ref, yc_ref, m_ref, *, n_prompt_tiles, tiles_per_seq):
    i = pl.program_id(0)
    is_prompt = i < n_prompt_tiles
    c_u, c_v, c_b, c_c, c_x = (k * D_HALF for k in range(5))
    c_ga = c_x + D_CONV
    c_gb = c_ga + D_MODEL

    v = z_ref[:, c_v:c_v + D_HALF].astype(F32)
    mu = jnp.mean(v, axis=-1, keepdims=True)
    vc = v - mu
    vn = vc * lax.rsqrt(jnp.mean(vc * vc, axis=-1, keepdims=True) + EPS) * lng_ref[...] + lnb_ref[...]
    vnb = vn.astype(BF16)
    for c in range(TM // CHUNK):
        rows = slice(c * CHUNK, (c + 1) * CHUNK)
        for g in range(G_A):
            cols = slice(g * D_GROUP_A, (g + 1) * D_GROUP_A)
            s = jnp.dot(wsp_ref[g], vnb[rows, cols], preferred_element_type=F32) + bsp_ref[:, cols]
            u = z_ref[rows, c_u + g * D_GROUP_A:c_u + (g + 1) * D_GROUP_A].astype(F32)
            ya_ref[rows, cols] = (u * s).astype(BF16)

    q = z_ref[:, c_c:c_c + D_CONV].astype(F32) * z_ref[:, c_x:c_x + D_CONV].astype(F32)
    q1 = pltpu.roll(q, 1, 0)
    q2 = pltpu.roll(q, 2, 0)
    row = lax.broadcasted_iota(jnp.int32, q.shape, 0)
    cw0, cw1, cw2 = cw_ref[0:1, :], cw_ref[1:2, :], cw_ref[2:3, :]

    @pl.when(is_prompt)
    def _():
        seq_start = i % tiles_per_seq == 0
        p6 = jnp.where(seq_start, 0.0, qprev_ref[6:7, :])
        p7 = jnp.where(seq_start, 0.0, qprev_ref[7:8, :])
        a1 = jnp.where(row == 0, p7, q1)
        a2 = jnp.where(row == 0, p6, jnp.where(row == 1, p7, q2))
        yc_ref[...] = cw0 * a2 + cw1 * a1 + cw2 * q
        qprev_ref[...] = q[TM - 8:TM, :]
        qp_ref[...] = q[TM - (CONV_W - 1):TM, :]
        vnp_ref[...] = vn[TM - CHUNK:TM, :]

    @pl.when(jnp.logical_not(is_prompt))
    def _():
        f = f_ref[...]
        f1 = pltpu.roll(f, TM - 1, 0)
        r8 = row & 7
        a1 = jnp.where(r8 == 0, f1, q1)
        a2 = jnp.where(r8 < 2, f, q2)
        yc_ref[...] = cw0 * a2 + cw1 * a1 + cw2 * q
        qs_ref[...] = q
        vns_ref[...] = vn

    yb_ref[...] = (z_ref[:, c_b:c_b + D_CONV].astype(F32) * yc_ref[...]).astype(BF16)

    half = D_MODEL // 2
    for hh in range(2):
        cols = slice(hh * half, (hh + 1) * half)
        a = jnp.dot(ya_ref[...], wa_ref[:, cols], preferred_element_type=F32)
        b = jnp.dot(yb_ref[...], wb_ref[:, cols], preferred_element_type=F32)
        ga = z_ref[:, c_ga + hh * half:c_ga + (hh + 1) * half].astype(F32)
        gb = z_ref[:, c_gb + hh * half:c_gb + (hh + 1) * half].astype(F32)
        m_ref[:, cols] = (_sigmoid(ga) * a + _sigmoid(gb) * b).astype(BF16)

    x = jnp.where(is_prompt, xp_ref[...], xs_ref[...])
    h = x + jnp.dot(m_ref[...], wo_ref[...], preferred_element_type=F32)
    h_ref[...] = h
    xm = _rms(h, gm_ref[...])
    _store_token_major(xm_ref, xm)
    lg =jnp.dot(xm.astype(BF16), wr_ref[...], preferred_element_type=F32) + br_ref[...]
    route_ref[...] = _route(lg)


def _mixer(z, xp, xs, f, lng, lnb, cw, wsp, bsp, wa, wb, wo, gm, wr, br, *, n_batch):
    tp, ts = xp.shape[0], xs.shape[0]
    t = tp + ts
    npt, nst = tp // TM, ts // TM
    tiles_per_seq = tp // n_batch // TM
    once = pl.Buffered(1)

    def pidx(i):
        return jnp.minimum(i, npt - 1)

    def sidx(i):
        return jnp.maximum(i - npt, 0)

    const = lambda shape: pl.BlockSpec(shape, lambda i: (0,) * len(shape), pipeline_mode=once)
    in_specs = [
        pl.BlockSpec((TM, D_IN_TOTAL), lambda i: (i, 0)),
        pl.BlockSpec((TM, D_MODEL), lambda i: (pidx(i), 0)),
        pl.BlockSpec((TM, D_MODEL), lambda i: (sidx(i), 0)),
        pl.BlockSpec((TM, D_CONV), lambda i: (sidx(i), 0)),
        const((1, D_HALF)), const((1, D_HALF)), const((CONV_W, D_CONV)),
        pl.BlockSpec((None, G_A, CHUNK, CHUNK), lambda i: (jnp.where(i < npt, 0, 1), 0, 0, 0)),
        pl.BlockSpec((None, CHUNK, D_HALF), lambda i: (jnp.where(i < npt, 0, 1), 0, 0)),
        const((D_HALF, D_MODEL)), const((D_CONV, D_MODEL)), const((D_MODEL, D_MODEL)),
        const((1, D_MODEL)), const((D_MODEL, LANES)), const((1, LANES)),
    ]
    out_specs = [
        pl.BlockSpec((TM, D_MODEL), lambda i: (i, 0)),
        pl.BlockSpec((TM * ROWS, LANES), lambda i: (i, 0)),
        pl.BlockSpec((TM, LANES), lambda i: (i, 0)),
        pl.BlockSpec((None, CHUNK, D_HALF), lambda i: (pidx(i) // tiles_per_seq, 0, 0)),
        pl.BlockSpec((TM, D_HALF), lambda i: (sidx(i), 0)),
        pl.BlockSpec((None, CONV_W - 1, D_CONV), lambda i: (pidx(i) // tiles_per_seq, 0, 0)),
        pl.BlockSpec((TM, D_CONV), lambda i: (sidx(i), 0)),
    ]
    out_shape = [
        jax.ShapeDtypeStruct((t, D_MODEL), F32),
        jax.ShapeDtypeStruct((t * ROWS, LANES), PACKED),
        jax.ShapeDtypeStruct((t, LANES), F32),
        jax.ShapeDtypeStruct((n_batch, CHUNK, D_HALF), F32),
        jax.ShapeDtypeStruct((ts, D_HALF), F32),
        jax.ShapeDtypeStruct((n_batch, CONV_W - 1, D_CONV), F32),
        jax.ShapeDtypeStruct((ts, D_CONV), F32),
    ]
    return pl.pallas_call(
        functools.partial(_mixer_kernel, n_prompt_tiles=npt, tiles_per_seq=tiles_per_seq),
        grid=(npt + nst,),
        in_specs=in_specs,
        out_specs=out_specs,
        out_shape=out_shape,
        scratch_shapes=[
            pltpu.VMEM((8, D_CONV), F32),
            pltpu.VMEM((TM, D_HALF), BF16),
            pltpu.VMEM((TM, D_CONV), BF16),
            pltpu.VMEM((TM, D_CONV), F32),
            pltpu.VMEM((TM, D_MODEL), BF16),
        ],
        compiler_params=pltpu.CompilerParams(
            dimension_semantics=("arbitrary",), vmem_limit_bytes=VMEM_LIMIT),
        name="mixer",
    )(z, xp, xs, f, lng, lnb, cw, wsp, bsp, wa, wb, wo, gm, wr, br)


def _dispatch_kernel(pos_ref, cnt_ref, off_ref, nused_ref, xm_hbm, xs_hbm, inbuf, zbuf, isem, rsem, zsem,
                     *, n_steps, n_tiles):
    zbuf[...] = _packed_zeros(zbuf.shape)
    nused = nused_ref[0]
    pad_bits = TM_E.bit_length() - 1

    def pad_copies(e, fn):
        cnt = cnt_ref[e]
        npad = (TM_E - cnt % TM_E) % TM_E
        start = off_ref[e] + cnt
        for b in range(pad_bits):
            size = 1 << b
            at = start + ((npad >> (b + 1)) << (b + 1))

            @pl.when(((npad >> b) & 1) == 1)
            def _():
                fn(pltpu.make_async_copy(zbuf.at[pl.ds(0, size * ROWS), :],
                                         xs_hbm.at[pl.ds(at * ROWS, size * ROWS), :], zsem))

    def tile_copy(tile):
        return pltpu.make_async_copy(zbuf, xs_hbm.at[pl.ds(tile * TM_E * ROWS, TM_E * ROWS), :], zsem)

    def loop(lo, hi, body):
        lax.fori_loop(lo, hi, lambda k, c: (body(k), c)[1], 0)

    loop(0, N_EXPERTS, lambda e: pad_copies(e, lambda cp: cp.start()))
    loop(nused, n_tiles, lambda tile: tile_copy(tile).start())

    def in_copy(step, slot):
        return pltpu.make_async_copy(xm_hbm.at[pl.ds(step * COPY_TILE * ROWS, COPY_TILE * ROWS), :],
                                     inbuf.at[slot], isem.at[slot])

    def rows_wait(slot):
        for _ in range(2):
            pltpu.make_async_copy(inbuf.at[slot], xs_hbm.at[pl.ds(0, COPY_TILE * ROWS), :], rsem.at[slot]).wait()

    in_copy(0, 0).start()

    def step(s):
        slot = s % DISPATCH_SLOTS
        nxt = (s + 1) % DISPATCH_SLOTS

        @pl.when(s >= DISPATCH_SLOTS - 1)
        def _():
            rows_wait(nxt)

        @pl.when(s + 1 < n_steps)
        def _():
            in_copy(s + 1, nxt).start()

        in_copy(s, slot).wait()
        base = s * COPY_TILE * 2

        def body(r, carry):
            for k in range(2):
                p = pos_ref[base + 2 * r + k]
                pltpu.make_async_copy(inbuf.at[slot, pl.ds(r * ROWS, ROWS), :],
                                      xs_hbm.at[pl.ds(p * ROWS, ROWS), :], rsem.at[slot]).start()
            return carry

        lax.fori_loop(0, COPY_TILE, body, 0, unroll=4)

    loop(0, n_steps, step)
    for back in range(DISPATCH_SLOTS - 1, 0, -1):
        rows_wait((n_steps - back) % DISPATCH_SLOTS)
    loop(0, N_EXPERTS, lambda e: pad_copies(e, lambda cp: cp.wait()))
    loop(nused, n_tiles, lambda tile: tile_copy(tile).wait())


def _dispatch(pos, cnt, off, nused, xm, *, n_tiles):
    n_tokens = xm.shape[0] // ROWS
    assert n_tokens % COPY_TILE == 0 and n_tokens // COPY_TILE >= DISPATCH_SLOTS
    smem = pl.BlockSpec(memory_space=pltpu.SMEM)
    return pl.pallas_call(
        functools.partial(_dispatch_kernel, n_steps=n_tokens // COPY_TILE, n_tiles=n_tiles),
        in_specs=[smem, smem, smem, smem, pl.BlockSpec(memory_space=pl.ANY)],
        out_specs=pl.BlockSpec(memory_space=pl.ANY),
        out_shape=jax.ShapeDtypeStruct((n_tiles * TM_E * ROWS, LANES), PACKED),
        scratch_shapes=[
            pltpu.VMEM((DISPATCH_SLOTS, COPY_TILE * ROWS, LANES), PACKED),
            pltpu.VMEM((TM_E * ROWS, LANES), PACKED),
            pltpu.SemaphoreType.DMA((DISPATCH_SLOTS,)),
            pltpu.SemaphoreType.DMA((DISPATCH_SLOTS,)),
            pltpu.SemaphoreType.DMA(()),
        ],
        compiler_params=pltpu.CompilerParams(vmem_limit_bytes=VMEM_LIMIT),
        name="dispatch",
    )(pos, cnt, off, nused, xm)


def _expert_kernel(te_ref, nused_ref, xs_ref, wg_hbm, wu_hbm, wd_hbm, y_ref,
                   wgf, wuf, wdf, wgb, wub, wdb, wsem, seen_ref):
    i = pl.program_id(0)
    nused = nused_ref[0]

    def weight_copies(e, slot):
        return [pltpu.make_async_copy(src.at[0, e], dst.at[slot], wsem.at[slot, k])
                for k, (src, dst) in enumerate(((wg_hbm, wgf), (wu_hbm, wuf), (wd_hbm, wdf)))]

    @pl.when(i == 0)
    def _():
        seen_ref[0] = 0
        for cp in weight_copies(te_ref[0], 0):
            cp.start()

    @pl.when(i < nused)
    def _():
        e = te_ref[i]
        new_expert = jnp.logical_or(i == 0, e != te_ref[jnp.maximum(i - 1, 0)])

        @pl.when(new_expert)
        def _():
            slot = seen_ref[0] % 2
            seen_ref[0] = seen_ref[0] + 1
            nxt = lax.while_loop(lambda j: jnp.logical_and(j < nused, te_ref[jnp.minimum(j, nused - 1)] == e),
                                 lambda j: j + 1, i + 1)

            @pl.when(nxt < nused)
            def _():
                for cp in weight_copies(te_ref[jnp.minimum(nxt, nused - 1)], 1 - slot):
                    cp.start()

            for cp in weight_copies(e, slot):
                cp.wait()
            wgb[...] = wgf[slot].astype(BF16)
            wub[...] = wuf[slot].astype(BF16)
            wdb[...] = wdf[slot].astype(BF16)

        x = _load_token_major(xs_ref, TM_E).astype(BF16)
        g = jnp.dot(x, wgb[...], preferred_element_type=F32)
        u = jnp.dot(x, wub[...], preferred_element_type=F32)
        hid = (g * _sigmoid(g)) * u
        _store_token_major(y_ref, jnp.dot(hid.astype(BF16), wdb[...], preferred_element_type=F32))

    @pl.when(i >= nused)
    def _():
        y_ref[...] = _packed_zeros(y_ref.shape)


def _experts(te, nused, xs, wg, wu, wd, *, n_tiles):
    grid_spec = pltpu.PrefetchScalarGridSpec(
        num_scalar_prefetch=2,
        grid=(n_tiles,),
        in_specs=[
            pl.BlockSpec((TM_E * ROWS, LANES), lambda i, te, nu: (jnp.minimum(i, nu[0] - 1), 0)),
            pl.BlockSpec(memory_space=pl.ANY),
            pl.BlockSpec(memory_space=pl.ANY),
            pl.BlockSpec(memory_space=pl.ANY),
        ],
        out_specs=pl.BlockSpec((TM_E * ROWS, LANES), lambda i, te, nu: (i, 0)),
        scratch_shapes=[
            pltpu.VMEM((2, D_MODEL, D_EXPERT), F32),
            pltpu.VMEM((2, D_MODEL, D_EXPERT), F32),
            pltpu.VMEM((2, D_EXPERT, D_MODEL), F32),
            pltpu.VMEM((D_MODEL, D_EXPERT), BF16),
            pltpu.VMEM((D_MODEL, D_EXPERT), BF16),
            pltpu.VMEM((D_EXPERT, D_MODEL), BF16),
            pltpu.SemaphoreType.DMA((2, 3)),
            pltpu.SMEM((1,), jnp.int32),
        ],
    )
    return pl.pallas_call(
        _expert_kernel,
        grid_spec=grid_spec,
        out_shape=jax.ShapeDtypeStruct((n_tiles * TM_E * ROWS, LANES), PACKED),
        compiler_params=pltpu.CompilerParams(
            dimension_semantics=("arbitrary",), vmem_limit_bytes=VMEM_LIMIT),
        name="experts",
    )(te, nused, xs, wg, wu, wd)


def _tail_kernel(pos_ref, ys_hbm, h_ref, route_ref, pp_ref, ps_ref, gp_ref, wpg_ref, wpp_ref, gf_ref,
                 yp_ref, ysm_ref, ybuf, sem, h2_ref, a_ref, proj_ref, *, n_prompt_tiles, n_tiles):
    i = pl.program_id(0)
    slot = i % 2
    is_prompt = i < n_prompt_tiles

    def issue(tile, dst_slot):
        base = tile * TM * 2

        def body(r, carry):
            for k in range(2):
                p = pos_ref[base + 2 * r + k]
                pltpu.make_async_copy(ys_hbm.at[pl.ds(p * ROWS, ROWS), :],
                                      ybuf.at[dst_slot, k, pl.ds(r * ROWS, ROWS), :],
                                      sem.at[dst_slot]).start()
            return carry

        lax.fori_loop(0, TM, body, 0, unroll=4)

    def slot_wait(s):
        for k in range(2):
            pltpu.make_async_copy(ys_hbm.at[pl.ds(0, TM * ROWS), :], ybuf.at[s, k], sem.at[s]).wait()

    def row_sumsq(acc):
        return jnp.sum(acc, axis=-1, keepdims=True) * (1.0 / D_MODEL)

    @pl.when(i == 0)
    def _():
        issue(0, 0)

    slot_wait(slot)

    w0 = jnp.broadcast_to(route_ref[:, 2:3], (TM, LANES))
    w1 = jnp.broadcast_to(route_ref[:, 3:4], (TM, LANES))
    acc = jnp.zeros((TM, LANES), F32)
    for s in range(ROWS):
        word0 = ybuf[slot, 0, pl.ds(s, TM, stride=ROWS), :]
        word1 = ybuf[slot, 1, pl.ds(s, TM, stride=ROWS), :]
        for idx in range(2):
            cols = slice((idx * ROWS + s) * LANES, (idx * ROWS + s + 1) * LANES)
            y0 = pltpu.unpack_elementwise(word0, index=idx, packed_dtype=BF16, unpacked_dtype=F32)
            y1 = pltpu.unpack_elementwise(word1, index=idx, packed_dtype=BF16, unpacked_dtype=F32)
            h2 = h_ref[:, cols] + (w0 * y0 + w1 * y1)
            h2_ref[:, cols] = h2
            acc = acc + h2 * h2
    r = lax.rsqrt(row_sumsq(acc) + EPS)
    a_ref[...] = (h2_ref[...] * r * gp_ref[...]).astype(BF16)
    p = jnp.where(is_prompt, pp_ref[...], ps_ref[...]).astype(BF16)
    proj_ref[...] = jnp.dot(p, wpp_ref[...], preferred_element_type=F32)

    nxt_base = jnp.minimum(i + 1, n_tiles - 1) * TM * 2
    rows_per_chunk = TM // TAIL_CHUNKS
    width = D_MODEL // TAIL_CHUNKS
    acc = jnp.zeros((TM, LANES), F32)
    for c in range(TAIL_CHUNKS):
        for rr in range(rows_per_chunk):
            row = c * rows_per_chunk + rr
            for k in range(2):
                p_row = pos_ref[nxt_base + 2 * row + k]
                pltpu.make_async_copy(ys_hbm.at[pl.ds(p_row * ROWS, ROWS), :],
                                      ybuf.at[1 - slot, k, pl.ds(row * ROWS, ROWS), :],
                                      sem.at[1 - slot]).start()
        cols = slice(c * width, (c + 1) * width)
        gate = _sigmoid(jnp.dot(a_ref[...], wpg_ref[:, cols], preferred_element_type=F32))
        h3 = h2_ref[:, cols] + gate * proj_ref[:, cols]
        h2_ref[:, cols] = h3
        for j in range(width // LANES):
            part = h3[:, j * LANES:(j + 1) * LANES]
            acc = acc + part * part
    scale = lax.rsqrt(row_sumsq(acc) + EPS)

    @pl.when(is_prompt)
    def _():
        yp_ref[...] = h2_ref[...] * scale * gf_ref[...]

    @pl.when(jnp.logical_not(is_prompt))
    def _():
        ysm_ref[...] = h2_ref[...] * scale * gf_ref[...]

    @pl.when(i == n_tiles - 1)
    def _():
        slot_wait(1 - slot)


def _tail(pos, ys, h, route, pp, ps, gp, wpg, wpp, gf):
    tp, ts = pp.shape[0], ps.shape[0]
    npt, nst = tp // TM, ts // TM
    ple = pp.shape[1]
    once = pl.Buffered(1)

    def pidx(i):
        return jnp.minimum(i, npt - 1)

    def sidx(i):
        return jnp.maximum(i - npt, 0)

    grid_spec = pltpu.PrefetchScalarGridSpec(
        num_scalar_prefetch=1,
        grid=(npt + nst,),
        in_specs=[
            pl.BlockSpec(memory_space=pl.ANY),
            pl.BlockSpec((TM, D_MODEL), lambda i, pos: (i, 0)),
            pl.BlockSpec((TM, LANES), lambda i, pos: (i, 0)),
            pl.BlockSpec((TM, ple), lambda i, pos: (pidx(i), 0)),
            pl.BlockSpec((TM, ple), lambda i, pos: (sidx(i), 0)),
            pl.BlockSpec((1, D_MODEL), lambda i, pos: (0, 0), pipeline_mode=once),
            pl.BlockSpec((D_MODEL, D_MODEL), lambda i, pos: (0, 0), pipeline_mode=once),
            pl.BlockSpec((ple, D_MODEL), lambda i, pos: (0, 0), pipeline_mode=once),
            pl.BlockSpec((1, D_MODEL), lambda i, pos: (0, 0), pipeline_mode=once),
        ],
        out_specs=[
            pl.BlockSpec((TM, D_MODEL), lambda i, pos: (pidx(i), 0)),
            pl.BlockSpec((TM, D_MODEL), lambda i, pos: (sidx(i), 0)),
        ],
        scratch_shapes=[
            pltpu.VMEM((2, 2, TM * ROWS, LANES), PACKED),
            pltpu.SemaphoreType.DMA((2,)),
            pltpu.VMEM((TM, D_MODEL), F32),
            pltpu.VMEM((TM, D_MODEL), BF16),
            pltpu.VMEM((TM, D_MODEL), F32),
        ],
    )
    return pl.pallas_call(
        functools.partial(_tail_kernel, n_prompt_tiles=npt, n_tiles=npt + nst),
        grid_spec=grid_spec,
        out_shape=[jax.ShapeDtypeStruct((tp, D_MODEL), F32), jax.ShapeDtypeStruct((ts, D_MODEL), F32)],
        compiler_params=pltpu.CompilerParams(
            dimension_semantics=("arbitrary",), vmem_limit_bytes=VMEM_LIMIT),
        name="tail",
    )(pos, ys, h, route, pp, ps, gp, wpg, wpp, gf)


def _route_tables(route, n_tiles):
    e = route[:, 0:2].astype(jnp.int32).reshape(-1)
    onehot = (e[:, None] == jnp.arange(N_EXPERTS, dtype=jnp.int32)[None, :]).astype(jnp.int32)
    cum = jnp.cumsum(onehot, axis=0)
    counts = cum[-1]
    rank = jnp.sum((cum - onehot) * onehot, axis=1)
    padded = ((counts + TM_E - 1) // TM_E) * TM_E
    ends = jnp.cumsum(padded)
    offs = ends - padded
    pos = jnp.sum(onehot * offs[None, :], axis=1) + rank
    nused = ends[-1] // TM_E
    tile_id = jnp.arange(n_tiles, dtype=jnp.int32)
    te = jnp.sum((tile_id[:, None] * TM_E >= ends[None, :]).astype(jnp.int32), axis=1)
    te_last = jnp.sum((((nused - 1) * TM_E) >= ends).astype(jnp.int32))
    te = jnp.where(tile_id < nused, te, te_last)
    i32 = lambda a: a.astype(jnp.int32)
    return i32(te), i32(nused).reshape(1), i32(pos), i32(counts), i32(offs)


def kernel(x_prompt, x_sample, state_conv, p_prompt, p_sample, g_mix, w_in, ln_v_g, ln_v_b, w_spatial, b_spatial, conv_w, w_br_a, w_br_b, w_out, g_moe, w_router_group, b_router_group, w_router_expert, b_router_expert, w_exp_gate, w_exp_up, w_exp_down, g_ple, w_ple_gate, w_ple_proj, g_final):
    depth = w_in.shape[0]
    assert depth == 1, "single-layer step only"
    nb, seq, _ = x_prompt.shape
    ns, dseq, _ = x_sample.shape
    assert dseq == 8 and seq % TM == 0 and (ns * dseq) % TM_NORM == 0 and (nb * seq) % TM_NORM == 0
    tp, ts = nb * seq, ns * dseq
    t = tp + ts
    xp = x_prompt.reshape(tp, D_MODEL)
    xs = x_sample.reshape(ts, D_MODEL)

    row = lambda a: a.reshape(1, -1)
    tril = jnp.tril(jnp.ones((CHUNK, CHUNK), dtype=bool))
    w_sp = jnp.where(tril[None], w_spatial[0], 0.0)
    eye = jnp.eye(CHUNK // dseq, dtype=F32)
    w_sp_s = jnp.einsum("ij,gts->gitjs", eye, w_sp[:, :dseq, :dseq]).reshape(G_A, CHUNK, CHUNK)
    wsp = jnp.stack([w_sp, w_sp_s]).astype(BF16)
    b_p = jnp.repeat(b_spatial[0].T, D_GROUP_A, axis=1)
    b_s = jnp.tile(b_p[:dseq], (CHUNK // dseq, 1))
    bsp = jnp.stack([b_p, b_s])
    wr = jnp.zeros((D_MODEL, LANES), F32)
    wr = wr.at[:, :N_GROUPS].set(w_router_group[0]).at[:, N_GROUPS:N_GROUPS + N_EXPERTS].set(w_router_expert[0])
    br = jnp.zeros((1, LANES), F32)
    br = br.at[0, :N_GROUPS].set(b_router_group[0]).at[0, N_GROUPS:N_GROUPS + N_EXPERTS].set(b_router_expert[0])
    f = jnp.pad(state_conv[0], ((0, 0), (0, dseq - (CONV_W - 1)), (0, 0))).reshape(ts, D_CONV)

    n = _norm(xp, xs, row(g_mix[0]))
    z = _inproj(n, w_in)
    h, xm, route, vnp, vns, qp, qs = _mixer(
        z, xp, xs, f, row(ln_v_g[0]), row(ln_v_b[0]), conv_w[0], wsp, bsp,
        w_br_a[0].astype(BF16), w_br_b[0].astype(BF16), w_out[0].astype(BF16),
        row(g_moe[0]), wr.astype(BF16), br, n_batch=nb)

    n_tiles = (2 * t) // TM_E + N_EXPERTS
    te, nused, pos, counts, offs = _route_tables(route, n_tiles)
    xsort = _dispatch(pos, counts, offs, nused, xm, n_tiles=n_tiles)
    ys = _experts(te, nused, xsort, w_exp_gate, w_exp_up, w_exp_down, n_tiles=n_tiles)

    yp, ysm = _tail(pos, ys, h, route, p_prompt[0].reshape(tp, -1), p_sample[0].reshape(ts, -1),
                    row(g_ple[0]), w_ple_gate[0].astype(BF16), w_ple_proj[0].astype(BF16), row(g_final))

    y_prompt = yp.reshape(nb, seq, D_MODEL)
    y_sample = ysm.reshape(ns, dseq, D_MODEL)
    conv_state_prompt = qp[None]
    conv_state_sample = qs.reshape(ns, dseq, D_CONV)[:, dseq - (CONV_W - 1):][None]
    v_rows_prompt = vnp[None]
    v_rows_sample = vns.reshape(ns, dseq, D_HALF)[None]
    return (y_prompt, y_sample, conv_state_prompt, conv_state_sample, v_rows_prompt, v_rows_sample)
```

```python
import functools

import jax
import jax.numpy as jnp
from jax import lax
from jax.experimental import pallas as pl
from jax.experimental.pallas import tpu as pltpu

D_MODEL = 2048
D_HALF = D_MODEL // 2
D_CONV = D_MODEL // 2
G_A = 4
D_GROUP_A = D_HALF // G_A
CHUNK = 128
CONV_W = 3
N_GROUPS = 4
EXPERTS_PER_GROUP = 8
N_EXPERTS = N_GROUPS * EXPERTS_PER_GROUP
D_EXPERT = D_MODEL // 4
D_IN_TOTAL = 2 * D_HALF + 3 * D_CONV + 2 * D_MODEL
EPS = 1e-6

BF16 = jnp.bfloat16
F32 = jnp.float32

LANES = 128
VMEM_LIMIT = 60 * 1024 * 1024

TM_NORM = 512
TN_IN = 1024
TM = 256
TM_E = 256
ROWS = D_MODEL // (2 * LANES)
PACKED = jnp.uint32
COPY_TILE = 256
DISPATCH_SLOTS = 3
TAIL_CHUNKS = 8


def _rms(x, g):
    return x * lax.rsqrt(jnp.mean(x * x, axis=-1, keepdims=True) + EPS) * g


def _sigmoid(x):
    return 1.0 / (1.0 + jnp.exp(-x))


def _store_token_major(ref, x):
    m = x.shape[0]
    for s in range(ROWS):
        lo = x[:, s * LANES:(s + 1) * LANES]
        hi = x[:, (ROWS + s) * LANES:(ROWS + s + 1) * LANES]
        ref[pl.ds(s, m, stride=ROWS), :] = pltpu.pack_elementwise([lo, hi], packed_dtype=BF16)


def _packed_zeros(shape):
    zero = jnp.zeros(shape, F32)
    return pltpu.pack_elementwise([zero, zero], packed_dtype=BF16)


def _unpack_words(words):
    half = lambda idx: [pltpu.unpack_elementwise(w, index=idx, packed_dtype=BF16, unpacked_dtype=F32)
                        for w in words]
    return jnp.concatenate(half(0) + half(1), axis=1)


def _load_token_major(ref, m):
    return _unpack_words([ref[pl.ds(s, m, stride=ROWS), :] for s in range(ROWS)])


def _norm_kernel(xp_ref, xs_ref, g_ref, o_ref, *, n_prompt_tiles):
    i = pl.program_id(0)

    def run(x_ref):
        o_ref[...] = _rms(x_ref[...], g_ref[...]).astype(BF16)

    @pl.when(i < n_prompt_tiles)
    def _():
        run(xp_ref)

    @pl.when(i >= n_prompt_tiles)
    def _():
        run(xs_ref)


def _norm(xp, xs, g):
    tp, ts = xp.shape[0], xs.shape[0]
    npt, nst = tp // TM_NORM, ts // TM_NORM
    return pl.pallas_call(
        functools.partial(_norm_kernel, n_prompt_tiles=npt),
        grid=(npt + nst,),
        in_specs=[
            pl.BlockSpec((TM_NORM, D_MODEL), lambda i: (jnp.minimum(i, npt - 1), 0)),
            pl.BlockSpec((TM_NORM, D_MODEL), lambda i: (jnp.maximum(i - npt, 0), 0)),
            pl.BlockSpec((1, D_MODEL), lambda i: (0, 0)),
        ],
        out_specs=pl.BlockSpec((TM_NORM, D_MODEL), lambda i: (i, 0)),
        out_shape=jax.ShapeDtypeStruct((tp + ts, D_MODEL), BF16),
        compiler_params=pltpu.CompilerParams(
            dimension_semantics=("arbitrary",), vmem_limit_bytes=VMEM_LIMIT),
        name="norm",
    )(xp, xs, g)


def _inproj_kernel(n_ref, w_ref, z_ref, wb_ref):
    @pl.when(pl.program_id(1) == 0)
    def _():
        wb_ref[...] = w_ref[...].astype(BF16)

    z_ref[...] = jnp.dot(n_ref[...], wb_ref[...], preferred_element_type=F32).astype(BF16)


def _inproj(n, w_in):
    t = n.shape[0]
    return pl.pallas_call(
        _inproj_kernel,
        grid=(D_IN_TOTAL // TN_IN, t // TM_NORM),
        in_specs=[
            pl.BlockSpec((TM_NORM, D_MODEL), lambda j, i: (i, 0)),
            pl.BlockSpec((None, D_MODEL, TN_IN), lambda j, i: (0, 0, j)),
        ],
        out_specs=pl.BlockSpec((TM_NORM, TN_IN), lambda j, i: (i, j)),
        out_shape=jax.ShapeDtypeStruct((t, D_IN_TOTAL), BF16),
        scratch_shapes=[pltpu.VMEM((D_MODEL, TN_IN), BF16)],
        compiler_params=pltpu.CompilerParams(
            dimension_semantics=("arbitrary", "arbitrary"), vmem_limit_bytes=VMEM_LIMIT),
        name="in_proj",
    )(n, w_in)


def _route(lg):
    shape = lg.shape
    lane = lax.broadcasted_iota(jnp.int32, shape, 1)
    lanef = lane.astype(F32)
    neg = jnp.float32(-jnp.inf)
    far = jnp.float32(LANES)
    is_g = lane < N_GROUPS
    lgm = jnp.where(is_g, lg, neg)
    gmax = jnp.max(lgm, axis=-1, keepdims=True)
    gsel = jnp.min(jnp.where(lgm == gmax, lanef, far), axis=-1, keepdims=True)
    p_group = 1.0 / jnp.sum(jnp.where(is_g, jnp.exp(lgm - gmax), 0.0), axis=-1, keepdims=True)
    lane_group = ((lane - N_GROUPS) >> 3).astype(F32)
    is_e = lane_group == gsel
    le = jnp.where(is_e, lg, neg)
    m1 = jnp.max(le, axis=-1, keepdims=True)
    i1 = jnp.min(jnp.where(le == m1, lanef, far), axis=-1, keepdims=True)
    le2 = jnp.where(lanef == i1, neg, le)
    m2 = jnp.max(le2, axis=-1, keepdims=True)
    i2 = jnp.min(jnp.where(le2 == m2, lanef, far), axis=-1, keepdims=True)
    e21 = jnp.exp(m2 - m1)
    w1 = 1.0 / (1.0 + e21)
    w2 = e21 * w1
    out = jnp.where(lane == 0, i1 - N_GROUPS,
                    jnp.where(lane == 1, i2 - N_GROUPS,
                              jnp.where(lane == 2, p_group * w1,
                                        jnp.where(lane == 3, p_group * w2, 0.0))))
    return out


def _mixer_kernel(z_ref, xp_ref, xs_ref, f_ref, lng_ref, lnb_ref, cw_ref, wsp_ref, bsp_ref,
                  wa_ref, wb_ref, wo_ref, gm_ref, wr_ref, br_ref,
                  h_ref, xm_ref, route_ref, vnp_ref, vns_ref, qp_ref, qs_ref,
                  qprev_ref, ya_ref, yb_ref, yc_ref, m_ref, *, n_prompt_tiles, tiles_per_seq):
    i = pl.program_id(0)
    is_prompt = i < n_prompt_tiles
    c_u, c_v, c_b, c_c, c_x = (k * D_HALF for k in range(5))
    c_ga = c_x + D_CONV
    c_gb = c_ga + D_MODEL

    v = z_ref[:, c_v:c_v + D_HALF].astype(F32)
    mu = jnp.mean(v, axis=-1, keepdims=True)
    vc = v - mu
    vn = vc * lax.rsqrt(jnp.mean(vc * vc, axis=-1, keepdims=True) + EPS) * lng_ref[...] + lnb_ref[...]
    vnb = vn.astype(BF16)
    for c in range(TM // CHUNK):
        rows = slice(c * CHUNK, (c + 1) * CHUNK)
        for g in range(G_A):
            cols = slice(g * D_GROUP_A, (g + 1) * D_GROUP_A)
            s = jnp.dot(wsp_ref[g], vnb[rows, cols], preferred_element_type=F32) + bsp_ref[:, cols]
            u = z_ref[rows, c_u + g * D_GROUP_A:c_u + (g + 1) * D_GROUP_A].astype(F32)
            ya_ref[rows, cols] = (u * s).astype(BF16)

    q = z_ref[:, c_c:c_c + D_CONV].astype(F32) * z_ref[:, c_x:c_x + D_CONV].astype(F32)
    q1 = pltpu.roll(q, 1, 0)
    q2 = pltpu.roll(q, 2, 0)
    row = lax.broadcasted_iota(jnp.int32, q.shape, 0)
    cw0, cw1, cw2 = cw_ref[0:1, :], cw_ref[1:2, :], cw_ref[2:3, :]

    @pl.when(is_prompt)
    def _():
        seq_start = i % tiles_per_seq == 0
        p6 = jnp.where(seq_start, 0.0, qprev_ref[6:7, :])
        p7 = jnp.where(seq_start, 0.0, qprev_ref[7:8, :])
        a1 = jnp.where(row == 0, p7, q1)
        a2 = jnp.where(row == 0, p6, jnp.where(row == 1, p7, q2))
        yc_ref[...] = cw0 * a2 + cw1 * a1 + cw2 * q
        qprev_ref[...] = q[TM - 8:TM, :]
        qp_ref[...] = q[TM - (CONV_W - 1):TM, :]
        vnp_ref[...] = vn[TM - CHUNK:TM, :]

    @pl.when(jnp.logical_not(is_prompt))
    def _():
        f = f_ref[...]
        f1 = pltpu.roll(f, TM - 1, 0)
        r8 = row & 7
        a1 = jnp.where(r8 == 0, f1, q1)
        a2 = jnp.where(r8 < 2, f, q2)
        yc_ref[...] = cw0 * a2 + cw1 * a1 + cw2 * q
        qs_ref[...] = q
        vns_ref[...] = vn

    yb_ref[...] = (z_ref[:, c_b:c_b + D_CONV].astype(F32) * yc_ref[...]).astype(BF16)

    half = D_MODEL // 2
    for hh in range(2):
        cols = slice(hh * half, (hh + 1) * half)
        a = jnp.dot(ya_ref[...], wa_ref[:, cols], preferred_element_type=F32)
        b = jnp.dot(yb_ref[...], wb_ref[:, cols], preferred_element_type=F32)
        ga = z_ref[:, c_ga + hh * half:c_ga + (hh + 1) * half].astype(F32)
        gb = z_ref[:, c_gb + hh * half:c_gb + (hh + 1) * half].astype(F32)
        m_ref[:, cols] = (_sigmoid(ga) * a + _sigmoid(gb) * b).astype(BF16)

    x = jnp.where(is_prompt, xp_ref[...], xs_ref[...])
    h = x + jnp.dot(m_ref[...], wo_ref[...], preferred_element_type=F32)
    h_ref[...] = h
    xm = _rms(h, gm_ref[...])
    _store_token_major(xm_ref, xm)
    lg =jnp.dot(xm.astype(BF16), wr_ref[...], preferred_element_type=F32) + br_ref[...]
    route_ref[...] = _route(lg)


def _mixer(z, xp, xs, f, lng, lnb, cw, wsp, bsp, wa, wb, wo, gm, wr, br, *, n_batch):
    tp, ts = xp.shape[0], xs.shape[0]
    t = tp + ts
    npt, nst = tp // TM, ts // TM
    tiles_per_seq = tp // n_batch // TM
    once = pl.Buffered(1)

    def pidx(i):
        return jnp.minimum(i, npt - 1)

    def sidx(i):
        return jnp.maximum(i - npt, 0)

    const = lambda shape: pl.BlockSpec(shape, lambda i: (0,) * len(shape), pipeline_mode=once)
    in_specs = [
        pl.BlockSpec((TM, D_IN_TOTAL), lambda i: (i, 0)),
        pl.BlockSpec((TM, D_MODEL), lambda i: (pidx(i), 0)),
        pl.BlockSpec((TM, D_MODEL), lambda i: (sidx(i), 0)),
        pl.BlockSpec((TM, D_CONV), lambda i: (sidx(i), 0)),
        const((1, D_HALF)), const((1, D_HALF)), const((CONV_W, D_CONV)),
        pl.BlockSpec((None, G_A, CHUNK, CHUNK), lambda i: (jnp.where(i < npt, 0, 1), 0, 0, 0)),
        pl.BlockSpec((None, CHUNK, D_HALF), lambda i: (jnp.where(i < npt, 0, 1), 0, 0)),
        const((D_HALF, D_MODEL)), const((D_CONV, D_MODEL)), const((D_MODEL, D_MODEL)),
        const((1, D_MODEL)), const((D_MODEL, LANES)), const((1, LANES)),
    ]
    out_specs = [
        pl.BlockSpec((TM, D_MODEL), lambda i: (i, 0)),
        pl.BlockSpec((TM * ROWS, LANES), lambda i: (i, 0)),
        pl.BlockSpec((TM, LANES), lambda i: (i, 0)),
        pl.BlockSpec((None, CHUNK, D_HALF), lambda i: (pidx(i) // tiles_per_seq, 0, 0)),
        pl.BlockSpec((TM, D_HALF), lambda i: (sidx(i), 0)),
        pl.BlockSpec((None, CONV_W - 1, D_CONV), lambda i: (pidx(i) // tiles_per_seq, 0, 0)),
        pl.BlockSpec((TM, D_CONV), lambda i: (sidx(i), 0)),
    ]
    out_shape = [
        jax.ShapeDtypeStruct((t, D_MODEL), F32),
        jax.ShapeDtypeStruct((t * ROWS, LANES), PACKED),
        jax.ShapeDtypeStruct((t, LANES), F32),
        jax.ShapeDtypeStruct((n_batch, CHUNK, D_HALF), F32),
        jax.ShapeDtypeStruct((ts, D_HALF), F32),
        jax.ShapeDtypeStruct((n_batch, CONV_W - 1, D_CONV), F32),
        jax.ShapeDtypeStruct((ts, D_CONV), F32),
    ]
    return pl.pallas_call(
        functools.partial(_mixer_kernel, n_prompt_tiles=npt, tiles_per_seq=tiles_per_seq),
        grid=(npt + nst,),
        in_specs=in_specs,
        out_specs=out_specs,
        out_shape=out_shape,
        scratch_shapes=[
            pltpu.VMEM((8, D_CONV), F32),
            pltpu.VMEM((TM, D_HALF), BF16),
            pltpu.VMEM((TM, D_CONV), BF16),
            pltpu.VMEM((TM, D_CONV), F32),
            pltpu.VMEM((TM, D_MODEL), BF16),
        ],
        compiler_params=pltpu.CompilerParams(
            dimension_semantics=("arbitrary",), vmem_limit_bytes=VMEM_LIMIT),
        name="mixer",
    )(z, xp, xs, f, lng, lnb, cw, wsp, bsp, wa, wb, wo, gm, wr, br)


def _dispatch_kernel(pos_ref, cnt_ref, off_ref, nused_ref, xm_hbm, xs_hbm, inbuf, zbuf, isem, rsem, zsem,
                     *, n_steps, n_tiles):
    zbuf[...] = _packed_zeros(zbuf.shape)
    nused = nused_ref[0]
    pad_bits = TM_E.bit_length() - 1

    def pad_copies(e, fn):
        cnt = cnt_ref[e]
        npad = (TM_E - cnt % TM_E) % TM_E
        start = off_ref[e] + cnt
        for b in range(pad_bits):
            size = 1 << b
            at = start + ((npad >> (b + 1)) << (b + 1))

            @pl.when(((npad >> b) & 1) == 1)
            def _():
                fn(pltpu.make_async_copy(zbuf.at[pl.ds(0, size * ROWS), :],
                                         xs_hbm.at[pl.ds(at * ROWS, size * ROWS), :], zsem))

    def tile_copy(tile):
        return pltpu.make_async_copy(zbuf, xs_hbm.at[pl.ds(tile * TM_E * ROWS, TM_E * ROWS), :], zsem)

    def loop(lo, hi, body):
        lax.fori_loop(lo, hi, lambda k, c: (body(k), c)[1], 0)

    loop(0, N_EXPERTS, lambda e: pad_copies(e, lambda cp: cp.start()))
    loop(nused, n_tiles, lambda tile: tile_copy(tile).start())

    def in_copy(step, slot):
        return pltpu.make_async_copy(xm_hbm.at[pl.ds(step * COPY_TILE * ROWS, COPY_TILE * ROWS), :],
                                     inbuf.at[slot], isem.at[slot])

    def rows_wait(slot):
        for _ in range(2):
            pltpu.make_async_copy(inbuf.at[slot], xs_hbm.at[pl.ds(0, COPY_TILE * ROWS), :], rsem.at[slot]).wait()

    in_copy(0, 0).start()

    def step(s):
        slot = s % DISPATCH_SLOTS
        nxt = (s + 1) % DISPATCH_SLOTS

        @pl.when(s >= DISPATCH_SLOTS - 1)
        def _():
            rows_wait(nxt)

        @pl.when(s + 1 < n_steps)
        def _():
            in_copy(s + 1, nxt).start()

        in_copy(s, slot).wait()
        base = s * COPY_TILE * 2

        def body(r, carry):
            for k in range(2):
                p = pos_ref[base + 2 * r + k]
                pltpu.make_async_copy(inbuf.at[slot, pl.ds(r * ROWS, ROWS), :],
                                      xs_hbm.at[pl.ds(p * ROWS, ROWS), :], rsem.at[slot]).start(priority=k)
            return carry

        lax.fori_loop(0, COPY_TILE, body, 0, unroll=4)

    loop(0, n_steps, step)
    for back in range(DISPATCH_SLOTS - 1, 0, -1):
        rows_wait((n_steps - back) % DISPATCH_SLOTS)
    loop(0, N_EXPERTS, lambda e: pad_copies(e, lambda cp: cp.wait()))
    loop(nused, n_tiles, lambda tile: tile_copy(tile).wait())


def _dispatch(pos, cnt, off, nused, xm, *, n_tiles):
    n_tokens = xm.shape[0] // ROWS
    assert n_tokens % COPY_TILE == 0 and n_tokens // COPY_TILE >= DISPATCH_SLOTS
    smem = pl.BlockSpec(memory_space=pltpu.SMEM)
    return pl.pallas_call(
        functools.partial(_dispatch_kernel, n_steps=n_tokens // COPY_TILE, n_tiles=n_tiles),
        in_specs=[smem, smem, smem, smem, pl.BlockSpec(memory_space=pl.ANY)],
        out_specs=pl.BlockSpec(memory_space=pl.ANY),
        out_shape=jax.ShapeDtypeStruct((n_tiles * TM_E * ROWS, LANES), PACKED),
        scratch_shapes=[
            pltpu.VMEM((DISPATCH_SLOTS, COPY_TILE * ROWS, LANES), PACKED),
            pltpu.VMEM((TM_E * ROWS, LANES), PACKED),
            pltpu.SemaphoreType.DMA((DISPATCH_SLOTS,)),
            pltpu.SemaphoreType.DMA((DISPATCH_SLOTS,)),
            pltpu.SemaphoreType.DMA(()),
        ],
        compiler_params=pltpu.CompilerParams(vmem_limit_bytes=VMEM_LIMIT),
        name="dispatch",
    )(pos, cnt, off, nused, xm)


def _expert_kernel(te_ref, nused_ref, xs_ref, wg_hbm, wu_hbm, wd_hbm, y_ref,
                   wgf, wuf, wdf, wgb, wub, wdb, wsem, seen_ref):
    i = pl.program_id(0)
    nused = nused_ref[0]

    def weight_copies(e, slot):
        return [pltpu.make_async_copy(src.at[0, e], dst.at[slot], wsem.at[slot, k])
                for k, (src, dst) in enumerate(((wg_hbm, wgf), (wu_hbm, wuf), (wd_hbm, wdf)))]

    @pl.when(i == 0)
    def _():
        seen_ref[0] = 0
        for cp in weight_copies(te_ref[0], 0):
            cp.start()

    @pl.when(i < nused)
    def _():
        e = te_ref[i]
        new_expert = jnp.logical_or(i == 0, e != te_ref[jnp.maximum(i - 1, 0)])

        @pl.when(new_expert)
        def _():
            slot = seen_ref[0] % 2
            seen_ref[0] = seen_ref[0] + 1
            nxt = lax.while_loop(lambda j: jnp.logical_and(j < nused, te_ref[jnp.minimum(j, nused - 1)] == e),
                                 lambda j: j + 1, i + 1)

            @pl.when(nxt < nused)
            def _():
                for cp in weight_copies(te_ref[jnp.minimum(nxt, nused - 1)], 1 - slot):
                    cp.start(priority=1)

            for cp in weight_copies(e, slot):
                cp.wait()
            wgb[...] = wgf[slot].astype(BF16)
            wub[...] = wuf[slot].astype(BF16)
            wdb[...] = wdf[slot].astype(BF16)

        x = _load_token_major(xs_ref, TM_E).astype(BF16)
        g = jnp.dot(x, wgb[...], preferred_element_type=F32)
        u = jnp.dot(x, wub[...], preferred_element_type=F32)
        hid = (g * _sigmoid(g)) * u
        _store_token_major(y_ref, jnp.dot(hid.astype(BF16), wdb[...], preferred_element_type=F32))

    @pl.when(i >= nused)
    def _():
        y_ref[...] = _packed_zeros(y_ref.shape)


def _experts(te, nused, xs, wg, wu, wd, *, n_tiles):
    grid_spec = pltpu.PrefetchScalarGridSpec(
        num_scalar_prefetch=2,
        grid=(n_tiles,),
        in_specs=[
            pl.BlockSpec((TM_E * ROWS, LANES), lambda i, te, nu: (jnp.minimum(i, nu[0] - 1), 0)),
            pl.BlockSpec(memory_space=pl.ANY),
            pl.BlockSpec(memory_space=pl.ANY),
            pl.BlockSpec(memory_space=pl.ANY),
        ],
        out_specs=pl.BlockSpec((TM_E * ROWS, LANES), lambda i, te, nu: (i, 0)),
        scratch_shapes=[
            pltpu.VMEM((2, D_MODEL, D_EXPERT), F32),
            pltpu.VMEM((2, D_MODEL, D_EXPERT), F32),
            pltpu.VMEM((2, D_EXPERT, D_MODEL), F32),
            pltpu.VMEM((D_MODEL, D_EXPERT), BF16),
            pltpu.VMEM((D_MODEL, D_EXPERT), BF16),
            pltpu.VMEM((D_EXPERT, D_MODEL), BF16),
            pltpu.SemaphoreType.DMA((2, 3)),
            pltpu.SMEM((1,), jnp.int32),
        ],
    )
    return pl.pallas_call(
        _expert_kernel,
        grid_spec=grid_spec,
        out_shape=jax.ShapeDtypeStruct((n_tiles * TM_E * ROWS, LANES), PACKED),
        compiler_params=pltpu.CompilerParams(
            dimension_semantics=("arbitrary",), vmem_limit_bytes=VMEM_LIMIT),
        name="experts",
    )(te, nused, xs, wg, wu, wd)


def _tail_kernel(pos_ref, ys_hbm, h_ref, route_ref, pp_ref, ps_ref, gp_ref, wpg_ref, wpp_ref, gf_ref,
                 yp_ref, ysm_ref, ybuf, sem, h2_ref, a_ref, proj_ref, *, n_prompt_tiles, n_tiles):
    i = pl.program_id(0)
    slot = i % 2
    is_prompt = i < n_prompt_tiles

    def issue(tile, dst_slot):
        base = tile * TM * 2

        def body(r, carry):
            for k in range(2):
                p = pos_ref[base + 2 * r + k]
                pltpu.make_async_copy(ys_hbm.at[pl.ds(p * ROWS, ROWS), :],
                                      ybuf.at[dst_slot, k, pl.ds(r * ROWS, ROWS), :],
                                      sem.at[dst_slot]).start()
            return carry

        lax.fori_loop(0, TM, body, 0, unroll=4)

    def slot_wait(s):
        for k in range(2):
            pltpu.make_async_copy(ys_hbm.at[pl.ds(0, TM * ROWS), :], ybuf.at[s, k], sem.at[s]).wait()

    def row_sumsq(acc):
        return jnp.sum(acc, axis=-1, keepdims=True) * (1.0 / D_MODEL)

    @pl.when(i == 0)
    def _():
        issue(0, 0)

    slot_wait(slot)

    w0 = jnp.broadcast_to(route_ref[:, 2:3], (TM, LANES))
    w1 = jnp.broadcast_to(route_ref[:, 3:4], (TM, LANES))
    acc = jnp.zeros((TM, LANES), F32)
    for s in range(ROWS):
        word0 = ybuf[slot, 0, pl.ds(s, TM, stride=ROWS), :]
        word1 = ybuf[slot, 1, pl.ds(s, TM, stride=ROWS), :]
        for idx in range(2):
            cols = slice((idx * ROWS + s) * LANES, (idx * ROWS + s + 1) * LANES)
            y0 = pltpu.unpack_elementwise(word0, index=idx, packed_dtype=BF16, unpacked_dtype=F32)
            y1 = pltpu.unpack_elementwise(word1, index=idx, packed_dtype=BF16, unpacked_dtype=F32)
            h2 = h_ref[:, cols] + (w0 * y0 + w1 * y1)
            h2_ref[:, cols] = h2
            acc = acc + h2 * h2
    r = lax.rsqrt(row_sumsq(acc) + EPS)
    a_ref[...] = (h2_ref[...] * r * gp_ref[...]).astype(BF16)
    p = jnp.where(is_prompt, pp_ref[...], ps_ref[...]).astype(BF16)
    proj_ref[...] = jnp.dot(p, wpp_ref[...], preferred_element_type=F32)

    nxt_base = jnp.minimum(i + 1, n_tiles - 1) * TM * 2
    rows_per_chunk = TM // TAIL_CHUNKS
    width = D_MODEL // TAIL_CHUNKS
    acc = jnp.zeros((TM, LANES), F32)
    for c in range(TAIL_CHUNKS):
        for rr in range(rows_per_chunk):
            row = c * rows_per_chunk + rr
            for k in range(2):
                p_row = pos_ref[nxt_base + 2 * row + k]
                pltpu.make_async_copy(ys_hbm.at[pl.ds(p_row * ROWS, ROWS), :],
                                      ybuf.at[1 - slot, k, pl.ds(row * ROWS, ROWS), :],
                                      sem.at[1 - slot]).start()
        cols = slice(c * width, (c + 1) * width)
        gate = _sigmoid(jnp.dot(a_ref[...], wpg_ref[:, cols], preferred_element_type=F32))
        h3 = h2_ref[:, cols] + gate * proj_ref[:, cols]
        h2_ref[:, cols] = h3
        for j in range(width // LANES):
            part = h3[:, j * LANES:(j + 1) * LANES]
            acc = acc + part * part
    scale = lax.rsqrt(row_sumsq(acc) + EPS)

    @pl.when(is_prompt)
    def _():
        yp_ref[...] = h2_ref[...] * scale * gf_ref[...]

    @pl.when(jnp.logical_not(is_prompt))
    def _():
        ysm_ref[...] = h2_ref[...] * scale * gf_ref[...]

    @pl.when(i == n_tiles - 1)
    def _():
        slot_wait(1 - slot)


def _tail(pos, ys, h, route, pp, ps, gp, wpg, wpp, gf):
    tp, ts = pp.shape[0], ps.shape[0]
    npt, nst = tp // TM, ts // TM
    ple = pp.shape[1]
    once = pl.Buffered(1)

    def pidx(i):
        return jnp.minimum(i, npt - 1)

    def sidx(i):
        return jnp.maximum(i - npt, 0)

    grid_spec = pltpu.PrefetchScalarGridSpec(
        num_scalar_prefetch=1,
        grid=(npt + nst,),
        in_specs=[
            pl.BlockSpec(memory_space=pl.ANY),
            pl.BlockSpec((TM, D_MODEL), lambda i, pos: (i, 0)),
            pl.BlockSpec((TM, LANES), lambda i, pos: (i, 0)),
            pl.BlockSpec((TM, ple), lambda i, pos: (pidx(i), 0)),
            pl.BlockSpec((TM, ple), lambda i, pos: (sidx(i), 0)),
            pl.BlockSpec((1, D_MODEL), lambda i, pos: (0, 0), pipeline_mode=once),
            pl.BlockSpec((D_MODEL, D_MODEL), lambda i, pos: (0, 0), pipeline_mode=once),
            pl.BlockSpec((ple, D_MODEL), lambda i, pos: (0, 0), pipeline_mode=once),
            pl.BlockSpec((1, D_MODEL), lambda i, pos: (0, 0), pipeline_mode=once),
        ],
        out_specs=[
            pl.BlockSpec((TM, D_MODEL), lambda i, pos: (pidx(i), 0)),
            pl.BlockSpec((TM, D_MODEL), lambda i, pos: (sidx(i), 0)),
        ],
        scratch_shapes=[
            pltpu.VMEM((2, 2, TM * ROWS, LANES), PACKED),
            pltpu.SemaphoreType.DMA((2,)),
            pltpu.VMEM((TM, D_MODEL), F32),
            pltpu.VMEM((TM, D_MODEL), BF16),
            pltpu.VMEM((TM, D_MODEL), F32),
        ],
    )
    return pl.pallas_call(
        functools.partial(_tail_kernel, n_prompt_tiles=npt, n_tiles=npt + nst),
        grid_spec=grid_spec,
        out_shape=[jax.ShapeDtypeStruct((tp, D_MODEL), F32), jax.ShapeDtypeStruct((ts, D_MODEL), F32)],
        compiler_params=pltpu.CompilerParams(
            dimension_semantics=("arbitrary",), vmem_limit_bytes=VMEM_LIMIT),
        name="tail",
    )(pos, ys, h, route, pp, ps, gp, wpg, wpp, gf)


def _route_tables(route, n_tiles):
    e = route[:, 0:2].astype(jnp.int32).reshape(-1)
    onehot = (e[:, None] == jnp.arange(N_EXPERTS, dtype=jnp.int32)[None, :]).astype(jnp.int32)
    cum = jnp.cumsum(onehot, axis=0)
    counts = cum[-1]
    rank = jnp.sum((cum - onehot) * onehot, axis=1)
    padded = ((counts + TM_E - 1) // TM_E) * TM_E
    ends = jnp.cumsum(padded)
    offs = ends - padded
    pos = jnp.sum(onehot * offs[None, :], axis=1) + rank
    nused = ends[-1] // TM_E
    tile_id = jnp.arange(n_tiles, dtype=jnp.int32)
    te = jnp.sum((tile_id[:, None] * TM_E >= ends[None, :]).astype(jnp.int32), axis=1)
    te_last = jnp.sum((((nused - 1) * TM_E) >= ends).astype(jnp.int32))
    te = jnp.where(tile_id < nused, te, te_last)
    i32 = lambda a: a.astype(jnp.int32)
    return i32(te), i32(nused).reshape(1), i32(pos), i32(counts), i32(offs)


def kernel(x_prompt, x_sample, state_conv, p_prompt, p_sample, g_mix, w_in, ln_v_g, ln_v_b, w_spatial, b_spatial, conv_w, w_br_a, w_br_b, w_out, g_moe, w_router_group, b_router_group, w_router_expert, b_router_expert, w_exp_gate, w_exp_up, w_exp_down, g_ple, w_ple_gate, w_ple_proj, g_final):
    depth = w_in.shape[0]
    assert depth == 1, "single-layer step only"
    nb, seq, _ = x_prompt.shape
    ns, dseq, _ = x_sample.shape
    assert dseq == 8 and seq % TM == 0 and (ns * dseq) % TM_NORM == 0 and (nb * seq) % TM_NORM == 0
    tp, ts = nb * seq, ns * dseq
    t = tp + ts
    xp = x_prompt.reshape(tp, D_MODEL)
    xs = x_sample.reshape(ts, D_MODEL)

    row = lambda a: a.reshape(1, -1)
    tril = jnp.tril(jnp.ones((CHUNK, CHUNK), dtype=bool))
    w_sp = jnp.where(tril[None], w_spatial[0], 0.0)
    eye = jnp.eye(CHUNK // dseq, dtype=F32)
    w_sp_s = jnp.einsum("ij,gts->gitjs", eye, w_sp[:, :dseq, :dseq]).reshape(G_A, CHUNK, CHUNK)
    wsp = jnp.stack([w_sp, w_sp_s]).astype(BF16)
    b_p = jnp.repeat(b_spatial[0].T, D_GROUP_A, axis=1)
    b_s = jnp.tile(b_p[:dseq], (CHUNK // dseq, 1))
    bsp = jnp.stack([b_p, b_s])
    wr = jnp.zeros((D_MODEL, LANES), F32)
    wr = wr.at[:, :N_GROUPS].set(w_router_group[0]).at[:, N_GROUPS:N_GROUPS + N_EXPERTS].set(w_router_expert[0])
    br = jnp.zeros((1, LANES), F32)
    br = br.at[0, :N_GROUPS].set(b_router_group[0]).at[0, N_GROUPS:N_GROUPS + N_EXPERTS].set(b_router_expert[0])
    f = jnp.pad(state_conv[0], ((0, 0), (0, dseq - (CONV_W - 1)), (0, 0))).reshape(ts, D_CONV)

    n = _norm(xp, xs, row(g_mix[0]))
    z = _inproj(n, w_in)
    h, xm, route, vnp, vns, qp, qs = _mixer(
        z, xp, xs, f, row(ln_v_g[0]), row(ln_v_b[0]), conv_w[0], wsp, bsp,
        w_br_a[0].astype(BF16), w_br_b[0].astype(BF16), w_out[0].astype(BF16),
        row(g_moe[0]), wr.astype(BF16), br, n_batch=nb)

    n_tiles = (2 * t) // TM_E + N_EXPERTS
    te, nused, pos, counts, offs = _route_tables(route, n_tiles)
    xsort = _dispatch(pos, counts, offs, nused, xm, n_tiles=n_tiles)
    ys = _experts(te, nused, xsort, w_exp_gate, w_exp_up, w_exp_down, n_tiles=n_tiles)

    yp, ysm = _tail(pos, ys, h, route, p_prompt[0].reshape(tp, -1), p_sample[0].reshape(ts, -1),
                    row(g_ple[0]), w_ple_gate[0].astype(BF16), w_ple_proj[0].astype(BF16), row(g_final))

    y_prompt = yp.reshape(nb, seq, D_MODEL)
    y_sample = ysm.reshape(ns, dseq, D_MODEL)
    conv_state_prompt = qp[None]
    conv_state_sample = qs.reshape(ns, dseq, D_CONV)[:, dseq - (CONV_W - 1):][None]
    v_rows_prompt = vnp[None]
    v_rows_sample = vns.reshape(ns, dseq, D_HALF)[None]
    return (y_prompt, y_sample, conv_state_prompt, conv_state_sample, v_rows_prompt, v_rows_sample)
```

```python
import functools

import jax
import jax.numpy as jnp
from jax import lax
from jax.experimental import pallas as pl
from jax.experimental.pallas import tpu as pltpu

D_MODEL = 2048
D_HALF = D_MODEL // 2
D_CONV = D_MODEL // 2
G_A = 4
D_GROUP_A = D_HALF // G_A
CHUNK = 128
CONV_W = 3
N_GROUPS = 4
EXPERTS_PER_GROUP = 8
N_EXPERTS = N_GROUPS * EXPERTS_PER_GROUP
D_EXPERT = D_MODEL // 4
D_IN_TOTAL = 2 * D_HALF + 3 * D_CONV + 2 * D_MODEL
EPS = 1e-6

BF16 = jnp.bfloat16
F32 = jnp.float32

LANES = 128
VMEM_LIMIT = 60 * 1024 * 1024

TM_NORM = 512
TM_IN = 1024
CAST_ROWS = 128
TN_IN = 1024
TM = 256
TM_E = 256
ROWS = D_MODEL // (2 * LANES)
PACKED = jnp.uint32
COPY_TILE = 256
DISPATCH_SLOTS = 3
TAIL_CHUNKS = 8


def _rms(x, g):
    return x * lax.rsqrt(jnp.mean(x * x, axis=-1, keepdims=True) + EPS) * g


def _sigmoid(x):
    return 1.0 / (1.0 + jnp.exp(-x))


def _store_token_major(ref, x):
    m = x.shape[0]
    for s in range(ROWS):
        lo = x[:, s * LANES:(s + 1) * LANES]
        hi = x[:, (ROWS + s) * LANES:(ROWS + s + 1) * LANES]
        ref[pl.ds(s, m, stride=ROWS), :] = pltpu.pack_elementwise([lo, hi], packed_dtype=BF16)


def _packed_zeros(shape):
    zero = jnp.zeros(shape, F32)
    return pltpu.pack_elementwise([zero, zero], packed_dtype=BF16)


def _unpack_words(words):
    half = lambda idx: [pltpu.unpack_elementwise(w, index=idx, packed_dtype=BF16, unpacked_dtype=F32)
                        for w in words]
    return jnp.concatenate(half(0) + half(1), axis=1)


def _load_token_major(ref, m):
    return _unpack_words([ref[pl.ds(s, m, stride=ROWS), :] for s in range(ROWS)])


def _norm_kernel(xp_ref, xs_ref, g_ref, o_ref, *, n_prompt_tiles):
    i = pl.program_id(0)

    def run(x_ref):
        o_ref[...] = _rms(x_ref[...], g_ref[...]).astype(BF16)

    @pl.when(i < n_prompt_tiles)
    def _():
        run(xp_ref)

    @pl.when(i >= n_prompt_tiles)
    def _():
        run(xs_ref)


def _norm(xp, xs, g):
    tp, ts = xp.shape[0], xs.shape[0]
    npt, nst = tp // TM_NORM, ts // TM_NORM
    return pl.pallas_call(
        functools.partial(_norm_kernel, n_prompt_tiles=npt),
        grid=(npt + nst,),
        in_specs=[
            pl.BlockSpec((TM_NORM, D_MODEL), lambda i: (jnp.minimum(i, npt - 1), 0)),
            pl.BlockSpec((TM_NORM, D_MODEL), lambda i: (jnp.maximum(i - npt, 0), 0)),
            pl.BlockSpec((1, D_MODEL), lambda i: (0, 0)),
        ],
        out_specs=pl.BlockSpec((TM_NORM, D_MODEL), lambda i: (i, 0)),
        out_shape=jax.ShapeDtypeStruct((tp + ts, D_MODEL), BF16),
        compiler_params=pltpu.CompilerParams(
            dimension_semantics=("arbitrary",), vmem_limit_bytes=VMEM_LIMIT),
        name="norm",
    )(xp, xs, g)


def _inproj_kernel(n_ref, w_ref, *refs, n_side):
    side_in, z_ref, side_out, wb_ref = refs[:n_side], refs[n_side], refs[n_side + 1:-1], refs[-1]

    @pl.when(pl.program_id(1) == 0)
    def _():
        wb_ref[...] = w_ref[...].astype(BF16)

    z_ref[...] = jnp.dot(n_ref[...], wb_ref[...], preferred_element_type=F32).astype(BF16)
    for src, dst in zip(side_in, side_out):
        dst[...] = src[...].astype(BF16)


def _inproj(n, w_in, side_weights):
    t = n.shape[0]
    n_j, n_i = D_IN_TOTAL // TN_IN, t // TM_IN
    starts, start = [], 0
    for w in side_weights:
        assert w.shape[1] % CAST_ROWS == 0
        starts.append(start)
        start += w.shape[1] // CAST_ROWS
    assert start <= n_j * n_i, "not enough grid steps for the weight casts"

    def side_index(k):
        n_blk = side_weights[k].shape[1] // CAST_ROWS
        return lambda j, i: jnp.clip(j * n_i + i - starts[k], 0, n_blk - 1)

    side_in = [pl.BlockSpec((None, CAST_ROWS, w.shape[2]), lambda j, i, f=side_index(k): (0, f(j, i), 0))
               for k, w in enumerate(side_weights)]
    side_out = [pl.BlockSpec((CAST_ROWS, w.shape[2]), lambda j, i, f=side_index(k): (f(j, i), 0))
                for k, w in enumerate(side_weights)]
    outs = pl.pallas_call(
        functools.partial(_inproj_kernel, n_side=len(side_weights)),
        grid=(n_j, n_i),
        in_specs=[
            pl.BlockSpec((TM_IN, D_MODEL), lambda j, i: (i, 0)),
            pl.BlockSpec((None, D_MODEL, TN_IN), lambda j, i: (0, 0, j)),
        ] + side_in,
        out_specs=[pl.BlockSpec((TM_IN, TN_IN), lambda j, i: (i, j))] + side_out,
        out_shape=[jax.ShapeDtypeStruct((t, D_IN_TOTAL), BF16)]
        + [jax.ShapeDtypeStruct(w.shape[1:], BF16) for w in side_weights],
        scratch_shapes=[pltpu.VMEM((D_MODEL, TN_IN), BF16)],
        compiler_params=pltpu.CompilerParams(
            dimension_semantics=("arbitrary", "arbitrary"), vmem_limit_bytes=VMEM_LIMIT),
        name="in_proj",
    )(n, w_in, *side_weights)
    return outs[0], outs[1:]


def _route(lg):
    shape = lg.shape
    lane = lax.broadcasted_iota(jnp.int32, shape, 1)
    lanef = lane.astype(F32)
    neg = jnp.float32(-jnp.inf)
    far = jnp.float32(LANES)
    is_g = lane < N_GROUPS
    lgm = jnp.where(is_g, lg, neg)
    gmax = jnp.max(lgm, axis=-1, keepdims=True)
    gsel = jnp.min(jnp.where(lgm == gmax, lanef, far), axis=-1, keepdims=True)
    p_group = 1.0 / jnp.sum(jnp.where(is_g, jnp.exp(lgm - gmax), 0.0), axis=-1, keepdims=True)
    lane_group = ((lane - N_GROUPS) >> 3).astype(F32)
    is_e = lane_group == gsel
    le = jnp.where(is_e, lg, neg)
    m1 = jnp.max(le, axis=-1, keepdims=True)
    i1 = jnp.min(jnp.where(le == m1, lanef, far), axis=-1, keepdims=True)
    le2 = jnp.where(lanef == i1, neg, le)
    m2 = jnp.max(le2, axis=-1, keepdims=True)
    i2 = jnp.min(jnp.where(le2 == m2, lanef, far), axis=-1, keepdims=True)
    e21 = jnp.exp(m2 - m1)
    w1 = 1.0 / (1.0 + e21)
    w2 = e21 * w1
    out = jnp.where(lane == 0, i1 - N_GROUPS,
                    jnp.where(lane == 1, i2 - N_GROUPS,
                              jnp.where(lane == 2, p_group * w1,
                                        jnp.where(lane == 3, p_group * w2, 0.0))))
    return out


def _mixer_kernel(z_ref, xp_ref, xs_ref, f_ref, lng_ref, lnb_ref, cw_ref, wsp_ref, bsp_ref,
                  wa_ref, wb_ref, wo_ref, gm_ref, wr_ref, br_ref,
                  h_ref, xm_ref, route_ref, vnp_ref, vns_ref, qp_ref, qs_ref,
                  qprev_ref, ya_ref, yb_ref, yc_ref, m_ref, *, n_prompt_tiles, tiles_per_seq):
    i = pl.program_id(0)
    is_prompt = i < n_prompt_tiles
    c_u, c_v, c_b, c_c, c_x = (k * D_HALF for k in range(5))
    c_ga = c_x + D_CONV
    c_gb = c_ga + D_MODEL

    v = z_ref[:, c_v:c_v + D_HALF].astype(F32)
    mu = jnp.mean(v, axis=-1, keepdims=True)
    vc = v - mu
    vn = vc * lax.rsqrt(jnp.mean(vc * vc, axis=-1, keepdims=True) + EPS) * lng_ref[...] + lnb_ref[...]
    vnb = vn.astype(BF16)
    for c in range(TM // CHUNK):
        rows = slice(c * CHUNK, (c + 1) * CHUNK)
        for g in range(G_A):
            cols = slice(g * D_GROUP_A, (g + 1) * D_GROUP_A)
            s = jnp.dot(wsp_ref[g], vnb[rows, cols], preferred_element_type=F32) + bsp_ref[:, cols]
            u = z_ref[rows, c_u + g * D_GROUP_A:c_u + (g + 1) * D_GROUP_A].astype(F32)
            ya_ref[rows, cols] = (u * s).astype(BF16)

    q = z_ref[:, c_c:c_c + D_CONV].astype(F32) * z_ref[:, c_x:c_x + D_CONV].astype(F32)
    q1 = pltpu.roll(q, 1, 0)
    q2 = pltpu.roll(q, 2, 0)
    row = lax.broadcasted_iota(jnp.int32, q.shape, 0)
    cw0, cw1, cw2 = cw_ref[0:1, :], cw_ref[1:2, :], cw_ref[2:3, :]

    @pl.when(is_prompt)
    def _():
        seq_start = i % tiles_per_seq == 0
        p6 = jnp.where(seq_start, 0.0, qprev_ref[6:7, :])
        p7 = jnp.where(seq_start, 0.0, qprev_ref[7:8, :])
        a1 = jnp.where(row == 0, p7, q1)
        a2 = jnp.where(row == 0, p6, jnp.where(row == 1, p7, q2))
        yc_ref[...] = cw0 * a2 + cw1 * a1 + cw2 * q
        qprev_ref[...] = q[TM - 8:TM, :]
        qp_ref[...] = q[TM - (CONV_W - 1):TM, :]
        vnp_ref[...] = vn[TM - CHUNK:TM, :]

    @pl.when(jnp.logical_not(is_prompt))
    def _():
        f = f_ref[...]
        f1 = pltpu.roll(f, TM - 1, 0)
        r8 = row & 7
        a1 = jnp.where(r8 == 0, f1, q1)
        a2 = jnp.where(r8 < 2, f, q2)
        yc_ref[...] = cw0 * a2 + cw1 * a1 + cw2 * q
        qs_ref[...] = q
        vns_ref[...] = vn

    yb_ref[...] = (z_ref[:, c_b:c_b + D_CONV].astype(F32) * yc_ref[...]).astype(BF16)

    half = D_MODEL // 2
    for hh in range(2):
        cols = slice(hh * half, (hh + 1) * half)
        a = jnp.dot(ya_ref[...], wa_ref[:, cols], preferred_element_type=F32)
        b = jnp.dot(yb_ref[...], wb_ref[:, cols], preferred_element_type=F32)
        ga = z_ref[:, c_ga + hh * half:c_ga + (hh + 1) * half].astype(F32)
        gb = z_ref[:, c_gb + hh * half:c_gb + (hh + 1) * half].astype(F32)
        m_ref[:, cols] = (_sigmoid(ga) * a + _sigmoid(gb) * b).astype(BF16)

    x = jnp.where(is_prompt, xp_ref[...], xs_ref[...])
    h = x + jnp.dot(m_ref[...], wo_ref[...], preferred_element_type=F32)
    h_ref[...] = h
    xm = _rms(h, gm_ref[...])
    _store_token_major(xm_ref, xm)
    lg =jnp.dot(xm.astype(BF16), wr_ref[...], preferred_element_type=F32) + br_ref[...]
    route_ref[...] = _route(lg)


def _mixer(z, xp, xs, f, lng, lnb, cw, wsp, bsp, wa, wb, wo, gm, wr, br, *, n_batch):
    tp, ts = xp.shape[0], xs.shape[0]
    t = tp + ts
    npt, nst = tp // TM, ts // TM
    tiles_per_seq = tp // n_batch // TM
    once = pl.Buffered(1)

    def pidx(i):
        return jnp.minimum(i, npt - 1)

    def sidx(i):
        return jnp.maximum(i - npt, 0)

    const = lambda shape: pl.BlockSpec(shape, lambda i: (0,) * len(shape), pipeline_mode=once)
    in_specs = [
        pl.BlockSpec((TM, D_IN_TOTAL), lambda i: (i, 0)),
        pl.BlockSpec((TM, D_MODEL), lambda i: (pidx(i), 0)),
        pl.BlockSpec((TM, D_MODEL), lambda i: (sidx(i), 0)),
        pl.BlockSpec((TM, D_CONV), lambda i: (sidx(i), 0)),
        const((1, D_HALF)), const((1, D_HALF)), const((CONV_W, D_CONV)),
        pl.BlockSpec((None, G_A, CHUNK, CHUNK), lambda i: (jnp.where(i < npt, 0, 1), 0, 0, 0)),
        pl.BlockSpec((None, CHUNK, D_HALF), lambda i: (jnp.where(i < npt, 0, 1), 0, 0)),
        const((D_HALF, D_MODEL)), const((D_CONV, D_MODEL)), const((D_MODEL, D_MODEL)),
        const((1, D_MODEL)), const((D_MODEL, LANES)), const((1, LANES)),
    ]
    out_specs = [
        pl.BlockSpec((TM, D_MODEL), lambda i: (i, 0)),
        pl.BlockSpec((TM * ROWS, LANES), lambda i: (i, 0)),
        pl.BlockSpec((TM, LANES), lambda i: (i, 0)),
        pl.BlockSpec((None, CHUNK, D_HALF), lambda i: (pidx(i) // tiles_per_seq, 0, 0)),
        pl.BlockSpec((TM, D_HALF), lambda i: (sidx(i), 0)),
        pl.BlockSpec((None, CONV_W - 1, D_CONV), lambda i: (pidx(i) // tiles_per_seq, 0, 0)),
        pl.BlockSpec((TM, D_CONV), lambda i: (sidx(i), 0)),
    ]
    out_shape = [
        jax.ShapeDtypeStruct((t, D_MODEL), F32),
        jax.ShapeDtypeStruct((t * ROWS, LANES), PACKED),
        jax.ShapeDtypeStruct((t, LANES), F32),
        jax.ShapeDtypeStruct((n_batch, CHUNK, D_HALF), F32),
        jax.ShapeDtypeStruct((ts, D_HALF), F32),
        jax.ShapeDtypeStruct((n_batch, CONV_W - 1, D_CONV), F32),
        jax.ShapeDtypeStruct((ts, D_CONV), F32),
    ]
    return pl.pallas_call(
        functools.partial(_mixer_kernel, n_prompt_tiles=npt, tiles_per_seq=tiles_per_seq),
        grid=(npt + nst,),
        in_specs=in_specs,
        out_specs=out_specs,
        out_shape=out_shape,
        scratch_shapes=[
            pltpu.VMEM((8, D_CONV), F32),
            pltpu.VMEM((TM, D_HALF), BF16),
            pltpu.VMEM((TM, D_CONV), BF16),
            pltpu.VMEM((TM, D_CONV), F32),
            pltpu.VMEM((TM, D_MODEL), BF16),
        ],
        compiler_params=pltpu.CompilerParams(
            dimension_semantics=("arbitrary",), vmem_limit_bytes=VMEM_LIMIT),
        name="mixer",
    )(z, xp, xs, f, lng, lnb, cw, wsp, bsp, wa, wb, wo, gm, wr, br)


def _dispatch_kernel(pos_ref, cnt_ref, off_ref, nused_ref, xm_hbm, xs_hbm, inbuf, zbuf, isem, rsem, zsem,
                     *, n_steps, n_tiles):
    zbuf[...] = _packed_zeros(zbuf.shape)
    nused = nused_ref[0]
    pad_bits = TM_E.bit_length() - 1

    def pad_copies(e, fn):
        cnt = cnt_ref[e]
        npad = (TM_E - cnt % TM_E) % TM_E
        start = off_ref[e] + cnt
        for b in range(pad_bits):
            size = 1 << b
            at = start + ((npad >> (b + 1)) << (b + 1))

            @pl.when(((npad >> b) & 1) == 1)
            def _():
                fn(pltpu.make_async_copy(zbuf.at[pl.ds(0, size * ROWS), :],
                                         xs_hbm.at[pl.ds(at * ROWS, size * ROWS), :], zsem))

    def tile_copy(tile):
        return pltpu.make_async_copy(zbuf, xs_hbm.at[pl.ds(tile * TM_E * ROWS, TM_E * ROWS), :], zsem)

    def loop(lo, hi, body):
        lax.fori_loop(lo, hi, lambda k, c: (body(k), c)[1], 0)

    loop(0, N_EXPERTS, lambda e: pad_copies(e, lambda cp: cp.start()))
    loop(nused, n_tiles, lambda tile: tile_copy(tile).start())

    def in_copy(step, slot):
        return pltpu.make_async_copy(xm_hbm.at[pl.ds(step * COPY_TILE * ROWS, COPY_TILE * ROWS), :],
                                     inbuf.at[slot], isem.at[slot])

    def rows_wait(slot):
        for _ in range(2):
            pltpu.make_async_copy(inbuf.at[slot], xs_hbm.at[pl.ds(0, COPY_TILE * ROWS), :], rsem.at[slot]).wait()

    in_copy(0, 0).start()

    def step(s):
        slot = s % DISPATCH_SLOTS
        nxt = (s + 1) % DISPATCH_SLOTS

        @pl.when(s >= DISPATCH_SLOTS - 1)
        def _():
            rows_wait(nxt)

        @pl.when(s + 1 < n_steps)
        def _():
            in_copy(s + 1, nxt).start()

        in_copy(s, slot).wait()
        base = s * COPY_TILE * 2

        def body(r, carry):
            for k in range(2):
                p = pos_ref[base + 2 * r + k]
                pltpu.make_async_copy(inbuf.at[slot, pl.ds(r * ROWS, ROWS), :],
                                      xs_hbm.at[pl.ds(p * ROWS, ROWS), :], rsem.at[slot]).start(priority=k)
            return carry

        lax.fori_loop(0, COPY_TILE, body, 0, unroll=4)

    loop(0, n_steps, step)
    for back in range(DISPATCH_SLOTS - 1, 0, -1):
        rows_wait((n_steps - back) % DISPATCH_SLOTS)
    loop(0, N_EXPERTS, lambda e: pad_copies(e, lambda cp: cp.wait()))
    loop(nused, n_tiles, lambda tile: tile_copy(tile).wait())


def _dispatch(pos, cnt, off, nused, xm, *, n_tiles):
    n_tokens = xm.shape[0] // ROWS
    assert n_tokens % COPY_TILE == 0 and n_tokens // COPY_TILE >= DISPATCH_SLOTS
    smem = pl.BlockSpec(memory_space=pltpu.SMEM)
    return pl.pallas_call(
        functools.partial(_dispatch_kernel, n_steps=n_tokens // COPY_TILE, n_tiles=n_tiles),
        in_specs=[smem, smem, smem, smem, pl.BlockSpec(memory_space=pl.ANY)],
        out_specs=pl.BlockSpec(memory_space=pl.ANY),
        out_shape=jax.ShapeDtypeStruct((n_tiles * TM_E * ROWS, LANES), PACKED),
        scratch_shapes=[
            pltpu.VMEM((DISPATCH_SLOTS, COPY_TILE * ROWS, LANES), PACKED),
            pltpu.VMEM((TM_E * ROWS, LANES), PACKED),
            pltpu.SemaphoreType.DMA((DISPATCH_SLOTS,)),
            pltpu.SemaphoreType.DMA((DISPATCH_SLOTS,)),
            pltpu.SemaphoreType.DMA(()),
        ],
        compiler_params=pltpu.CompilerParams(vmem_limit_bytes=VMEM_LIMIT),
        name="dispatch",
    )(pos, cnt, off, nused, xm)


def _expert_kernel(te_ref, nused_ref, xs_ref, wg_hbm, wu_hbm, wd_hbm, y_ref,
                   wgf, wuf, wdf, wgb, wub, wdb, wsem, seen_ref):
    i = pl.program_id(0)
    nused = nused_ref[0]

    def weight_copies(e, slot):
        return [pltpu.make_async_copy(src.at[0, e], dst.at[slot], wsem.at[slot, k])
                for k, (src, dst) in enumerate(((wg_hbm, wgf), (wu_hbm, wuf), (wd_hbm, wdf)))]

    @pl.when(i == 0)
    def _():
        seen_ref[0] = 0
        for cp in weight_copies(te_ref[0], 0):
            cp.start()

    @pl.when(i < nused)
    def _():
        e = te_ref[i]
        new_expert = jnp.logical_or(i == 0, e != te_ref[jnp.maximum(i - 1, 0)])

        @pl.when(new_expert)
        def _():
            slot = seen_ref[0] % 2
            seen_ref[0] = seen_ref[0] + 1
            nxt = lax.while_loop(lambda j: jnp.logical_and(j < nused, te_ref[jnp.minimum(j, nused - 1)] == e),
                                 lambda j: j + 1, i + 1)

            @pl.when(nxt < nused)
            def _():
                for cp in weight_copies(te_ref[jnp.minimum(nxt, nused - 1)], 1 - slot):
                    cp.start(priority=1)

            for cp in weight_copies(e, slot):
                cp.wait()
            wgb[...] = wgf[slot].astype(BF16)
            wub[...] = wuf[slot].astype(BF16)
            wdb[...] = wdf[slot].astype(BF16)

        x = _load_token_major(xs_ref, TM_E).astype(BF16)
        g = jnp.dot(x, wgb[...], preferred_element_type=F32)
        u = jnp.dot(x, wub[...], preferred_element_type=F32)
        hid = (g * _sigmoid(g)) * u
        _store_token_major(y_ref, jnp.dot(hid.astype(BF16), wdb[...], preferred_element_type=F32))

    @pl.when(i >= nused)
    def _():
        y_ref[...] = _packed_zeros(y_ref.shape)


def _experts(te, nused, xs, wg, wu, wd, *, n_tiles):
    grid_spec = pltpu.PrefetchScalarGridSpec(
        num_scalar_prefetch=2,
        grid=(n_tiles,),
        in_specs=[
            pl.BlockSpec((TM_E * ROWS, LANES), lambda i, te, nu: (jnp.minimum(i, nu[0] - 1), 0)),
            pl.BlockSpec(memory_space=pl.ANY),
            pl.BlockSpec(memory_space=pl.ANY),
            pl.BlockSpec(memory_space=pl.ANY),
        ],
        out_specs=pl.BlockSpec((TM_E * ROWS, LANES), lambda i, te, nu: (i, 0)),
        scratch_shapes=[
            pltpu.VMEM((2, D_MODEL, D_EXPERT), F32),
            pltpu.VMEM((2, D_MODEL, D_EXPERT), F32),
            pltpu.VMEM((2, D_EXPERT, D_MODEL), F32),
            pltpu.VMEM((D_MODEL, D_EXPERT), BF16),
            pltpu.VMEM((D_MODEL, D_EXPERT), BF16),
            pltpu.VMEM((D_EXPERT, D_MODEL), BF16),
            pltpu.SemaphoreType.DMA((2, 3)),
            pltpu.SMEM((1,), jnp.int32),
        ],
    )
    return pl.pallas_call(
        _expert_kernel,
        grid_spec=grid_spec,
        out_shape=jax.ShapeDtypeStruct((n_tiles * TM_E * ROWS, LANES), PACKED),
        compiler_params=pltpu.CompilerParams(
            dimension_semantics=("arbitrary",), vmem_limit_bytes=VMEM_LIMIT),
        name="experts",
    )(te, nused, xs, wg, wu, wd)


def _tail_kernel(pos_ref, ys_hbm, h_ref, route_ref, pp_ref, ps_ref, gp_ref, wpg_ref, wpp_ref, gf_ref,
                 yp_ref, ysm_ref, ybuf, sem, h2_ref, a_ref, proj_ref, *, n_prompt_tiles, n_tiles):
    i = pl.program_id(0)
    slot = i % 2
    is_prompt = i < n_prompt_tiles

    def issue(tile, dst_slot):
        base = tile * TM * 2

        def body(r, carry):
            for k in range(2):
                p = pos_ref[base + 2 * r + k]
                pltpu.make_async_copy(ys_hbm.at[pl.ds(p * ROWS, ROWS), :],
                                      ybuf.at[dst_slot, k, pl.ds(r * ROWS, ROWS), :],
                                      sem.at[dst_slot]).start()
            return carry

        lax.fori_loop(0, TM, body, 0, unroll=4)

    def slot_wait(s):
        for k in range(2):
            pltpu.make_async_copy(ys_hbm.at[pl.ds(0, TM * ROWS), :], ybuf.at[s, k], sem.at[s]).wait()

    def row_sumsq(acc):
        return jnp.sum(acc, axis=-1, keepdims=True) * (1.0 / D_MODEL)

    @pl.when(i == 0)
    def _():
        issue(0, 0)

    slot_wait(slot)

    w0 = jnp.broadcast_to(route_ref[:, 2:3], (TM, LANES))
    w1 = jnp.broadcast_to(route_ref[:, 3:4], (TM, LANES))
    acc = jnp.zeros((TM, LANES), F32)
    for s in range(ROWS):
        word0 = ybuf[slot, 0, pl.ds(s, TM, stride=ROWS), :]
        word1 = ybuf[slot, 1, pl.ds(s, TM, stride=ROWS), :]
        for idx in range(2):
            cols = slice((idx * ROWS + s) * LANES, (idx * ROWS + s + 1) * LANES)
            y0 = pltpu.unpack_elementwise(word0, index=idx, packed_dtype=BF16, unpacked_dtype=F32)
            y1 = pltpu.unpack_elementwise(word1, index=idx, packed_dtype=BF16, unpacked_dtype=F32)
            h2 = h_ref[:, cols] + (w0 * y0 + w1 * y1)
            h2_ref[:, cols] = h2
            acc = acc + h2 * h2
    r = lax.rsqrt(row_sumsq(acc) + EPS)
    a_ref[...] = (h2_ref[...] * r * gp_ref[...]).astype(BF16)
    p = jnp.where(is_prompt, pp_ref[...], ps_ref[...]).astype(BF16)
    proj_ref[...] = jnp.dot(p, wpp_ref[...], preferred_element_type=F32)

    nxt_base = jnp.minimum(i + 1, n_tiles - 1) * TM * 2
    rows_per_chunk = TM // TAIL_CHUNKS
    width = D_MODEL // TAIL_CHUNKS
    acc = jnp.zeros((TM, LANES), F32)
    for c in range(TAIL_CHUNKS):
        for rr in range(rows_per_chunk):
            row = c * rows_per_chunk + rr
            for k in range(2):
                p_row = pos_ref[nxt_base + 2 * row + k]
                pltpu.make_async_copy(ys_hbm.at[pl.ds(p_row * ROWS, ROWS), :],
                                      ybuf.at[1 - slot, k, pl.ds(row * ROWS, ROWS), :],
                                      sem.at[1 - slot]).start()
        cols = slice(c * width, (c + 1) * width)
        gate = _sigmoid(jnp.dot(a_ref[...], wpg_ref[:, cols], preferred_element_type=F32))
        h3 = h2_ref[:, cols] + gate * proj_ref[:, cols]
        h2_ref[:, cols] = h3
        for j in range(width // LANES):
            part = h3[:, j * LANES:(j + 1) * LANES]
            acc = acc + part * part
    scale = lax.rsqrt(row_sumsq(acc) + EPS)

    @pl.when(is_prompt)
    def _():
        yp_ref[...] = h2_ref[...] * scale * gf_ref[...]

    @pl.when(jnp.logical_not(is_prompt))
    def _():
        ysm_ref[...] = h2_ref[...] * scale * gf_ref[...]

    @pl.when(i == n_tiles - 1)
    def _():
        slot_wait(1 - slot)


def _tail(pos, ys, h, route, pp, ps, gp, wpg, wpp, gf):
    tp, ts = pp.shape[0], ps.shape[0]
    npt, nst = tp // TM, ts // TM
    ple = pp.shape[1]
    once = pl.Buffered(1)

    def pidx(i):
        return jnp.minimum(i, npt - 1)

    def sidx(i):
        return jnp.maximum(i - npt, 0)

    grid_spec = pltpu.PrefetchScalarGridSpec(
        num_scalar_prefetch=1,
        grid=(npt + nst,),
        in_specs=[
            pl.BlockSpec(memory_space=pl.ANY),
            pl.BlockSpec((TM, D_MODEL), lambda i, pos: (i, 0)),
            pl.BlockSpec((TM, LANES), lambda i, pos: (i, 0)),
            pl.BlockSpec((TM, ple), lambda i, pos: (pidx(i), 0)),
            pl.BlockSpec((TM, ple), lambda i, pos: (sidx(i), 0)),
            pl.BlockSpec((1, D_MODEL), lambda i, pos: (0, 0), pipeline_mode=once),
            pl.BlockSpec((D_MODEL, D_MODEL), lambda i, pos: (0, 0), pipeline_mode=once),
            pl.BlockSpec((ple, D_MODEL), lambda i, pos: (0, 0), pipeline_mode=once),
            pl.BlockSpec((1, D_MODEL), lambda i, pos: (0, 0), pipeline_mode=once),
        ],
        out_specs=[
            pl.BlockSpec((TM, D_MODEL), lambda i, pos: (pidx(i), 0)),
            pl.BlockSpec((TM, D_MODEL), lambda i, pos: (sidx(i), 0)),
        ],
        scratch_shapes=[
            pltpu.VMEM((2, 2, TM * ROWS, LANES), PACKED),
            pltpu.SemaphoreType.DMA((2,)),
            pltpu.VMEM((TM, D_MODEL), F32),
            pltpu.VMEM((TM, D_MODEL), BF16),
            pltpu.VMEM((TM, D_MODEL), F32),
        ],
    )
    return pl.pallas_call(
        functools.partial(_tail_kernel, n_prompt_tiles=npt, n_tiles=npt + nst),
        grid_spec=grid_spec,
        out_shape=[jax.ShapeDtypeStruct((tp, D_MODEL), F32), jax.ShapeDtypeStruct((ts, D_MODEL), F32)],
        compiler_params=pltpu.CompilerParams(
            dimension_semantics=("arbitrary",), vmem_limit_bytes=VMEM_LIMIT),
        name="tail",
    )(pos, ys, h, route, pp, ps, gp, wpg, wpp, gf)


def _route_tables(route, n_tiles):
    e = route[:, 0:2].astype(jnp.int32).reshape(-1)
    onehot = (e[:, None] == jnp.arange(N_EXPERTS, dtype=jnp.int32)[None, :]).astype(jnp.int32)
    cum = jnp.cumsum(onehot, axis=0)
    counts = cum[-1]
    rank = jnp.sum((cum - onehot) * onehot, axis=1)
    padded = ((counts + TM_E - 1) // TM_E) * TM_E
    ends = jnp.cumsum(padded)
    offs = ends - padded
    pos = jnp.sum(onehot * offs[None, :], axis=1) + rank
    nused = ends[-1] // TM_E
    tile_id = jnp.arange(n_tiles, dtype=jnp.int32)
    te = jnp.sum((tile_id[:, None] * TM_E >= ends[None, :]).astype(jnp.int32), axis=1)
    te_last = jnp.sum((((nused - 1) * TM_E) >= ends).astype(jnp.int32))
    te = jnp.where(tile_id < nused, te, te_last)
    i32 = lambda a: a.astype(jnp.int32)
    return i32(te), i32(nused).reshape(1), i32(pos), i32(counts), i32(offs)


def kernel(x_prompt, x_sample, state_conv, p_prompt, p_sample, g_mix, w_in, ln_v_g, ln_v_b, w_spatial, b_spatial, conv_w, w_br_a, w_br_b, w_out, g_moe, w_router_group, b_router_group, w_router_expert, b_router_expert, w_exp_gate, w_exp_up, w_exp_down, g_ple, w_ple_gate, w_ple_proj, g_final):
    depth = w_in.shape[0]
    assert depth == 1, "single-layer step only"
    nb, seq, _ = x_prompt.shape
    ns, dseq, _ = x_sample.shape
    assert dseq == 8 and seq % TM == 0 and (ns * dseq) % TM_NORM == 0 and (nb * seq) % TM_NORM == 0
    assert (nb * seq + ns * dseq) % TM_IN == 0
    tp, ts = nb * seq, ns * dseq
    t = tp + ts
    xp = x_prompt.reshape(tp, D_MODEL)
    xs = x_sample.reshape(ts, D_MODEL)

    row = lambda a: a.reshape(1, -1)
    tril = jnp.tril(jnp.ones((CHUNK, CHUNK), dtype=bool))
    w_sp = jnp.where(tril[None], w_spatial[0], 0.0)
    eye = jnp.eye(CHUNK // dseq, dtype=F32)
    w_sp_s = jnp.einsum("ij,gts->gitjs", eye, w_sp[:, :dseq, :dseq]).reshape(G_A, CHUNK, CHUNK)
    wsp = jnp.stack([w_sp, w_sp_s]).astype(BF16)
    b_p = jnp.repeat(b_spatial[0].T, D_GROUP_A, axis=1)
    b_s = jnp.tile(b_p[:dseq], (CHUNK // dseq, 1))
    bsp = jnp.stack([b_p, b_s])
    wr = jnp.zeros((D_MODEL, LANES), F32)
    wr = wr.at[:, :N_GROUPS].set(w_router_group[0]).at[:, N_GROUPS:N_GROUPS + N_EXPERTS].set(w_router_expert[0])
    br = jnp.zeros((1, LANES), F32)
    br = br.at[0, :N_GROUPS].set(b_router_group[0]).at[0, N_GROUPS:N_GROUPS + N_EXPERTS].set(b_router_expert[0])
    f = jnp.pad(state_conv[0], ((0, 0), (0, dseq - (CONV_W - 1)), (0, 0))).reshape(ts, D_CONV)

    n = _norm(xp, xs, row(g_mix[0]))
    z, (wa, wb, wo, wpg) = _inproj(n, w_in, [w_br_a, w_br_b, w_out, w_ple_gate])
    h, xm, route, vnp, vns, qp, qs = _mixer(
        z, xp, xs, f, row(ln_v_g[0]), row(ln_v_b[0]), conv_w[0], wsp, bsp, wa, wb, wo,
        row(g_moe[0]), wr.astype(BF16), br, n_batch=nb)

    n_tiles = (2 * t) // TM_E + N_EXPERTS
    te, nused, pos, counts, offs = _route_tables(route, n_tiles)
    xsort = _dispatch(pos, counts, offs, nused, xm, n_tiles=n_tiles)
    ys = _experts(te, nused, xsort, w_exp_gate, w_exp_up, w_exp_down, n_tiles=n_tiles)

    yp, ysm = _tail(pos, ys, h, route, p_prompt[0].reshape(tp, -1), p_sample[0].reshape(ts, -1),
                    row(g_ple[0]), wpg, w_ple_proj[0].astype(BF16), row(g_final))

    y_prompt = yp.reshape(nb, seq, D_MODEL)
    y_sample = ysm.reshape(ns, dseq, D_MODEL)
    conv_state_prompt = qp[None]
    conv_state_sample = qs.reshape(ns, dseq, D_CONV)[:, dseq - (CONV_W - 1):][None]
    v_rows_prompt = vnp[None]
    v_rows_sample = vns.reshape(ns, dseq, D_HALF)[None]
    return (y_prompt, y_sample, conv_state_prompt, conv_state_sample, v_rows_prompt, v_rows_sample)
```

```python
import functools

import jax
import jax.numpy as jnp
from jax import lax
from jax.experimental import pallas as pl
from jax.experimental.pallas import tpu as pltpu

D_MODEL = 2048
D_HALF = D_MODEL // 2
D_CONV = D_MODEL // 2
G_A = 4
D_GROUP_A = D_HALF // G_A
CHUNK = 128
CONV_W = 3
N_GROUPS = 4
EXPERTS_PER_GROUP = 8
N_EXPERTS = N_GROUPS * EXPERTS_PER_GROUP
D_EXPERT = D_MODEL // 4
D_IN_TOTAL = 2 * D_HALF + 3 * D_CONV + 2 * D_MODEL
EPS = 1e-6

BF16 = jnp.bfloat16
F32 = jnp.float32

LANES = 128
VMEM_LIMIT = 60 * 1024 * 1024

TM_NORM = 512
TM_IN = 1024
CAST_ROWS = 128
TN_IN = 1024
TM = 256
TM_E = 256
ROWS = D_MODEL // (2 * LANES)
PACKED = jnp.uint32
COPY_TILE = 256
DISPATCH_SLOTS = 3
TAIL_CHUNKS = 8


def _rms(x, g):
    return x * lax.rsqrt(jnp.mean(x * x, axis=-1, keepdims=True) + EPS) * g


def _sigmoid(x):
    return 1.0 / (1.0 + jnp.exp(-x))


def _store_token_major(ref, x):
    m = x.shape[0]
    for s in range(ROWS):
        lo = x[:, s * LANES:(s + 1) * LANES]
        hi = x[:, (ROWS + s) * LANES:(ROWS + s + 1) * LANES]
        ref[pl.ds(s, m, stride=ROWS), :] = pltpu.pack_elementwise([lo, hi], packed_dtype=BF16)


def _packed_zeros(shape):
    zero = jnp.zeros(shape, F32)
    return pltpu.pack_elementwise([zero, zero], packed_dtype=BF16)


def _unpack_words(words):
    half = lambda idx: [pltpu.unpack_elementwise(w, index=idx, packed_dtype=BF16, unpacked_dtype=F32)
                        for w in words]
    return jnp.concatenate(half(0) + half(1), axis=1)


def _load_token_major(ref, m):
    return _unpack_words([ref[pl.ds(s, m, stride=ROWS), :] for s in range(ROWS)])


def _norm_kernel(xp_ref, xs_ref, g_ref, o_ref, *, n_prompt_tiles):
    i = pl.program_id(0)

    def run(x_ref):
        o_ref[...] = _rms(x_ref[...], g_ref[...]).astype(BF16)

    @pl.when(i < n_prompt_tiles)
    def _():
        run(xp_ref)

    @pl.when(i >= n_prompt_tiles)
    def _():
        run(xs_ref)


def _norm(xp, xs, g):
    tp, ts = xp.shape[0], xs.shape[0]
    npt, nst = tp // TM_NORM, ts // TM_NORM
    return pl.pallas_call(
        functools.partial(_norm_kernel, n_prompt_tiles=npt),
        grid=(npt + nst,),
        in_specs=[
            pl.BlockSpec((TM_NORM, D_MODEL), lambda i: (jnp.minimum(i, npt - 1), 0)),
            pl.BlockSpec((TM_NORM, D_MODEL), lambda i: (jnp.maximum(i - npt, 0), 0)),
            pl.BlockSpec((1, D_MODEL), lambda i: (0, 0)),
        ],
        out_specs=pl.BlockSpec((TM_NORM, D_MODEL), lambda i: (i, 0)),
        out_shape=jax.ShapeDtypeStruct((tp + ts, D_MODEL), BF16),
        compiler_params=pltpu.CompilerParams(
            dimension_semantics=("arbitrary",), vmem_limit_bytes=VMEM_LIMIT),
        name="norm",
    )(xp, xs, g)


def _inproj_kernel(n_ref, w_ref, *refs, n_side):
    side_in, z_ref, side_out, wb_ref = refs[:n_side], refs[n_side], refs[n_side + 1:-1], refs[-1]

    @pl.when(pl.program_id(1) == 0)
    def _():
        wb_ref[...] = w_ref[...].astype(BF16)

    z_ref[...] = jnp.dot(n_ref[...], wb_ref[...], preferred_element_type=F32).astype(BF16)
    for src, dst in zip(side_in, side_out):
        dst[...] = src[...].astype(BF16)


def _inproj(n, w_in, side_weights):
    t = n.shape[0]
    n_j, n_i = D_IN_TOTAL // TN_IN, t // TM_IN
    starts, start = [], 0
    for w in side_weights:
        assert w.shape[1] % CAST_ROWS == 0
        starts.append(start)
        start += w.shape[1] // CAST_ROWS
    assert start <= n_j * n_i, "not enough grid steps for the weight casts"

    def side_index(k):
        n_blk = side_weights[k].shape[1] // CAST_ROWS
        return lambda j, i: jnp.clip(j * n_i + i - starts[k], 0, n_blk - 1)

    side_in = [pl.BlockSpec((None, CAST_ROWS, w.shape[2]), lambda j, i, f=side_index(k): (0, f(j, i), 0))
               for k, w in enumerate(side_weights)]
    side_out = [pl.BlockSpec((CAST_ROWS, w.shape[2]), lambda j, i, f=side_index(k): (f(j, i), 0))
                for k, w in enumerate(side_weights)]
    outs = pl.pallas_call(
        functools.partial(_inproj_kernel, n_side=len(side_weights)),
        grid=(n_j, n_i),
        in_specs=[
            pl.BlockSpec((TM_IN, D_MODEL), lambda j, i: (i, 0)),
            pl.BlockSpec((None, D_MODEL, TN_IN), lambda j, i: (0, 0, j)),
        ] + side_in,
        out_specs=[pl.BlockSpec((TM_IN, TN_IN), lambda j, i: (i, j))] + side_out,
        out_shape=[jax.ShapeDtypeStruct((t, D_IN_TOTAL), BF16)]
        + [jax.ShapeDtypeStruct(w.shape[1:], BF16) for w in side_weights],
        scratch_shapes=[pltpu.VMEM((D_MODEL, TN_IN), BF16)],
        compiler_params=pltpu.CompilerParams(
            dimension_semantics=("arbitrary", "arbitrary"), vmem_limit_bytes=VMEM_LIMIT),
        name="in_proj",
    )(n, w_in, *side_weights)
    return outs[0], outs[1:]


def _route(lg):
    shape = lg.shape
    lane = lax.broadcasted_iota(jnp.int32, shape, 1)
    lanef = lane.astype(F32)
    neg = jnp.float32(-jnp.inf)
    far = jnp.float32(LANES)
    is_g = lane < N_GROUPS
    lgm = jnp.where(is_g, lg, neg)
    gmax = jnp.max(lgm, axis=-1, keepdims=True)
    gsel = jnp.min(jnp.where(lgm == gmax, lanef, far), axis=-1, keepdims=True)
    p_group = 1.0 / jnp.sum(jnp.where(is_g, jnp.exp(lgm - gmax), 0.0), axis=-1, keepdims=True)
    lane_group = ((lane - N_GROUPS) >> 3).astype(F32)
    is_e = lane_group == gsel
    le = jnp.where(is_e, lg, neg)
    m1 = jnp.max(le, axis=-1, keepdims=True)
    i1 = jnp.min(jnp.where(le == m1, lanef, far), axis=-1, keepdims=True)
    le2 = jnp.where(lanef == i1, neg, le)
    m2 = jnp.max(le2, axis=-1, keepdims=True)
    i2 = jnp.min(jnp.where(le2 == m2, lanef, far), axis=-1, keepdims=True)
    e21 = jnp.exp(m2 - m1)
    w1 = 1.0 / (1.0 + e21)
    w2 = e21 * w1
    out = jnp.where(lane == 0, i1 - N_GROUPS,
                    jnp.where(lane == 1, i2 - N_GROUPS,
                              jnp.where(lane == 2, p_group * w1,
                                        jnp.where(lane == 3, p_group * w2, 0.0))))
    return out


def _mixer_kernel(z_ref, xp_ref, xs_ref, f_ref, lng_ref, lnb_ref, cw_ref, wsp_ref, bsp_ref,
                  wa_ref, wb_ref, wo_ref, gm_ref, wr_ref, br_ref,
                  h_ref, xm_ref, route_ref, vnp_ref, vns_ref, qp_ref, qs_ref,
                  qprev_ref, ya_ref, yb_ref, m_ref, *, n_prompt_tiles, tiles_per_seq):
    i = pl.program_id(0)
    is_prompt = i < n_prompt_tiles
    c_u, c_v, c_b, c_c, c_x = (k * D_HALF for k in range(5))
    c_ga = c_x + D_CONV
    c_gb = c_ga + D_MODEL

    v = z_ref[:, c_v:c_v + D_HALF].astype(F32)
    mu = jnp.mean(v, axis=-1, keepdims=True)
    vc = v - mu
    vn = vc * lax.rsqrt(jnp.mean(vc * vc, axis=-1, keepdims=True) + EPS) * lng_ref[...] + lnb_ref[...]
    vnb = vn.astype(BF16)
    for c in range(TM // CHUNK):
        rows = slice(c * CHUNK, (c + 1) * CHUNK)
        for g in range(G_A):
            cols = slice(g * D_GROUP_A, (g + 1) * D_GROUP_A)
            s = jnp.dot(wsp_ref[g], vnb[rows, cols], preferred_element_type=F32) + bsp_ref[:, cols]
            u = z_ref[rows, c_u + g * D_GROUP_A:c_u + (g + 1) * D_GROUP_A].astype(F32)
            ya_ref[rows, cols] = (u * s).astype(BF16)

    q = z_ref[:, c_c:c_c + D_CONV].astype(F32) * z_ref[:, c_x:c_x + D_CONV].astype(F32)
    q1 = pltpu.roll(q, 1, 0)
    q2 = pltpu.roll(q, 2, 0)
    row = lax.broadcasted_iota(jnp.int32, q.shape, 0)
    cw0, cw1, cw2 = cw_ref[0:1, :], cw_ref[1:2, :], cw_ref[2:3, :]

    seq_start = i % tiles_per_seq == 0
    p6 = jnp.where(seq_start, 0.0, qprev_ref[6:7, :])
    p7 = jnp.where(seq_start, 0.0, qprev_ref[7:8, :])
    f = f_ref[...]
    pos_in_seq = jnp.where(is_prompt, row, row & 7)
    fix1 = jnp.where(is_prompt, jnp.broadcast_to(p7, q.shape), pltpu.roll(f, TM - 1, 0))
    fix2 = jnp.where(is_prompt, jnp.where(row == 0, p6, p7), f)
    a1 = jnp.where(pos_in_seq == 0, fix1, q1)
    a2 = jnp.where(pos_in_seq < 2, fix2, q2)
    yc = cw0 * a2 + cw1 * a1 + cw2 * q
    qprev_ref[...] = q[TM - 8:TM, :]
    qp_ref[...] = q[TM - (CONV_W - 1):TM, :]
    vnp_ref[...] = vn[TM - CHUNK:TM, :]
    qs_ref[...] = q
    vns_ref[...] = vn

    yb_ref[...] = (z_ref[:, c_b:c_b + D_CONV].astype(F32) * yc).astype(BF16)

    half = D_MODEL // 2
    for hh in range(2):
        cols = slice(hh * half, (hh + 1) * half)
        a = jnp.dot(ya_ref[...], wa_ref[:, cols], preferred_element_type=F32)
        b = jnp.dot(yb_ref[...], wb_ref[:, cols], preferred_element_type=F32)
        ga = z_ref[:, c_ga + hh * half:c_ga + (hh + 1) * half].astype(F32)
        gb = z_ref[:, c_gb + hh * half:c_gb + (hh + 1) * half].astype(F32)
        m_ref[:, cols] = (_sigmoid(ga) * a + _sigmoid(gb) * b).astype(BF16)

    x = jnp.where(is_prompt, xp_ref[...], xs_ref[...])
    h = x + jnp.dot(m_ref[...], wo_ref[...], preferred_element_type=F32)
    h_ref[...] = h
    xm = _rms(h, gm_ref[...])
    _store_token_major(xm_ref, xm)
    lg =jnp.dot(xm.astype(BF16), wr_ref[...], preferred_element_type=F32) + br_ref[...]
    route_ref[...] = _route(lg)


def _mixer(z, xp, xs, f, lng, lnb, cw, wsp, bsp, wa, wb, wo, gm, wr, br, *, n_batch):
    tp, ts = xp.shape[0], xs.shape[0]
    t = tp + ts
    npt, nst = tp // TM, ts // TM
    tiles_per_seq = tp // n_batch // TM
    once = pl.Buffered(1)

    def pidx(i):
        return jnp.minimum(i, npt - 1)

    def sidx(i):
        return jnp.maximum(i - npt, 0)

    def seq_or_spare(i):
        return jnp.where(i < npt, i // tiles_per_seq, n_batch)

    const = lambda shape: pl.BlockSpec(shape, lambda i: (0,) * len(shape), pipeline_mode=once)
    in_specs = [
        pl.BlockSpec((TM, D_IN_TOTAL), lambda i: (i, 0)),
        pl.BlockSpec((TM, D_MODEL), lambda i: (pidx(i), 0)),
        pl.BlockSpec((TM, D_MODEL), lambda i: (sidx(i), 0)),
        pl.BlockSpec((TM, D_CONV), lambda i: (sidx(i), 0)),
        const((1, D_HALF)), const((1, D_HALF)), const((CONV_W, D_CONV)),
        pl.BlockSpec((None, G_A, CHUNK, CHUNK), lambda i: (jnp.where(i < npt, 0, 1), 0, 0, 0)),
        pl.BlockSpec((None, CHUNK, D_HALF), lambda i: (jnp.where(i < npt, 0, 1), 0, 0)),
        const((D_HALF, D_MODEL)), const((D_CONV, D_MODEL)), const((D_MODEL, D_MODEL)),
        const((1, D_MODEL)), const((D_MODEL, LANES)), const((1, LANES)),
    ]
    out_specs = [
        pl.BlockSpec((TM, D_MODEL), lambda i: (i, 0)),
        pl.BlockSpec((TM * ROWS, LANES), lambda i: (i, 0)),
        pl.BlockSpec((TM, LANES), lambda i: (i, 0)),
        pl.BlockSpec((None, CHUNK, D_HALF), lambda i: (seq_or_spare(i), 0, 0)),
        pl.BlockSpec((TM, D_HALF), lambda i: (sidx(i), 0)),
        pl.BlockSpec((None, CONV_W - 1, D_CONV), lambda i: (seq_or_spare(i), 0, 0)),
        pl.BlockSpec((TM, D_CONV), lambda i: (sidx(i), 0)),
    ]
    out_shape = [
        jax.ShapeDtypeStruct((t, D_MODEL), F32),
        jax.ShapeDtypeStruct((t * ROWS, LANES), PACKED),
        jax.ShapeDtypeStruct((t, LANES), F32),
        jax.ShapeDtypeStruct((n_batch + 1, CHUNK, D_HALF), F32),
        jax.ShapeDtypeStruct((ts, D_HALF), F32),
        jax.ShapeDtypeStruct((n_batch + 1, CONV_W - 1, D_CONV), F32),
        jax.ShapeDtypeStruct((ts, D_CONV), F32),
    ]
    return pl.pallas_call(
        functools.partial(_mixer_kernel, n_prompt_tiles=npt, tiles_per_seq=tiles_per_seq),
        grid=(npt + nst,),
        in_specs=in_specs,
        out_specs=out_specs,
        out_shape=out_shape,
        scratch_shapes=[
            pltpu.VMEM((8, D_CONV), F32),
            pltpu.VMEM((TM, D_HALF), BF16),
            pltpu.VMEM((TM, D_CONV), BF16),
            pltpu.VMEM((TM, D_MODEL), BF16),
        ],
        compiler_params=pltpu.CompilerParams(
            dimension_semantics=("arbitrary",), vmem_limit_bytes=VMEM_LIMIT),
        name="mixer",
    )(z, xp, xs, f, lng, lnb, cw, wsp, bsp, wa, wb, wo, gm, wr, br)


def _dispatch_kernel(pos_ref, cnt_ref, off_ref, nused_ref, xm_hbm, xs_hbm, inbuf, zbuf, isem, rsem, zsem,
                     *, n_steps, n_tiles):
    zbuf[...] = _packed_zeros(zbuf.shape)
    nused = nused_ref[0]
    pad_bits = TM_E.bit_length() - 1

    def pad_copies(e, fn):
        cnt = cnt_ref[e]
        npad = (TM_E - cnt % TM_E) % TM_E
        start = off_ref[e] + cnt
        for b in range(pad_bits):
            size = 1 << b
            at = start + ((npad >> (b + 1)) << (b + 1))

            @pl.when(((npad >> b) & 1) == 1)
            def _():
                fn(pltpu.make_async_copy(zbuf.at[pl.ds(0, size * ROWS), :],
                                         xs_hbm.at[pl.ds(at * ROWS, size * ROWS), :], zsem))

    def tile_copy(tile):
        return pltpu.make_async_copy(zbuf, xs_hbm.at[pl.ds(tile * TM_E * ROWS, TM_E * ROWS), :], zsem)

    def loop(lo, hi, body):
        lax.fori_loop(lo, hi, lambda k, c: (body(k), c)[1], 0)

    loop(0, N_EXPERTS, lambda e: pad_copies(e, lambda cp: cp.start()))
    loop(nused, n_tiles, lambda tile: tile_copy(tile).start())

    def in_copy(step, slot):
        return pltpu.make_async_copy(xm_hbm.at[pl.ds(step * COPY_TILE * ROWS, COPY_TILE * ROWS), :],
                                     inbuf.at[slot], isem.at[slot])

    def rows_wait(slot):
        for _ in range(2):
            pltpu.make_async_copy(inbuf.at[slot], xs_hbm.at[pl.ds(0, COPY_TILE * ROWS), :], rsem.at[slot]).wait()

    in_copy(0, 0).start()

    def step(s):
        slot = s % DISPATCH_SLOTS
        nxt = (s + 1) % DISPATCH_SLOTS

        @pl.when(s >= DISPATCH_SLOTS - 1)
        def _():
            rows_wait(nxt)

        @pl.when(s + 1 < n_steps)
        def _():
            in_copy(s + 1, nxt).start()

        in_copy(s, slot).wait()
        base = s * COPY_TILE * 2

        def body(r, carry):
            for k in range(2):
                p = pos_ref[base + 2 * r + k]
                pltpu.make_async_copy(inbuf.at[slot, pl.ds(r * ROWS, ROWS), :],
                                      xs_hbm.at[pl.ds(p * ROWS, ROWS), :], rsem.at[slot]).start(priority=k)
            return carry

        lax.fori_loop(0, COPY_TILE, body, 0, unroll=4)

    loop(0, n_steps, step)
    for back in range(DISPATCH_SLOTS - 1, 0, -1):
        rows_wait((n_steps - back) % DISPATCH_SLOTS)
    loop(0, N_EXPERTS, lambda e: pad_copies(e, lambda cp: cp.wait()))
    loop(nused, n_tiles, lambda tile: tile_copy(tile).wait())


def _dispatch(pos, cnt, off, nused, xm, *, n_tiles):
    n_tokens = xm.shape[0] // ROWS
    assert n_tokens % COPY_TILE == 0 and n_tokens // COPY_TILE >= DISPATCH_SLOTS
    smem = pl.BlockSpec(memory_space=pltpu.SMEM)
    return pl.pallas_call(
        functools.partial(_dispatch_kernel, n_steps=n_tokens // COPY_TILE, n_tiles=n_tiles),
        in_specs=[smem, smem, smem, smem, pl.BlockSpec(memory_space=pl.ANY)],
        out_specs=pl.BlockSpec(memory_space=pl.ANY),
        out_shape=jax.ShapeDtypeStruct((n_tiles * TM_E * ROWS, LANES), PACKED),
        scratch_shapes=[
            pltpu.VMEM((DISPATCH_SLOTS, COPY_TILE * ROWS, LANES), PACKED),
            pltpu.VMEM((TM_E * ROWS, LANES), PACKED),
            pltpu.SemaphoreType.DMA((DISPATCH_SLOTS,)),
            pltpu.SemaphoreType.DMA((DISPATCH_SLOTS,)),
            pltpu.SemaphoreType.DMA(()),
        ],
        compiler_params=pltpu.CompilerParams(vmem_limit_bytes=VMEM_LIMIT),
        name="dispatch",
    )(pos, cnt, off, nused, xm)


def _expert_kernel(te_ref, nused_ref, xs_ref, wg_hbm, wu_hbm, wd_hbm, y_ref,
                   wgf, wuf, wdf, wgb, wub, wdb, wsem, seen_ref):
    i = pl.program_id(0)
    nused = nused_ref[0]

    def weight_copies(e, slot):
        return [pltpu.make_async_copy(src.at[0, e], dst.at[slot], wsem.at[slot, k])
                for k, (src, dst) in enumerate(((wg_hbm, wgf), (wu_hbm, wuf), (wd_hbm, wdf)))]

    @pl.when(i == 0)
    def _():
        seen_ref[0] = 0
        for cp in weight_copies(te_ref[0], 0):
            cp.start()

    @pl.when(i < nused)
    def _():
        e = te_ref[i]
        new_expert = jnp.logical_or(i == 0, e != te_ref[jnp.maximum(i - 1, 0)])

        @pl.when(new_expert)
        def _():
            slot = seen_ref[0] % 2
            seen_ref[0] = seen_ref[0] + 1
            nxt = lax.while_loop(lambda j: jnp.logical_and(j < nused, te_ref[jnp.minimum(j, nused - 1)] == e),
                                 lambda j: j + 1, i + 1)

            @pl.when(nxt < nused)
            def _():
                for cp in weight_copies(te_ref[jnp.minimum(nxt, nused - 1)], 1 - slot):
                    cp.start(priority=1)

            for cp in weight_copies(e, slot):
                cp.wait()
            wgb[...] = wgf[slot].astype(BF16)
            wub[...] = wuf[slot].astype(BF16)
            wdb[...] = wdf[slot].astype(BF16)

        x = _load_token_major(xs_ref, TM_E).astype(BF16)
        g = jnp.dot(x, wgb[...], preferred_element_type=F32)
        u = jnp.dot(x, wub[...], preferred_element_type=F32)
        hid = (g * _sigmoid(g)) * u
        _store_token_major(y_ref, jnp.dot(hid.astype(BF16), wdb[...], preferred_element_type=F32))

    @pl.when(i >= nused)
    def _():
        y_ref[...] = _packed_zeros(y_ref.shape)


def _experts(te, nused, xs, wg, wu, wd, *, n_tiles):
    grid_spec = pltpu.PrefetchScalarGridSpec(
        num_scalar_prefetch=2,
        grid=(n_tiles,),
        in_specs=[
            pl.BlockSpec((TM_E * ROWS, LANES), lambda i, te, nu: (jnp.minimum(i, nu[0] - 1), 0)),
            pl.BlockSpec(memory_space=pl.ANY),
            pl.BlockSpec(memory_space=pl.ANY),
            pl.BlockSpec(memory_space=pl.ANY),
        ],
        out_specs=pl.BlockSpec((TM_E * ROWS, LANES), lambda i, te, nu: (i, 0)),
        scratch_shapes=[
            pltpu.VMEM((2, D_MODEL, D_EXPERT), F32),
            pltpu.VMEM((2, D_MODEL, D_EXPERT), F32),
            pltpu.VMEM((2, D_EXPERT, D_MODEL), F32),
            pltpu.VMEM((D_MODEL, D_EXPERT), BF16),
            pltpu.VMEM((D_MODEL, D_EXPERT), BF16),
            pltpu.VMEM((D_EXPERT, D_MODEL), BF16),
            pltpu.SemaphoreType.DMA((2, 3)),
            pltpu.SMEM((1,), jnp.int32),
        ],
    )
    return pl.pallas_call(
        _expert_kernel,
        grid_spec=grid_spec,
        out_shape=jax.ShapeDtypeStruct((n_tiles * TM_E * ROWS, LANES), PACKED),
        compiler_params=pltpu.CompilerParams(
            dimension_semantics=("arbitrary",), vmem_limit_bytes=VMEM_LIMIT),
        name="experts",
    )(te, nused, xs, wg, wu, wd)


def _tail_kernel(pos_ref, ys_hbm, h_ref, route_ref, pp_ref, ps_ref, gp_ref, wpg_ref, wpp_ref, gf_ref,
                 yp_ref, ysm_ref, ybuf, sem, h2_ref, a_ref, proj_ref, *, n_prompt_tiles, n_tiles):
    i = pl.program_id(0)
    slot = i % 2
    is_prompt = i < n_prompt_tiles

    def issue(tile, dst_slot):
        base = tile * TM * 2

        def body(r, carry):
            for k in range(2):
                p = pos_ref[base + 2 * r + k]
                pltpu.make_async_copy(ys_hbm.at[pl.ds(p * ROWS, ROWS), :],
                                      ybuf.at[dst_slot, k, pl.ds(r * ROWS, ROWS), :],
                                      sem.at[dst_slot]).start()
            return carry

        lax.fori_loop(0, TM, body, 0, unroll=4)

    def slot_wait(s):
        for k in range(2):
            pltpu.make_async_copy(ys_hbm.at[pl.ds(0, TM * ROWS), :], ybuf.at[s, k], sem.at[s]).wait()

    def row_sumsq(acc):
        return jnp.sum(acc, axis=-1, keepdims=True) * (1.0 / D_MODEL)

    @pl.when(i == 0)
    def _():
        issue(0, 0)

    slot_wait(slot)

    w0 = jnp.broadcast_to(route_ref[:, 2:3], (TM, LANES))
    w1 = jnp.broadcast_to(route_ref[:, 3:4], (TM, LANES))
    acc = jnp.zeros((TM, LANES), F32)
    for s in range(ROWS):
        word0 = ybuf[slot, 0, pl.ds(s, TM, stride=ROWS), :]
        word1 = ybuf[slot, 1, pl.ds(s, TM, stride=ROWS), :]
        for idx in range(2):
            cols = slice((idx * ROWS + s) * LANES, (idx * ROWS + s + 1) * LANES)
            y0 = pltpu.unpack_elementwise(word0, index=idx, packed_dtype=BF16, unpacked_dtype=F32)
            y1 = pltpu.unpack_elementwise(word1, index=idx, packed_dtype=BF16, unpacked_dtype=F32)
            h2 = h_ref[:, cols] + (w0 * y0 + w1 * y1)
            h2_ref[:, cols] = h2
            acc = acc + h2 * h2
    r = lax.rsqrt(row_sumsq(acc) + EPS)
    a_ref[...] = (h2_ref[...] * r * gp_ref[...]).astype(BF16)
    p = jnp.where(is_prompt, pp_ref[...], ps_ref[...]).astype(BF16)
    proj_ref[...] = jnp.dot(p, wpp_ref[...], preferred_element_type=F32)

    nxt_base = jnp.minimum(i + 1, n_tiles - 1) * TM * 2
    rows_per_chunk = TM // TAIL_CHUNKS
    width = D_MODEL // TAIL_CHUNKS
    acc = jnp.zeros((TM, LANES), F32)
    for c in range(TAIL_CHUNKS):
        for rr in range(rows_per_chunk):
            row = c * rows_per_chunk + rr
            for k in range(2):
                p_row = pos_ref[nxt_base + 2 * row + k]
                pltpu.make_async_copy(ys_hbm.at[pl.ds(p_row * ROWS, ROWS), :],
                                      ybuf.at[1 - slot, k, pl.ds(row * ROWS, ROWS), :],
                                      sem.at[1 - slot]).start()
        cols = slice(c * width, (c + 1) * width)
        gate = _sigmoid(jnp.dot(a_ref[...], wpg_ref[:, cols], preferred_element_type=F32))
        h3 = h2_ref[:, cols] + gate * proj_ref[:, cols]
        h2_ref[:, cols] = h3
        for j in range(width // LANES):
            part = h3[:, j * LANES:(j + 1) * LANES]
            acc = acc + part * part
    scale = lax.rsqrt(row_sumsq(acc) + EPS)

    @pl.when(is_prompt)
    def _():
        yp_ref[...] = h2_ref[...] * scale * gf_ref[...]

    @pl.when(jnp.logical_not(is_prompt))
    def _():
        ysm_ref[...] = h2_ref[...] * scale * gf_ref[...]

    @pl.when(i == n_tiles - 1)
    def _():
        slot_wait(1 - slot)


def _tail(pos, ys, h, route, pp, ps, gp, wpg, wpp, gf):
    tp, ts = pp.shape[0], ps.shape[0]
    npt, nst = tp // TM, ts // TM
    ple = pp.shape[1]
    once = pl.Buffered(1)

    def pidx(i):
        return jnp.minimum(i, npt - 1)

    def sidx(i):
        return jnp.maximum(i - npt, 0)

    grid_spec = pltpu.PrefetchScalarGridSpec(
        num_scalar_prefetch=1,
        grid=(npt + nst,),
        in_specs=[
            pl.BlockSpec(memory_space=pl.ANY),
            pl.BlockSpec((TM, D_MODEL), lambda i, pos: (i, 0)),
            pl.BlockSpec((TM, LANES), lambda i, pos: (i, 0)),
            pl.BlockSpec((TM, ple), lambda i, pos: (pidx(i), 0)),
            pl.BlockSpec((TM, ple), lambda i, pos: (sidx(i), 0)),
            pl.BlockSpec((1, D_MODEL), lambda i, pos: (0, 0), pipeline_mode=once),
            pl.BlockSpec((D_MODEL, D_MODEL), lambda i, pos: (0, 0), pipeline_mode=once),
            pl.BlockSpec((ple, D_MODEL), lambda i, pos: (0, 0), pipeline_mode=once),
            pl.BlockSpec((1, D_MODEL), lambda i, pos: (0, 0), pipeline_mode=once),
        ],
        out_specs=[
            pl.BlockSpec((TM, D_MODEL), lambda i, pos: (pidx(i), 0)),
            pl.BlockSpec((TM, D_MODEL), lambda i, pos: (sidx(i), 0)),
        ],
        scratch_shapes=[
            pltpu.VMEM((2, 2, TM * ROWS, LANES), PACKED),
            pltpu.SemaphoreType.DMA((2,)),
            pltpu.VMEM((TM, D_MODEL), F32),
            pltpu.VMEM((TM, D_MODEL), BF16),
            pltpu.VMEM((TM, D_MODEL), F32),
        ],
    )
    return pl.pallas_call(
        functools.partial(_tail_kernel, n_prompt_tiles=npt, n_tiles=npt + nst),
        grid_spec=grid_spec,
        out_shape=[jax.ShapeDtypeStruct((tp, D_MODEL), F32), jax.ShapeDtypeStruct((ts, D_MODEL), F32)],
        compiler_params=pltpu.CompilerParams(
            dimension_semantics=("arbitrary",), vmem_limit_bytes=VMEM_LIMIT),
        name="tail",
    )(pos, ys, h, route, pp, ps, gp, wpg, wpp, gf)


def _route_tables(route, n_tiles):
    e = route[:, 0:2].astype(jnp.int32).reshape(-1)
    onehot = (e[:, None] == jnp.arange(N_EXPERTS, dtype=jnp.int32)[None, :]).astype(jnp.int32)
    cum = jnp.cumsum(onehot, axis=0)
    counts = cum[-1]
    rank = jnp.sum((cum - onehot) * onehot, axis=1)
    padded = ((counts + TM_E - 1) // TM_E) * TM_E
    ends = jnp.cumsum(padded)
    offs = ends - padded
    pos = jnp.sum(onehot * offs[None, :], axis=1) + rank
    nused = ends[-1] // TM_E
    tile_id = jnp.arange(n_tiles, dtype=jnp.int32)
    te = jnp.sum((tile_id[:, None] * TM_E >= ends[None, :]).astype(jnp.int32), axis=1)
    te_last = jnp.sum((((nused - 1) * TM_E) >= ends).astype(jnp.int32))
    te = jnp.where(tile_id < nused, te, te_last)
    i32 = lambda a: a.astype(jnp.int32)
    return i32(te), i32(nused).reshape(1), i32(pos), i32(counts), i32(offs)


def kernel(x_prompt, x_sample, state_conv, p_prompt, p_sample, g_mix, w_in, ln_v_g, ln_v_b, w_spatial, b_spatial, conv_w, w_br_a, w_br_b, w_out, g_moe, w_router_group, b_router_group, w_router_expert, b_router_expert, w_exp_gate, w_exp_up, w_exp_down, g_ple, w_ple_gate, w_ple_proj, g_final):
    depth = w_in.shape[0]
    assert depth == 1, "single-layer step only"
    nb, seq, _ = x_prompt.shape
    ns, dseq, _ = x_sample.shape
    assert dseq == 8 and seq % TM == 0 and (ns * dseq) % TM_NORM == 0 and (nb * seq) % TM_NORM == 0
    assert (nb * seq + ns * dseq) % TM_IN == 0
    tp, ts = nb * seq, ns * dseq
    t = tp + ts
    xp = x_prompt.reshape(tp, D_MODEL)
    xs = x_sample.reshape(ts, D_MODEL)

    row = lambda a: a.reshape(1, -1)
    tril = jnp.tril(jnp.ones((CHUNK, CHUNK), dtype=bool))
    w_sp = jnp.where(tril[None], w_spatial[0], 0.0)
    eye = jnp.eye(CHUNK // dseq, dtype=F32)
    w_sp_s = jnp.einsum("ij,gts->gitjs", eye, w_sp[:, :dseq, :dseq]).reshape(G_A, CHUNK, CHUNK)
    wsp = jnp.stack([w_sp, w_sp_s]).astype(BF16)
    b_p = jnp.repeat(b_spatial[0].T, D_GROUP_A, axis=1)
    b_s = jnp.tile(b_p[:dseq], (CHUNK // dseq, 1))
    bsp = jnp.stack([b_p, b_s])
    wr = jnp.zeros((D_MODEL, LANES), F32)
    wr = wr.at[:, :N_GROUPS].set(w_router_group[0]).at[:, N_GROUPS:N_GROUPS + N_EXPERTS].set(w_router_expert[0])
    br = jnp.zeros((1, LANES), F32)
    br = br.at[0, :N_GROUPS].set(b_router_group[0]).at[0, N_GROUPS:N_GROUPS + N_EXPERTS].set(b_router_expert[0])
    f = jnp.pad(state_conv[0], ((0, 0), (0, dseq - (CONV_W - 1)), (0, 0))).reshape(ts, D_CONV)

    n = _norm(xp, xs, row(g_mix[0]))
    z, (wa, wb, wo, wpg) = _inproj(n, w_in, [w_br_a, w_br_b, w_out, w_ple_gate])
    h, xm, route, vnp, vns, qp, qs = _mixer(
        z, xp, xs, f, row(ln_v_g[0]), row(ln_v_b[0]), conv_w[0], wsp, bsp, wa, wb, wo,
        row(g_moe[0]), wr.astype(BF16), br, n_batch=nb)

    n_tiles = (2 * t) // TM_E + N_EXPERTS
    te, nused, pos, counts, offs = _route_tables(route, n_tiles)
    xsort = _dispatch(pos, counts, offs, nused, xm, n_tiles=n_tiles)
    ys = _experts(te, nused, xsort, w_exp_gate, w_exp_up, w_exp_down, n_tiles=n_tiles)

    yp, ysm = _tail(pos, ys, h, route, p_prompt[0].reshape(tp, -1), p_sample[0].reshape(ts, -1),
                    row(g_ple[0]), wpg, w_ple_proj[0].astype(BF16), row(g_final))

    y_prompt = yp.reshape(nb, seq, D_MODEL)
    y_sample = ysm.reshape(ns, dseq, D_MODEL)
    conv_state_prompt = qp[:nb][None]
    conv_state_sample = qs.reshape(ns, dseq, D_CONV)[:, dseq - (CONV_W - 1):][None]
    v_rows_prompt = vnp[:nb][None]
    v_rows_sample = vns.reshape(ns, dseq, D_HALF)[None]
    return (y_prompt, y_sample, conv_state_prompt, conv_state_sample, v_rows_prompt, v_rows_sample)
```

```python
import functools

import jax
import jax.numpy as jnp
from jax import lax
from jax.experimental import pallas as pl
from jax.experimental.pallas import tpu as pltpu

D_MODEL = 2048
D_HALF = D_MODEL // 2
D_CONV = D_MODEL // 2
G_A = 4
D_GROUP_A = D_HALF // G_A
CHUNK = 128
CONV_W = 3
N_GROUPS = 4
EXPERTS_PER_GROUP = 8
N_EXPERTS = N_GROUPS * EXPERTS_PER_GROUP
D_EXPERT = D_MODEL // 4
D_IN_TOTAL = 2 * D_HALF + 3 * D_CONV + 2 * D_MODEL
EPS = 1e-6

BF16 = jnp.bfloat16
F32 = jnp.float32

LANES = 128
VMEM_LIMIT = 60 * 1024 * 1024

TM_NORM = 512
TM_IN = 1024
CAST_ROWS = 128
TN_IN = 1024
TM = 256
TM_E = 256
ROWS = D_MODEL // (2 * LANES)
PACKED = jnp.uint32
COPY_TILE = 256
DISPATCH_SLOTS = 3
TAIL_CHUNKS = 8


def _rms(x, g):
    return x * lax.rsqrt(jnp.mean(x * x, axis=-1, keepdims=True) + EPS) * g


def _sigmoid(x):
    return 1.0 / (1.0 + jnp.exp(-x))


def _store_token_major(ref, x):
    m = x.shape[0]
    for s in range(ROWS):
        lo = x[:, s * LANES:(s + 1) * LANES]
        hi = x[:, (ROWS + s) * LANES:(ROWS + s + 1) * LANES]
        ref[pl.ds(s, m, stride=ROWS), :] = pltpu.pack_elementwise([lo, hi], packed_dtype=BF16)


def _packed_zeros(shape):
    zero = jnp.zeros(shape, F32)
    return pltpu.pack_elementwise([zero, zero], packed_dtype=BF16)


def _unpack_words(words):
    half = lambda idx: [pltpu.unpack_elementwise(w, index=idx, packed_dtype=BF16, unpacked_dtype=F32)
                        for w in words]
    return jnp.concatenate(half(0) + half(1), axis=1)


def _load_token_major(ref, m):
    return _unpack_words([ref[pl.ds(s, m, stride=ROWS), :] for s in range(ROWS)])


def _norm_kernel(xp_ref, xs_ref, g_ref, o_ref, *, n_prompt_tiles):
    i = pl.program_id(0)

    def run(x_ref):
        o_ref[...] = _rms(x_ref[...], g_ref[...]).astype(BF16)

    @pl.when(i < n_prompt_tiles)
    def _():
        run(xp_ref)

    @pl.when(i >= n_prompt_tiles)
    def _():
        run(xs_ref)


def _norm(xp, xs, g):
    tp, ts = xp.shape[0], xs.shape[0]
    npt, nst = tp // TM_NORM, ts // TM_NORM
    return pl.pallas_call(
        functools.partial(_norm_kernel, n_prompt_tiles=npt),
        grid=(npt + nst,),
        in_specs=[
            pl.BlockSpec((TM_NORM, D_MODEL), lambda i: (jnp.minimum(i, npt - 1), 0)),
            pl.BlockSpec((TM_NORM, D_MODEL), lambda i: (jnp.maximum(i - npt, 0), 0)),
            pl.BlockSpec((1, D_MODEL), lambda i: (0, 0)),
        ],
        out_specs=pl.BlockSpec((TM_NORM, D_MODEL), lambda i: (i, 0)),
        out_shape=jax.ShapeDtypeStruct((tp + ts, D_MODEL), BF16),
        compiler_params=pltpu.CompilerParams(
            dimension_semantics=("arbitrary",), vmem_limit_bytes=VMEM_LIMIT),
        name="norm",
    )(xp, xs, g)


def _inproj_kernel(n_ref, w_ref, *refs, n_side):
    side_in, z_ref, side_out, wb_ref = refs[:n_side], refs[n_side], refs[n_side + 1:-1], refs[-1]

    @pl.when(pl.program_id(1) == 0)
    def _():
        wb_ref[...] = w_ref[...].astype(BF16)

    z_ref[...] = jnp.dot(n_ref[...], wb_ref[...], preferred_element_type=F32).astype(BF16)
    for src, dst in zip(side_in, side_out):
        dst[...] = src[...].astype(BF16)


def _inproj(n, w_in, side_weights):
    t = n.shape[0]
    n_j, n_i = D_IN_TOTAL // TN_IN, t // TM_IN
    starts, start = [], 0
    for w in side_weights:
        assert w.shape[1] % CAST_ROWS == 0
        starts.append(start)
        start += w.shape[1] // CAST_ROWS
    assert start <= n_j * n_i, "not enough grid steps for the weight casts"

    def side_index(k):
        n_blk = side_weights[k].shape[1] // CAST_ROWS
        return lambda j, i: jnp.clip(j * n_i + i - starts[k], 0, n_blk - 1)

    side_in = [pl.BlockSpec((None, CAST_ROWS, w.shape[2]), lambda j, i, f=side_index(k): (0, f(j, i), 0))
               for k, w in enumerate(side_weights)]
    side_out = [pl.BlockSpec((CAST_ROWS, w.shape[2]), lambda j, i, f=side_index(k): (f(j, i), 0))
                for k, w in enumerate(side_weights)]
    outs = pl.pallas_call(
        functools.partial(_inproj_kernel, n_side=len(side_weights)),
        grid=(n_j, n_i),
        in_specs=[
            pl.BlockSpec((TM_IN, D_MODEL), lambda j, i: (i, 0)),
            pl.BlockSpec((None, D_MODEL, TN_IN), lambda j, i: (0, 0, j)),
        ] + side_in,
        out_specs=[pl.BlockSpec((TM_IN, TN_IN), lambda j, i: (i, j))] + side_out,
        out_shape=[jax.ShapeDtypeStruct((t, D_IN_TOTAL), BF16)]
        + [jax.ShapeDtypeStruct(w.shape[1:], BF16) for w in side_weights],
        scratch_shapes=[pltpu.VMEM((D_MODEL, TN_IN), BF16)],
        compiler_params=pltpu.CompilerParams(
            dimension_semantics=("arbitrary", "arbitrary"), vmem_limit_bytes=VMEM_LIMIT),
        name="in_proj",
    )(n, w_in, *side_weights)
    return outs[0], outs[1:]


def _route(lg):
    shape = lg.shape
    lane = lax.broadcasted_iota(jnp.int32, shape, 1)
    lanef = lane.astype(F32)
    neg = jnp.float32(-jnp.inf)
    far = jnp.float32(LANES)
    is_g = lane < N_GROUPS
    lgm = jnp.where(is_g, lg, neg)
    gmax = jnp.max(lgm, axis=-1, keepdims=True)
    gsel = jnp.min(jnp.where(lgm == gmax, lanef, far), axis=-1, keepdims=True)
    p_group = 1.0 / jnp.sum(jnp.where(is_g, jnp.exp(lgm - gmax), 0.0), axis=-1, keepdims=True)
    lane_group = ((lane - N_GROUPS) >> 3).astype(F32)
    is_e = lane_group == gsel
    le = jnp.where(is_e, lg, neg)
    m1 = jnp.max(le, axis=-1, keepdims=True)
    i1 = jnp.min(jnp.where(le == m1, lanef, far), axis=-1, keepdims=True)
    le2 = jnp.where(lanef == i1, neg, le)
    m2 = jnp.max(le2, axis=-1, keepdims=True)
    i2 = jnp.min(jnp.where(le2 == m2, lanef, far), axis=-1, keepdims=True)
    e21 = jnp.exp(m2 - m1)
    w1 = 1.0 / (1.0 + e21)
    w2 = e21 * w1
    out = jnp.where(lane == 0, i1 - N_GROUPS,
                    jnp.where(lane == 1, i2 - N_GROUPS,
                              jnp.where(lane == 2, p_group * w1,
                                        jnp.where(lane == 3, p_group * w2, 0.0))))
    return out


def _mixer_kernel(z_ref, xp_ref, xs_ref, f_ref, lng_ref, lnb_ref, cw_ref, wsp_ref, bsp_ref,
                  wa_ref, wb_ref, wo_ref, gm_ref, wr_ref, br_ref,
                  h_ref, xm_ref, route_ref, vnp_ref, vns_ref, qp_ref, qs_ref,
                  qprev_ref, ya_ref, yb_ref, m_ref, *, n_prompt_tiles, tiles_per_seq):
    i = pl.program_id(0)
    is_prompt = i < n_prompt_tiles
    c_u, c_v, c_b, c_c, c_x = (k * D_HALF for k in range(5))
    c_ga = c_x + D_CONV
    c_gb = c_ga + D_MODEL

    v = z_ref[:, c_v:c_v + D_HALF].astype(F32)
    mu = jnp.mean(v, axis=-1, keepdims=True)
    vc = v - mu
    vn = vc * lax.rsqrt(jnp.mean(vc * vc, axis=-1, keepdims=True) + EPS) * lng_ref[...] + lnb_ref[...]
    vnb = vn.astype(BF16)
    for c in range(TM // CHUNK):
        rows = slice(c * CHUNK, (c + 1) * CHUNK)
        for g in range(G_A):
            cols = slice(g * D_GROUP_A, (g + 1) * D_GROUP_A)
            s = jnp.dot(wsp_ref[g], vnb[rows, cols], preferred_element_type=F32) + bsp_ref[:, cols]
            u = z_ref[rows, c_u + g * D_GROUP_A:c_u + (g + 1) * D_GROUP_A].astype(F32)
            ya_ref[rows, cols] = (u * s).astype(BF16)

    q = z_ref[:, c_c:c_c + D_CONV].astype(F32) * z_ref[:, c_x:c_x + D_CONV].astype(F32)
    q1 = pltpu.roll(q, 1, 0)
    q2 = pltpu.roll(q, 2, 0)
    row = lax.broadcasted_iota(jnp.int32, q.shape, 0)
    cw0, cw1, cw2 = cw_ref[0:1, :], cw_ref[1:2, :], cw_ref[2:3, :]

    seq_start = i % tiles_per_seq == 0
    p6 = jnp.where(seq_start, 0.0, qprev_ref[6:7, :])
    p7 = jnp.where(seq_start, 0.0, qprev_ref[7:8, :])
    f = f_ref[...]
    pos_in_seq = jnp.where(is_prompt, row, row & 7)
    fix1 = jnp.where(is_prompt, jnp.broadcast_to(p7, q.shape), pltpu.roll(f, TM - 1, 0))
    fix2 = jnp.where(is_prompt, jnp.where(row == 0, p6, p7), f)
    a1 = jnp.where(pos_in_seq == 0, fix1, q1)
    a2 = jnp.where(pos_in_seq < 2, fix2, q2)
    yc = cw0 * a2 + cw1 * a1 + cw2 * q
    qprev_ref[...] = q[TM - 8:TM, :]
    qp_ref[...] = q[TM - (CONV_W - 1):TM, :]
    vnp_ref[...] = vn[TM - CHUNK:TM, :]
    qs_ref[...] = q
    vns_ref[...] = vn

    yb_ref[...] = (z_ref[:, c_b:c_b + D_CONV].astype(F32) * yc).astype(BF16)

    half = D_MODEL // 2
    for hh in range(2):
        cols = slice(hh * half, (hh + 1) * half)
        a = jnp.dot(ya_ref[...], wa_ref[:, cols], preferred_element_type=F32)
        b = jnp.dot(yb_ref[...], wb_ref[:, cols], preferred_element_type=F32)
        ga = z_ref[:, c_ga + hh * half:c_ga + (hh + 1) * half].astype(F32)
        gb = z_ref[:, c_gb + hh * half:c_gb + (hh + 1) * half].astype(F32)
        m_ref[:, cols] = (_sigmoid(ga) * a + _sigmoid(gb) * b).astype(BF16)

    x = jnp.where(is_prompt, xp_ref[...], xs_ref[...])
    h = x + jnp.dot(m_ref[...], wo_ref[...], preferred_element_type=F32)
    h_ref[...] = h
    xm = _rms(h, gm_ref[...])
    _store_token_major(xm_ref, xm)
    lg = jnp.dot(xm.astype(BF16), wr_ref[...], preferred_element_type=F32) + br_ref[...]
    route_ref[...] = _route(lg)


def _mixer(z, xp, xs, f, lng, lnb, cw, wsp, bsp, wa, wb, wo, gm, wr, br, *, n_batch):
    tp, ts = xp.shape[0], xs.shape[0]
    t = tp + ts
    npt, nst = tp // TM, ts // TM
    tiles_per_seq = tp // n_batch // TM
    once = pl.Buffered(1)

    def pidx(i):
        return jnp.minimum(i, npt - 1)

    def sidx(i):
        return jnp.maximum(i - npt, 0)

    def seq_or_spare(i):
        return jnp.where(i < npt, i // tiles_per_seq, n_batch)

    const = lambda shape: pl.BlockSpec(shape, lambda i: (0,) * len(shape), pipeline_mode=once)
    in_specs = [
        pl.BlockSpec((TM, D_IN_TOTAL), lambda i: (i, 0)),
        pl.BlockSpec((TM, D_MODEL), lambda i: (pidx(i), 0)),
        pl.BlockSpec((TM, D_MODEL), lambda i: (sidx(i), 0)),
        pl.BlockSpec((TM, D_CONV), lambda i: (sidx(i), 0)),
        const((1, D_HALF)), const((1, D_HALF)), const((CONV_W, D_CONV)),
        pl.BlockSpec((None, G_A, CHUNK, CHUNK), lambda i: (jnp.where(i < npt, 0, 1), 0, 0, 0)),
        pl.BlockSpec((None, CHUNK, D_HALF), lambda i: (jnp.where(i < npt, 0, 1), 0, 0)),
        const((D_HALF, D_MODEL)), const((D_CONV, D_MODEL)), const((D_MODEL, D_MODEL)),
        const((1, D_MODEL)), const((D_MODEL, LANES)), const((1, LANES)),
    ]
    out_specs = [
        pl.BlockSpec((TM, D_MODEL), lambda i: (i, 0)),
        pl.BlockSpec((TM * ROWS, LANES), lambda i: (i, 0)),
        pl.BlockSpec((TM, LANES), lambda i: (i, 0)),
        pl.BlockSpec((None, CHUNK, D_HALF), lambda i: (seq_or_spare(i), 0, 0)),
        pl.BlockSpec((TM, D_HALF), lambda i: (sidx(i), 0)),
        pl.BlockSpec((None, CONV_W - 1, D_CONV), lambda i: (seq_or_spare(i), 0, 0)),
        pl.BlockSpec((TM, D_CONV), lambda i: (sidx(i), 0)),
    ]
    out_shape = [
        jax.ShapeDtypeStruct((t, D_MODEL), F32),
        jax.ShapeDtypeStruct((t * ROWS, LANES), PACKED),
        jax.ShapeDtypeStruct((t, LANES), F32),
        jax.ShapeDtypeStruct((n_batch + 1, CHUNK, D_HALF), F32),
        jax.ShapeDtypeStruct((ts, D_HALF), F32),
        jax.ShapeDtypeStruct((n_batch + 1, CONV_W - 1, D_CONV), F32),
        jax.ShapeDtypeStruct((ts, D_CONV), F32),
    ]
    return pl.pallas_call(
        functools.partial(_mixer_kernel, n_prompt_tiles=npt, tiles_per_seq=tiles_per_seq),
        grid=(npt + nst,),
        in_specs=in_specs,
        out_specs=out_specs,
        out_shape=out_shape,
        scratch_shapes=[
            pltpu.VMEM((8, D_CONV), F32),
            pltpu.VMEM((TM, D_HALF), BF16),
            pltpu.VMEM((TM, D_CONV), BF16),
            pltpu.VMEM((TM, D_MODEL), BF16),
        ],
        compiler_params=pltpu.CompilerParams(
            dimension_semantics=("arbitrary",), vmem_limit_bytes=VMEM_LIMIT),
        name="mixer",
    )(z, xp, xs, f, lng, lnb, cw, wsp, bsp, wa, wb, wo, gm, wr, br)


def _dispatch_kernel(pos_ref, cnt_ref, off_ref, nused_ref, xm_hbm, xs_hbm, inbuf, zbuf, isem, rsem, zsem,
                     *, n_steps, n_tiles):
    zbuf[...] = _packed_zeros(zbuf.shape)
    nused = nused_ref[0]
    pad_bits = TM_E.bit_length() - 1

    def pad_copies(e, fn):
        cnt = cnt_ref[e]
        npad = (TM_E - cnt % TM_E) % TM_E
        start = off_ref[e] + cnt
        for b in range(pad_bits):
            size = 1 << b
            at = start + ((npad >> (b + 1)) << (b + 1))

            @pl.when(((npad >> b) & 1) == 1)
            def _():
                fn(pltpu.make_async_copy(zbuf.at[pl.ds(0, size * ROWS), :],
                                         xs_hbm.at[pl.ds(at * ROWS, size * ROWS), :], zsem))

    def tile_copy(tile):
        return pltpu.make_async_copy(zbuf, xs_hbm.at[pl.ds(tile * TM_E * ROWS, TM_E * ROWS), :], zsem)

    def loop(lo, hi, body):
        lax.fori_loop(lo, hi, lambda k, c: (body(k), c)[1], 0)

    loop(0, N_EXPERTS, lambda e: pad_copies(e, lambda cp: cp.start()))
    loop(nused, n_tiles, lambda tile: tile_copy(tile).start())

    def in_copy(step, slot):
        return pltpu.make_async_copy(xm_hbm.at[pl.ds(step * COPY_TILE * ROWS, COPY_TILE * ROWS), :],
                                     inbuf.at[slot], isem.at[slot])

    def rows_wait(slot):
        for _ in range(2):
            pltpu.make_async_copy(inbuf.at[slot], xs_hbm.at[pl.ds(0, COPY_TILE * ROWS), :], rsem.at[slot]).wait()

    in_copy(0, 0).start()

    def step(s):
        slot = s % DISPATCH_SLOTS
        nxt = (s + 1) % DISPATCH_SLOTS

        @pl.when(s >= DISPATCH_SLOTS - 1)
        def _():
            rows_wait(nxt)

        @pl.when(s + 1 < n_steps)
        def _():
            in_copy(s + 1, nxt).start()

        in_copy(s, slot).wait()
        base = s * COPY_TILE * 2

        def body(r, carry):
            for k in range(2):
                p = pos_ref[base + 2 * r + k]
                pltpu.make_async_copy(inbuf.at[slot, pl.ds(r * ROWS, ROWS), :],
                                      xs_hbm.at[pl.ds(p * ROWS, ROWS), :], rsem.at[slot]).start(priority=k)
            return carry

        lax.fori_loop(0, COPY_TILE, body, 0, unroll=4)

    loop(0, n_steps, step)
    for back in range(DISPATCH_SLOTS - 1, 0, -1):
        rows_wait((n_steps - back) % DISPATCH_SLOTS)
    loop(0, N_EXPERTS, lambda e: pad_copies(e, lambda cp: cp.wait()))
    loop(nused, n_tiles, lambda tile: tile_copy(tile).wait())


def _dispatch(pos, cnt, off, nused, xm, *, n_tiles):
    n_tokens = xm.shape[0] // ROWS
    assert n_tokens % COPY_TILE == 0 and n_tokens // COPY_TILE >= DISPATCH_SLOTS
    smem = pl.BlockSpec(memory_space=pltpu.SMEM)
    return pl.pallas_call(
        functools.partial(_dispatch_kernel, n_steps=n_tokens // COPY_TILE, n_tiles=n_tiles),
        in_specs=[smem, smem, smem, smem, pl.BlockSpec(memory_space=pl.ANY)],
        out_specs=pl.BlockSpec(memory_space=pl.ANY),
        out_shape=jax.ShapeDtypeStruct((n_tiles * TM_E * ROWS, LANES), PACKED),
        scratch_shapes=[
            pltpu.VMEM((DISPATCH_SLOTS, COPY_TILE * ROWS, LANES), PACKED),
            pltpu.VMEM((TM_E * ROWS, LANES), PACKED),
            pltpu.SemaphoreType.DMA((DISPATCH_SLOTS,)),
            pltpu.SemaphoreType.DMA((DISPATCH_SLOTS,)),
            pltpu.SemaphoreType.DMA(()),
        ],
        compiler_params=pltpu.CompilerParams(vmem_limit_bytes=VMEM_LIMIT),
        name="dispatch",
    )(pos, cnt, off, nused, xm)


def _expert_kernel(te_ref, nused_ref, xs_ref, wg_hbm, wu_hbm, wd_hbm, y_ref,
                   wgf, wuf, wdf, wsem, seen_ref):
    i = pl.program_id(0)
    nused = nused_ref[0]

    def weight_copies(e, slot):
        return [pltpu.make_async_copy(src.at[0, e], dst.at[slot], wsem.at[slot, k])
                for k, (src, dst) in enumerate(((wg_hbm, wgf), (wu_hbm, wuf), (wd_hbm, wdf)))]

    @pl.when(i == 0)
    def _():
        seen_ref[0] = 0
        for cp in weight_copies(te_ref[0], 0):
            cp.start()

    @pl.when(i < nused)
    def _():
        e = te_ref[i]
        new_expert = jnp.logical_or(i == 0, e != te_ref[jnp.maximum(i - 1, 0)])

        @pl.when(new_expert)
        def _():
            slot = seen_ref[0] % 2
            seen_ref[0] = seen_ref[0] + 1
            seen_ref[1] = slot
            nxt = lax.while_loop(lambda j: jnp.logical_and(j < nused, te_ref[jnp.minimum(j, nused - 1)] == e),
                                 lambda j: j + 1, i + 1)

            @pl.when(nxt < nused)
            def _():
                for cp in weight_copies(te_ref[jnp.minimum(nxt, nused - 1)], 1 - slot):
                    cp.start(priority=1)

            for cp in weight_copies(e, slot):
                cp.wait()

        cur = seen_ref[1]
        x = _load_token_major(xs_ref, TM_E).astype(BF16)
        g = jnp.dot(x, wgf[cur].astype(BF16), preferred_element_type=F32)
        u = jnp.dot(x, wuf[cur].astype(BF16), preferred_element_type=F32)
        hid = (g * _sigmoid(g)) * u
        _store_token_major(y_ref, jnp.dot(hid.astype(BF16), wdf[cur].astype(BF16), preferred_element_type=F32))

    @pl.when(i >= nused)
    def _():
        y_ref[...] = _packed_zeros(y_ref.shape)


def _experts(te, nused, xs, wg, wu, wd, *, n_tiles):
    grid_spec = pltpu.PrefetchScalarGridSpec(
        num_scalar_prefetch=2,
        grid=(n_tiles,),
        in_specs=[
            pl.BlockSpec((TM_E * ROWS, LANES), lambda i, te, nu: (jnp.minimum(i, nu[0] - 1), 0)),
            pl.BlockSpec(memory_space=pl.ANY),
            pl.BlockSpec(memory_space=pl.ANY),
            pl.BlockSpec(memory_space=pl.ANY),
        ],
        out_specs=pl.BlockSpec((TM_E * ROWS, LANES), lambda i, te, nu: (i, 0)),
        scratch_shapes=[
            pltpu.VMEM((2, D_MODEL, D_EXPERT), F32),
            pltpu.VMEM((2, D_MODEL, D_EXPERT), F32),
            pltpu.VMEM((2, D_EXPERT, D_MODEL), F32),
            pltpu.SemaphoreType.DMA((2, 3)),
            pltpu.SMEM((2,), jnp.int32),
        ],
    )
    return pl.pallas_call(
        _expert_kernel,
        grid_spec=grid_spec,
        out_shape=jax.ShapeDtypeStruct((n_tiles * TM_E * ROWS, LANES), PACKED),
        compiler_params=pltpu.CompilerParams(
            dimension_semantics=("arbitrary",), vmem_limit_bytes=VMEM_LIMIT),
        name="experts",
    )(te, nused, xs, wg, wu, wd)


def _tail_kernel(pos_ref, ys_hbm, h_ref, route_ref, pp_ref, ps_ref, gp_ref, wpg_ref, wpp_ref, gf_ref,
                 yp_ref, ysm_ref, ybuf, sem, h2_ref, a_ref, proj_ref, *, n_prompt_tiles, n_tiles):
    i = pl.program_id(0)
    slot = i % 2
    is_prompt = i < n_prompt_tiles

    def issue(tile, dst_slot):
        base = tile * TM * 2

        def body(r, carry):
            for k in range(2):
                p = pos_ref[base + 2 * r + k]
                pltpu.make_async_copy(ys_hbm.at[pl.ds(p * ROWS, ROWS), :],
                                      ybuf.at[dst_slot, k, pl.ds(r * ROWS, ROWS), :],
                                      sem.at[dst_slot]).start()
            return carry

        lax.fori_loop(0, TM, body, 0, unroll=4)

    def slot_wait(s):
        for k in range(2):
            pltpu.make_async_copy(ys_hbm.at[pl.ds(0, TM * ROWS), :], ybuf.at[s, k], sem.at[s]).wait()

    def row_sumsq(acc):
        return jnp.sum(acc, axis=-1, keepdims=True) * (1.0 / D_MODEL)

    @pl.when(i == 0)
    def _():
        issue(0, 0)

    slot_wait(slot)

    w0 = jnp.broadcast_to(route_ref[:, 2:3], (TM, LANES))
    w1 = jnp.broadcast_to(route_ref[:, 3:4], (TM, LANES))
    acc = jnp.zeros((TM, LANES), F32)
    for s in range(ROWS):
        word0 = ybuf[slot, 0, pl.ds(s, TM, stride=ROWS), :]
        word1 = ybuf[slot, 1, pl.ds(s, TM, stride=ROWS), :]
        for idx in range(2):
            cols = slice((idx * ROWS + s) * LANES, (idx * ROWS + s + 1) * LANES)
            y0 = pltpu.unpack_elementwise(word0, index=idx, packed_dtype=BF16, unpacked_dtype=F32)
            y1 = pltpu.unpack_elementwise(word1, index=idx, packed_dtype=BF16, unpacked_dtype=F32)
            h2 = h_ref[:, cols] + (w0 * y0 + w1 * y1)
            h2_ref[:, cols] = h2
            acc = acc + h2 * h2
    r = lax.rsqrt(row_sumsq(acc) + EPS)
    a_ref[...] = (h2_ref[...] * r * gp_ref[...]).astype(BF16)
    p = jnp.where(is_prompt, pp_ref[...], ps_ref[...]).astype(BF16)
    proj_ref[...] = jnp.dot(p, wpp_ref[...], preferred_element_type=F32)

    nxt_base = jnp.minimum(i + 1, n_tiles - 1) * TM * 2
    rows_per_chunk = TM // TAIL_CHUNKS
    width = D_MODEL // TAIL_CHUNKS
    acc = jnp.zeros((TM, LANES), F32)
    for c in range(TAIL_CHUNKS):
        for rr in range(rows_per_chunk):
            row = c * rows_per_chunk + rr
            for k in range(2):
                p_row = pos_ref[nxt_base + 2 * row + k]
                pltpu.make_async_copy(ys_hbm.at[pl.ds(p_row * ROWS, ROWS), :],
                                      ybuf.at[1 - slot, k, pl.ds(row * ROWS, ROWS), :],
                                      sem.at[1 - slot]).start()
        cols = slice(c * width, (c + 1) * width)
        gate = _sigmoid(jnp.dot(a_ref[...], wpg_ref[:, cols], preferred_element_type=F32))
        h3 = h2_ref[:, cols] + gate * proj_ref[:, cols]
        h2_ref[:, cols] = h3
        for j in range(width // LANES):
            part = h3[:, j * LANES:(j + 1) * LANES]
            acc = acc + part * part
    scale = lax.rsqrt(row_sumsq(acc) + EPS)

    @pl.when(is_prompt)
    def _():
        yp_ref[...] = h2_ref[...] * scale * gf_ref[...]

    @pl.when(jnp.logical_not(is_prompt))
    def _():
        ysm_ref[...] = h2_ref[...] * scale * gf_ref[...]

    @pl.when(i == n_tiles - 1)
    def _():
        slot_wait(1 - slot)


def _tail(pos, ys, h, route, pp, ps, gp, wpg, wpp, gf):
    tp, ts = pp.shape[0], ps.shape[0]
    npt, nst = tp // TM, ts // TM
    ple = pp.shape[1]
    once = pl.Buffered(1)

    def pidx(i):
        return jnp.minimum(i, npt - 1)

    def sidx(i):
        return jnp.maximum(i - npt, 0)

    grid_spec = pltpu.PrefetchScalarGridSpec(
        num_scalar_prefetch=1,
        grid=(npt + nst,),
        in_specs=[
            pl.BlockSpec(memory_space=pl.ANY),
            pl.BlockSpec((TM, D_MODEL), lambda i, pos: (i, 0)),
            pl.BlockSpec((TM, LANES), lambda i, pos: (i, 0)),
            pl.BlockSpec((TM, ple), lambda i, pos: (pidx(i), 0)),
            pl.BlockSpec((TM, ple), lambda i, pos: (sidx(i), 0)),
            pl.BlockSpec((1, D_MODEL), lambda i, pos: (0, 0), pipeline_mode=once),
            pl.BlockSpec((D_MODEL, D_MODEL), lambda i, pos: (0, 0), pipeline_mode=once),
            pl.BlockSpec((ple, D_MODEL), lambda i, pos: (0, 0), pipeline_mode=once),
            pl.BlockSpec((1, D_MODEL), lambda i, pos: (0, 0), pipeline_mode=once),
        ],
        out_specs=[
            pl.BlockSpec((TM, D_MODEL), lambda i, pos: (pidx(i), 0)),
            pl.BlockSpec((TM, D_MODEL), lambda i, pos: (sidx(i), 0)),
        ],
        scratch_shapes=[
            pltpu.VMEM((2, 2, TM * ROWS, LANES), PACKED),
            pltpu.SemaphoreType.DMA((2,)),
            pltpu.VMEM((TM, D_MODEL), F32),
            pltpu.VMEM((TM, D_MODEL), BF16),
            pltpu.VMEM((TM, D_MODEL), F32),
        ],
    )
    return pl.pallas_call(
        functools.partial(_tail_kernel, n_prompt_tiles=npt, n_tiles=npt + nst),
        grid_spec=grid_spec,
        out_shape=[jax.ShapeDtypeStruct((tp, D_MODEL), F32), jax.ShapeDtypeStruct((ts, D_MODEL), F32)],
        compiler_params=pltpu.CompilerParams(
            dimension_semantics=("arbitrary",), vmem_limit_bytes=VMEM_LIMIT),
        name="tail",
    )(pos, ys, h, route, pp, ps, gp, wpg, wpp, gf)


def _route_tables(route, n_tiles):
    e = route[:, 0:2].astype(jnp.int32).reshape(-1)
    onehot = (e[:, None] == jnp.arange(N_EXPERTS, dtype=jnp.int32)[None, :]).astype(jnp.int32)
    cum = jnp.cumsum(onehot, axis=0)
    counts = cum[-1]
    rank = jnp.sum((cum - onehot) * onehot, axis=1)
    padded = ((counts + TM_E - 1) // TM_E) * TM_E
    ends = jnp.cumsum(padded)
    offs = ends - padded
    pos = jnp.sum(onehot * offs[None, :], axis=1) + rank
    nused = ends[-1] // TM_E
    tile_id = jnp.arange(n_tiles, dtype=jnp.int32)
    te = jnp.sum((tile_id[:, None] * TM_E >= ends[None, :]).astype(jnp.int32), axis=1)
    te_last = jnp.sum((((nused - 1) * TM_E) >= ends).astype(jnp.int32))
    te = jnp.where(tile_id < nused, te, te_last)
    i32 = lambda a: a.astype(jnp.int32)
    return i32(te), i32(nused).reshape(1), i32(pos), i32(counts), i32(offs)


def kernel(x_prompt, x_sample, state_conv, p_prompt, p_sample, g_mix, w_in, ln_v_g, ln_v_b, w_spatial, b_spatial, conv_w, w_br_a, w_br_b, w_out, g_moe, w_router_group, b_router_group, w_router_expert, b_router_expert, w_exp_gate, w_exp_up, w_exp_down, g_ple, w_ple_gate, w_ple_proj, g_final):
    depth = w_in.shape[0]
    assert depth == 1, "single-layer step only"
    nb, seq, _ = x_prompt.shape
    ns, dseq, _ = x_sample.shape
    assert dseq == 8 and seq % TM == 0 and (ns * dseq) % TM_NORM == 0 and (nb * seq) % TM_NORM == 0
    assert (nb * seq + ns * dseq) % TM_IN == 0
    tp, ts = nb * seq, ns * dseq
    t = tp + ts
    xp = x_prompt.reshape(tp, D_MODEL)
    xs = x_sample.reshape(ts, D_MODEL)

    row = lambda a: a.reshape(1, -1)
    tril = jnp.tril(jnp.ones((CHUNK, CHUNK), dtype=bool))
    w_sp = jnp.where(tril[None], w_spatial[0], 0.0)
    eye = jnp.eye(CHUNK // dseq, dtype=F32)
    w_sp_s = jnp.einsum("ij,gts->gitjs", eye, w_sp[:, :dseq, :dseq]).reshape(G_A, CHUNK, CHUNK)
    wsp = jnp.stack([w_sp, w_sp_s]).astype(BF16)
    b_p = jnp.repeat(b_spatial[0].T, D_GROUP_A, axis=1)
    b_s = jnp.tile(b_p[:dseq], (CHUNK // dseq, 1))
    bsp = jnp.stack([b_p, b_s])
    lane_pad = LANES - N_GROUPS - N_EXPERTS
    wr = jnp.concatenate([w_router_group[0], w_router_expert[0], jnp.zeros((D_MODEL, lane_pad), F32)], axis=1)
    br = jnp.concatenate([b_router_group[0], b_router_expert[0], jnp.zeros((lane_pad,), F32)]).reshape(1, LANES)
    f = jnp.pad(state_conv[0], ((0, 0), (0, dseq - (CONV_W - 1)), (0, 0))).reshape(ts, D_CONV)

    n = _norm(xp, xs, row(g_mix[0]))
    z, (wa, wb, wo, wpg) = _inproj(n, w_in, [w_br_a, w_br_b, w_out, w_ple_gate])
    h, xm, route, vnp, vns, qp, qs = _mixer(
        z, xp, xs, f, row(ln_v_g[0]), row(ln_v_b[0]), conv_w[0], wsp, bsp, wa, wb, wo,
        row(g_moe[0]), wr.astype(BF16), br, n_batch=nb)

    n_tiles = (2 * t) // TM_E + N_EXPERTS
    te, nused, pos, counts, offs = _route_tables(route, n_tiles)
    xsort = _dispatch(pos, counts, offs, nused, xm, n_tiles=n_tiles)
    ys = _experts(te, nused, xsort, w_exp_gate, w_exp_up, w_exp_down, n_tiles=n_tiles)

    yp, ysm = _tail(pos, ys, h, route, p_prompt[0].reshape(tp, -1), p_sample[0].reshape(ts, -1),
                    row(g_ple[0]), wpg, w_ple_proj[0].astype(BF16), row(g_final))

    y_prompt = yp.reshape(nb, seq, D_MODEL)
    y_sample = ysm.reshape(ns, dseq, D_MODEL)
    conv_state_prompt = qp[:nb][None]
    conv_state_sample = qs.reshape(ns, dseq, D_CONV)[:, dseq - (CONV_W - 1):][None]
    v_rows_prompt = vnp[:nb][None]
    v_rows_sample = vns.reshape(ns, dseq, D_HALF)[None]
    return (y_prompt, y_sample, conv_state_prompt, conv_state_sample, v_rows_prompt, v_rows_sample)
```

```python
import functools

import jax
import jax.numpy as jnp
from jax import lax
from jax.experimental import pallas as pl
from jax.experimental.pallas import tpu as pltpu

D_MODEL = 2048
D_HALF = D_MODEL // 2
D_CONV = D_MODEL // 2
G_A = 4
D_GROUP_A = D_HALF // G_A
CHUNK = 128
CONV_W = 3
N_GROUPS = 4
EXPERTS_PER_GROUP = 8
N_EXPERTS = N_GROUPS * EXPERTS_PER_GROUP
D_EXPERT = D_MODEL // 4
D_IN_TOTAL = 2 * D_HALF + 3 * D_CONV + 2 * D_MODEL
EPS = 1e-6

BF16 = jnp.bfloat16
F32 = jnp.float32

LANES = 128
VMEM_LIMIT = 60 * 1024 * 1024

TM_NORM = 512
TM_IN = 1024
CAST_ROWS = 128
TN_IN = 1024
TM = 256
TM_E = 256
ROWS = D_MODEL // (2 * LANES)
PACKED = jnp.uint32
COPY_TILE = 256
DISPATCH_SLOTS = 3
TAIL_CHUNKS = 8
MERGE_CHUNKS = 8
SAMPLE_SEQ = 8


def _rms(x, g):
    return x * lax.rsqrt(jnp.mean(x * x, axis=-1, keepdims=True) + EPS) * g


def _sigmoid(x):
    return 1.0 / (1.0 + jnp.exp(-x))


def _store_token_major(ref, x):
    m = x.shape[0]
    for s in range(ROWS):
        lo = x[:, s * LANES:(s + 1) * LANES]
        hi = x[:, (ROWS + s) * LANES:(ROWS + s + 1) * LANES]
        ref[pl.ds(s, m, stride=ROWS), :] = pltpu.pack_elementwise([lo, hi], packed_dtype=BF16)


def _packed_zeros(shape):
    zero = jnp.zeros(shape, F32)
    return pltpu.pack_elementwise([zero, zero], packed_dtype=BF16)


def _unpack_words(words):
    half = lambda idx: [pltpu.unpack_elementwise(w, index=idx, packed_dtype=BF16, unpacked_dtype=F32)
                        for w in words]
    return jnp.concatenate(half(0) + half(1), axis=1)


def _load_token_major(ref, m):
    return _unpack_words([ref[pl.ds(s, m, stride=ROWS), :] for s in range(ROWS)])


def _norm_kernel(xp_ref, xs_ref, g_ref, o_ref, *, n_prompt_tiles):
    i = pl.program_id(0)

    def run(x_ref):
        o_ref[...] = _rms(x_ref[...], g_ref[...]).astype(BF16)

    @pl.when(i < n_prompt_tiles)
    def _():
        run(xp_ref)

    @pl.when(i >= n_prompt_tiles)
    def _():
        run(xs_ref)


def _norm(xp, xs, g):
    tp, ts = xp.shape[0], xs.shape[0]
    npt, nst = tp // TM_NORM, ts // TM_NORM
    return pl.pallas_call(
        functools.partial(_norm_kernel, n_prompt_tiles=npt),
        grid=(npt + nst,),
        in_specs=[
            pl.BlockSpec((TM_NORM, D_MODEL), lambda i: (jnp.minimum(i, npt - 1), 0)),
            pl.BlockSpec((TM_NORM, D_MODEL), lambda i: (jnp.maximum(i - npt, 0), 0)),
            pl.BlockSpec((1, D_MODEL), lambda i: (0, 0)),
        ],
        out_specs=pl.BlockSpec((TM_NORM, D_MODEL), lambda i: (i, 0)),
        out_shape=jax.ShapeDtypeStruct((tp + ts, D_MODEL), BF16),
        compiler_params=pltpu.CompilerParams(
            dimension_semantics=("arbitrary",), vmem_limit_bytes=VMEM_LIMIT),
        name="norm",
    )(xp, xs, g)


def _inproj_kernel(n_ref, w_ref, *refs, n_side):
    side_in, z_ref, side_out = refs[:n_side], refs[n_side], refs[n_side + 1:]

    z_ref[...] = jnp.dot(n_ref[...], w_ref[...].astype(BF16), preferred_element_type=F32).astype(BF16)
    for src, dst in zip(side_in, side_out):
        dst[...] = src[...].astype(BF16)


def _inproj(n, w_in, side_weights):
    t = n.shape[0]
    n_j, n_i = D_IN_TOTAL // TN_IN, t // TM_IN
    starts, start = [], 0
    for w in side_weights:
        assert w.shape[1] % CAST_ROWS == 0
        starts.append(start)
        start += w.shape[1] // CAST_ROWS
    assert start <= n_j * n_i, "not enough grid steps for the weight casts"

    def side_index(k):
        n_blk = side_weights[k].shape[1] // CAST_ROWS
        return lambda j, i: jnp.clip(j * n_i + i - starts[k], 0, n_blk - 1)

    side_in = [pl.BlockSpec((None, CAST_ROWS, w.shape[2]), lambda j, i, f=side_index(k): (0, f(j, i), 0))
               for k, w in enumerate(side_weights)]
    side_out = [pl.BlockSpec((CAST_ROWS, w.shape[2]), lambda j, i, f=side_index(k): (f(j, i), 0))
                for k, w in enumerate(side_weights)]
    outs = pl.pallas_call(
        functools.partial(_inproj_kernel, n_side=len(side_weights)),
        grid=(n_j, n_i),
        in_specs=[
            pl.BlockSpec((TM_IN, D_MODEL), lambda j, i: (i, 0)),
            pl.BlockSpec((None, D_MODEL, TN_IN), lambda j, i: (0, 0, j)),
        ] + side_in,
        out_specs=[pl.BlockSpec((TM_IN, TN_IN), lambda j, i: (i, j))] + side_out,
        out_shape=[jax.ShapeDtypeStruct((t, D_IN_TOTAL), BF16)]
        + [jax.ShapeDtypeStruct(w.shape[1:], BF16) for w in side_weights],
        compiler_params=pltpu.CompilerParams(
            dimension_semantics=("arbitrary", "arbitrary"), vmem_limit_bytes=VMEM_LIMIT),
        name="in_proj",
    )(n, w_in, *side_weights)
    return outs[0], outs[1:]


def _route(lg):
    shape = lg.shape
    lane = lax.broadcasted_iota(jnp.int32, shape, 1)
    lanef = lane.astype(F32)
    neg = jnp.float32(-jnp.inf)
    far = jnp.float32(LANES)
    is_g = lane < N_GROUPS
    lgm = jnp.where(is_g, lg, neg)
    gmax = jnp.max(lgm, axis=-1, keepdims=True)
    gsel = jnp.min(jnp.where(lgm == gmax, lanef, far), axis=-1, keepdims=True)
    p_group = 1.0 / jnp.sum(jnp.where(is_g, jnp.exp(lgm - gmax), 0.0), axis=-1, keepdims=True)
    lane_group = ((lane - N_GROUPS) >> (EXPERTS_PER_GROUP.bit_length() - 1)).astype(F32)
    is_e = lane_group == gsel
    le = jnp.where(is_e, lg, neg)
    m1 = jnp.max(le, axis=-1, keepdims=True)
    i1 = jnp.min(jnp.where(le == m1, lanef, far), axis=-1, keepdims=True)
    le2 = jnp.where(lanef == i1, neg, le)
    m2 = jnp.max(le2, axis=-1, keepdims=True)
    i2 = jnp.min(jnp.where(le2 == m2, lanef, far), axis=-1, keepdims=True)
    e21 = jnp.exp(m2 - m1)
    w1 = 1.0 / (1.0 + e21)
    w2 = e21 * w1
    out = jnp.where(lane == 0, i1 - N_GROUPS,
                    jnp.where(lane == 1, i2 - N_GROUPS,
                              jnp.where(lane == 2, p_group * w1,
                                        jnp.where(lane == 3, p_group * w2, 0.0))))
    return out


def _mixer_kernel(z_ref, xp_ref, xs_ref, f_ref, lng_ref, lnb_ref, cw_ref, wsp_ref, bsp_ref,
                  wa_ref, wb_ref, wo_ref, gm_ref, wr_ref, br_ref,
                  h_ref, xm_ref, route_ref, vnp_ref, vns_ref, qp_ref, qs_ref,
                  qprev_ref, ya_ref, yb_ref, m_ref, *, n_prompt_tiles, tiles_per_seq):
    i = pl.program_id(0)
    is_prompt = i < n_prompt_tiles
    c_u, c_v, c_b, c_c, c_x = (k * D_HALF for k in range(5))
    c_ga = c_x + D_CONV
    c_gb = c_ga + D_MODEL

    v = z_ref[:, c_v:c_v + D_HALF].astype(F32)
    mu = jnp.mean(v, axis=-1, keepdims=True)
    vc = v - mu
    vn = vc * lax.rsqrt(jnp.mean(vc * vc, axis=-1, keepdims=True) + EPS) * lng_ref[...] + lnb_ref[...]
    vnb = vn.astype(BF16)
    for c in range(TM // CHUNK):
        rows = slice(c * CHUNK, (c + 1) * CHUNK)
        for g in range(G_A):
            cols = slice(g * D_GROUP_A, (g + 1) * D_GROUP_A)
            s = jnp.dot(wsp_ref[g], vnb[rows, cols], preferred_element_type=F32) + bsp_ref[:, cols]
            u = z_ref[rows, c_u + g * D_GROUP_A:c_u + (g + 1) * D_GROUP_A].astype(F32)
            ya_ref[rows, cols] = (u * s).astype(BF16)

    q = z_ref[:, c_c:c_c + D_CONV].astype(F32) * z_ref[:, c_x:c_x + D_CONV].astype(F32)
    q1 = pltpu.roll(q, 1, 0)
    q2 = pltpu.roll(q, 2, 0)
    row = lax.broadcasted_iota(jnp.int32, q.shape, 0)
    cw0, cw1, cw2 = cw_ref[0:1, :], cw_ref[1:2, :], cw_ref[2:3, :]

    seq_start = i % tiles_per_seq == 0
    p6 = jnp.where(seq_start, 0.0, qprev_ref[6:7, :])
    p7 = jnp.where(seq_start, 0.0, qprev_ref[7:8, :])
    f = f_ref[...]
    pos_in_seq = jnp.where(is_prompt, row, row & (SAMPLE_SEQ - 1))
    fix1 = jnp.where(is_prompt, jnp.broadcast_to(p7, q.shape), pltpu.roll(f, TM - 1, 0))
    fix2 = jnp.where(is_prompt, jnp.where(row == 0, p6, p7), f)
    a1 = jnp.where(pos_in_seq == 0, fix1, q1)
    a2 = jnp.where(pos_in_seq < 2, fix2, q2)
    yc = cw0 * a2 + cw1 * a1 + cw2 * q
    qprev_ref[...] = q[TM - 8:TM, :]
    qp_ref[...] = q[TM - (CONV_W - 1):TM, :]
    vnp_ref[...] = vn[TM - CHUNK:TM, :]
    qs_ref[...] = q
    vns_ref[...] = vn

    yb_ref[...] = (z_ref[:, c_b:c_b + D_CONV].astype(F32) * yc).astype(BF16)

    width = D_MODEL // MERGE_CHUNKS
    for c in range(MERGE_CHUNKS):
        cols = slice(c * width, (c + 1) * width)
        a = jnp.dot(ya_ref[...], wa_ref[:, cols], preferred_element_type=F32)
        b = jnp.dot(yb_ref[...], wb_ref[:, cols], preferred_element_type=F32)
        ga = z_ref[:, c_ga + c * width:c_ga + (c + 1) * width].astype(F32)
        gb = z_ref[:, c_gb + c * width:c_gb + (c + 1) * width].astype(F32)
        m_ref[:, cols] = (_sigmoid(ga) * a + _sigmoid(gb) * b).astype(BF16)

    x = jnp.where(is_prompt, xp_ref[...], xs_ref[...])
    h = x + jnp.dot(m_ref[...], wo_ref[...], preferred_element_type=F32)
    h_ref[...] = h
    xm = _rms(h, gm_ref[...])
    _store_token_major(xm_ref, xm)
    lg = jnp.dot(xm.astype(BF16), wr_ref[...], preferred_element_type=F32) + br_ref[...]
    route_ref[...] = _route(lg)


def _mixer(z, xp, xs, f, lng, lnb, cw, wsp, bsp, wa, wb, wo, gm, wr, br, *, n_batch):
    tp, ts = xp.shape[0], xs.shape[0]
    t = tp + ts
    npt, nst = tp // TM, ts // TM
    tiles_per_seq = tp // n_batch // TM
    once = pl.Buffered(1)

    def pidx(i):
        return jnp.minimum(i, npt - 1)

    def sidx(i):
        return jnp.maximum(i - npt, 0)

    def seq_or_spare(i):
        return jnp.where(i < npt, i // tiles_per_seq, n_batch)

    const = lambda shape: pl.BlockSpec(shape, lambda i: (0,) * len(shape), pipeline_mode=once)
    in_specs = [
        pl.BlockSpec((TM, D_IN_TOTAL), lambda i: (i, 0)),
        pl.BlockSpec((TM, D_MODEL), lambda i: (pidx(i), 0)),
        pl.BlockSpec((TM, D_MODEL), lambda i: (sidx(i), 0)),
        pl.BlockSpec((TM, D_CONV), lambda i: (sidx(i), 0)),
        const((1, D_HALF)), const((1, D_HALF)), const((CONV_W, D_CONV)),
        pl.BlockSpec((None, G_A, CHUNK, CHUNK), lambda i: (jnp.where(i < npt, 0, 1), 0, 0, 0)),
        pl.BlockSpec((None, CHUNK, D_HALF), lambda i: (jnp.where(i < npt, 0, 1), 0, 0)),
        const((D_HALF, D_MODEL)), const((D_CONV, D_MODEL)), const((D_MODEL, D_MODEL)),
        const((1, D_MODEL)), const((D_MODEL, LANES)), const((1, LANES)),
    ]
    out_specs = [
        pl.BlockSpec((TM, D_MODEL), lambda i: (i, 0)),
        pl.BlockSpec((TM * ROWS, LANES), lambda i: (i, 0)),
        pl.BlockSpec((TM, LANES), lambda i: (i, 0)),
        pl.BlockSpec((None, CHUNK, D_HALF), lambda i: (seq_or_spare(i), 0, 0)),
        pl.BlockSpec((TM, D_HALF), lambda i: (sidx(i), 0)),
        pl.BlockSpec((None, CONV_W - 1, D_CONV), lambda i: (seq_or_spare(i), 0, 0)),
        pl.BlockSpec((TM, D_CONV), lambda i: (sidx(i), 0)),
    ]
    out_shape = [
        jax.ShapeDtypeStruct((t, D_MODEL), F32),
        jax.ShapeDtypeStruct((t * ROWS, LANES), PACKED),
        jax.ShapeDtypeStruct((t, LANES), F32),
        jax.ShapeDtypeStruct((n_batch + 1, CHUNK, D_HALF), F32),
        jax.ShapeDtypeStruct((ts, D_HALF), F32),
        jax.ShapeDtypeStruct((n_batch + 1, CONV_W - 1, D_CONV), F32),
        jax.ShapeDtypeStruct((ts, D_CONV), F32),
    ]
    return pl.pallas_call(
        functools.partial(_mixer_kernel, n_prompt_tiles=npt, tiles_per_seq=tiles_per_seq),
        grid=(npt + nst,),
        in_specs=in_specs,
        out_specs=out_specs,
        out_shape=out_shape,
        scratch_shapes=[
            pltpu.VMEM((8, D_CONV), F32),
            pltpu.VMEM((TM, D_HALF), BF16),
            pltpu.VMEM((TM, D_CONV), BF16),
            pltpu.VMEM((TM, D_MODEL), BF16),
        ],
        compiler_params=pltpu.CompilerParams(
            dimension_semantics=("arbitrary",), vmem_limit_bytes=VMEM_LIMIT),
        name="mixer",
    )(z, xp, xs, f, lng, lnb, cw, wsp, bsp, wa, wb, wo, gm, wr, br)


def _dispatch_kernel(pos_ref, cnt_ref, off_ref, nused_ref, xm_hbm, xs_hbm, inbuf, zbuf, isem, rsem, zsem,
                     *, n_steps, n_tiles):
    zbuf[...] = _packed_zeros(zbuf.shape)
    nused = nused_ref[0]
    pad_bits = TM_E.bit_length() - 1

    def pad_copies(e, fn):
        cnt = cnt_ref[e]
        npad = (TM_E - cnt % TM_E) % TM_E
        start = off_ref[e] + cnt
        for b in range(pad_bits):
            size = 1 << b
            at = start + ((npad >> (b + 1)) << (b + 1))

            @pl.when(((npad >> b) & 1) == 1)
            def _():
                fn(pltpu.make_async_copy(zbuf.at[pl.ds(0, size * ROWS), :],
                                         xs_hbm.at[pl.ds(at * ROWS, size * ROWS), :], zsem))

    def tile_copy(tile):
        return pltpu.make_async_copy(zbuf, xs_hbm.at[pl.ds(tile * TM_E * ROWS, TM_E * ROWS), :], zsem)

    def loop(lo, hi, body):
        lax.fori_loop(lo, hi, lambda k, c: (body(k), c)[1], 0)

    loop(0, N_EXPERTS, lambda e: pad_copies(e, lambda cp: cp.start()))
    loop(nused, n_tiles, lambda tile: tile_copy(tile).start())

    def in_copy(step, slot):
        return pltpu.make_async_copy(xm_hbm.at[pl.ds(step * COPY_TILE * ROWS, COPY_TILE * ROWS), :],
                                     inbuf.at[slot], isem.at[slot])

    def rows_wait(slot):
        for _ in range(2):
            pltpu.make_async_copy(inbuf.at[slot], xs_hbm.at[pl.ds(0, COPY_TILE * ROWS), :], rsem.at[slot]).wait()

    in_copy(0, 0).start()

    def step(s):
        slot = s % DISPATCH_SLOTS
        nxt = (s + 1) % DISPATCH_SLOTS

        @pl.when(s >= DISPATCH_SLOTS - 1)
        def _():
            rows_wait(nxt)

        @pl.when(s + 1 < n_steps)
        def _():
            in_copy(s + 1, nxt).start()

        in_copy(s, slot).wait()
        base = s * COPY_TILE * 2

        def body(r, carry):
            for k in range(2):
                p = pos_ref[base + 2 * r + k]
                pltpu.make_async_copy(inbuf.at[slot, pl.ds(r * ROWS, ROWS), :],
                                      xs_hbm.at[pl.ds(p * ROWS, ROWS), :], rsem.at[slot]).start(priority=k)
            return carry

        lax.fori_loop(0, COPY_TILE, body, 0, unroll=4)

    loop(0, n_steps, step)
    for back in range(DISPATCH_SLOTS - 1, 0, -1):
        rows_wait((n_steps - back) % DISPATCH_SLOTS)
    loop(0, N_EXPERTS, lambda e: pad_copies(e, lambda cp: cp.wait()))
    loop(nused, n_tiles, lambda tile: tile_copy(tile).wait())


def _dispatch(pos, cnt, off, nused, xm, *, n_tiles):
    n_tokens = xm.shape[0] // ROWS
    assert n_tokens % COPY_TILE == 0 and n_tokens // COPY_TILE >= DISPATCH_SLOTS
    smem = pl.BlockSpec(memory_space=pltpu.SMEM)
    return pl.pallas_call(
        functools.partial(_dispatch_kernel, n_steps=n_tokens // COPY_TILE, n_tiles=n_tiles),
        in_specs=[smem, smem, smem, smem, pl.BlockSpec(memory_space=pl.ANY)],
        out_specs=pl.BlockSpec(memory_space=pl.ANY),
        out_shape=jax.ShapeDtypeStruct((n_tiles * TM_E * ROWS, LANES), PACKED),
        scratch_shapes=[
            pltpu.VMEM((DISPATCH_SLOTS, COPY_TILE * ROWS, LANES), PACKED),
            pltpu.VMEM((TM_E * ROWS, LANES), PACKED),
            pltpu.SemaphoreType.DMA((DISPATCH_SLOTS,)),
            pltpu.SemaphoreType.DMA((DISPATCH_SLOTS,)),
            pltpu.SemaphoreType.DMA(()),
        ],
        compiler_params=pltpu.CompilerParams(vmem_limit_bytes=VMEM_LIMIT),
        name="dispatch",
    )(pos, cnt, off, nused, xm)


def _expert_kernel(te_ref, nused_ref, xs_ref, wg_hbm, wu_hbm, wd_hbm, y_ref,
                   wgf, wuf, wdf, wsem, seen_ref):
    i = pl.program_id(0)
    nused = nused_ref[0]

    def weight_copies(e, slot):
        return [pltpu.make_async_copy(src.at[0, e], dst.at[slot], wsem.at[slot, k])
                for k, (src, dst) in enumerate(((wg_hbm, wgf), (wu_hbm, wuf), (wd_hbm, wdf)))]

    @pl.when(i == 0)
    def _():
        seen_ref[0] = 0
        for cp in weight_copies(te_ref[0], 0):
            cp.start()

    @pl.when(i < nused)
    def _():
        e = te_ref[i]
        new_expert = jnp.logical_or(i == 0, e != te_ref[jnp.maximum(i - 1, 0)])

        @pl.when(new_expert)
        def _():
            slot = seen_ref[0] % 2
            seen_ref[0] = seen_ref[0] + 1
            seen_ref[1] = slot
            nxt = lax.while_loop(lambda j: jnp.logical_and(j < nused, te_ref[jnp.minimum(j, nused - 1)] == e),
                                 lambda j: j + 1, i + 1)

            @pl.when(nxt < nused)
            def _():
                for cp in weight_copies(te_ref[jnp.minimum(nxt, nused - 1)], 1 - slot):
                    cp.start(priority=1)

            for cp in weight_copies(e, slot):
                cp.wait()

        cur = seen_ref[1]
        x = _load_token_major(xs_ref, TM_E).astype(BF16)
        g = jnp.dot(x, wgf[cur].astype(BF16), preferred_element_type=F32)
        u = jnp.dot(x, wuf[cur].astype(BF16), preferred_element_type=F32)
        hid = (g * _sigmoid(g)) * u
        _store_token_major(y_ref, jnp.dot(hid.astype(BF16), wdf[cur].astype(BF16), preferred_element_type=F32))

    @pl.when(i >= nused)
    def _():
        y_ref[...] = _packed_zeros(y_ref.shape)


def _experts(te, nused, xs, wg, wu, wd, *, n_tiles):
    grid_spec = pltpu.PrefetchScalarGridSpec(
        num_scalar_prefetch=2,
        grid=(n_tiles,),
        in_specs=[
            pl.BlockSpec((TM_E * ROWS, LANES), lambda i, te, nu: (jnp.minimum(i, nu[0] - 1), 0)),
            pl.BlockSpec(memory_space=pl.ANY),
            pl.BlockSpec(memory_space=pl.ANY),
            pl.BlockSpec(memory_space=pl.ANY),
        ],
        out_specs=pl.BlockSpec((TM_E * ROWS, LANES), lambda i, te, nu: (i, 0)),
        scratch_shapes=[
            pltpu.VMEM((2, D_MODEL, D_EXPERT), F32),
            pltpu.VMEM((2, D_MODEL, D_EXPERT), F32),
            pltpu.VMEM((2, D_EXPERT, D_MODEL), F32),
            pltpu.SemaphoreType.DMA((2, 3)),
            pltpu.SMEM((2,), jnp.int32),
        ],
    )
    return pl.pallas_call(
        _expert_kernel,
        grid_spec=grid_spec,
        out_shape=jax.ShapeDtypeStruct((n_tiles * TM_E * ROWS, LANES), PACKED),
        compiler_params=pltpu.CompilerParams(
            dimension_semantics=("arbitrary",), vmem_limit_bytes=VMEM_LIMIT),
        name="experts",
    )(te, nused, xs, wg, wu, wd)


def _tail_kernel(pos_ref, ys_hbm, h_ref, route_ref, pp_ref, ps_ref, gp_ref, wpg_ref, wpp_ref, gf_ref,
                 yp_ref, ysm_ref, ybuf, sem, h2_ref, a_ref, proj_ref, *, n_prompt_tiles, n_tiles):
    i = pl.program_id(0)
    slot = i % 2
    is_prompt = i < n_prompt_tiles

    def issue(tile, dst_slot):
        base = tile * TM * 2

        def body(r, carry):
            for k in range(2):
                p = pos_ref[base + 2 * r + k]
                pltpu.make_async_copy(ys_hbm.at[pl.ds(p * ROWS, ROWS), :],
                                      ybuf.at[dst_slot, k, pl.ds(r * ROWS, ROWS), :],
                                      sem.at[dst_slot]).start()
            return carry

        lax.fori_loop(0, TM, body, 0, unroll=4)

    def slot_wait(s):
        for k in range(2):
            pltpu.make_async_copy(ys_hbm.at[pl.ds(0, TM * ROWS), :], ybuf.at[s, k], sem.at[s]).wait()

    def row_sumsq(acc):
        return jnp.sum(acc, axis=-1, keepdims=True) * (1.0 / D_MODEL)

    @pl.when(i == 0)
    def _():
        issue(0, 0)

    slot_wait(slot)

    w0 = jnp.broadcast_to(route_ref[:, 2:3], (TM, LANES))
    w1 = jnp.broadcast_to(route_ref[:, 3:4], (TM, LANES))
    acc = jnp.zeros((TM, LANES), F32)
    for s in range(ROWS):
        word0 = ybuf[slot, 0, pl.ds(s, TM, stride=ROWS), :]
        word1 = ybuf[slot, 1, pl.ds(s, TM, stride=ROWS), :]
        for idx in range(2):
            cols = slice((idx * ROWS + s) * LANES, (idx * ROWS + s + 1) * LANES)
            y0 = pltpu.unpack_elementwise(word0, index=idx, packed_dtype=BF16, unpacked_dtype=F32)
            y1 = pltpu.unpack_elementwise(word1, index=idx, packed_dtype=BF16, unpacked_dtype=F32)
            h2 = h_ref[:, cols] + (w0 * y0 + w1 * y1)
            h2_ref[:, cols] = h2
            acc = acc + h2 * h2
    r = lax.rsqrt(row_sumsq(acc) + EPS)
    a_ref[...] = (h2_ref[...] * r * gp_ref[...]).astype(BF16)
    p = jnp.where(is_prompt, pp_ref[...], ps_ref[...]).astype(BF16)
    proj_ref[...] = jnp.dot(p, wpp_ref[...], preferred_element_type=F32)

    nxt_base = jnp.minimum(i + 1, n_tiles - 1) * TM * 2
    rows_per_chunk = TM // TAIL_CHUNKS
    width = D_MODEL // TAIL_CHUNKS
    acc = jnp.zeros((TM, LANES), F32)
    for c in range(TAIL_CHUNKS):
        for rr in range(rows_per_chunk):
            row = c * rows_per_chunk + rr
            for k in range(2):
                p_row = pos_ref[nxt_base + 2 * row + k]
                pltpu.make_async_copy(ys_hbm.at[pl.ds(p_row * ROWS, ROWS), :],
                                      ybuf.at[1 - slot, k, pl.ds(row * ROWS, ROWS), :],
                                      sem.at[1 - slot]).start()
        cols = slice(c * width, (c + 1) * width)
        gate = _sigmoid(jnp.dot(a_ref[...], wpg_ref[:, cols], preferred_element_type=F32))
        h3 = h2_ref[:, cols] + gate * proj_ref[:, cols]
        h2_ref[:, cols] = h3
        for j in range(width // LANES):
            part = h3[:, j * LANES:(j + 1) * LANES]
            acc = acc + part * part
    scale = lax.rsqrt(row_sumsq(acc) + EPS)

    @pl.when(is_prompt)
    def _():
        yp_ref[...] = h2_ref[...] * scale * gf_ref[...]

    @pl.when(jnp.logical_not(is_prompt))
    def _():
        ysm_ref[...] = h2_ref[...] * scale * gf_ref[...]

    @pl.when(i == n_tiles - 1)
    def _():
        slot_wait(1 - slot)


def _tail(pos, ys, h, route, pp, ps, gp, wpg, wpp, gf):
    tp, ts = pp.shape[0], ps.shape[0]
    npt, nst = tp // TM, ts // TM
    ple = pp.shape[1]
    once = pl.Buffered(1)

    def pidx(i):
        return jnp.minimum(i, npt - 1)

    def sidx(i):
        return jnp.maximum(i - npt, 0)

    grid_spec = pltpu.PrefetchScalarGridSpec(
        num_scalar_prefetch=1,
        grid=(npt + nst,),
        in_specs=[
            pl.BlockSpec(memory_space=pl.ANY),
            pl.BlockSpec((TM, D_MODEL), lambda i, pos: (i, 0)),
            pl.BlockSpec((TM, LANES), lambda i, pos: (i, 0)),
            pl.BlockSpec((TM, ple), lambda i, pos: (pidx(i), 0)),
            pl.BlockSpec((TM, ple), lambda i, pos: (sidx(i), 0)),
            pl.BlockSpec((1, D_MODEL), lambda i, pos: (0, 0), pipeline_mode=once),
            pl.BlockSpec((D_MODEL, D_MODEL), lambda i, pos: (0, 0), pipeline_mode=once),
            pl.BlockSpec((ple, D_MODEL), lambda i, pos: (0, 0), pipeline_mode=once),
            pl.BlockSpec((1, D_MODEL), lambda i, pos: (0, 0), pipeline_mode=once),
        ],
        out_specs=[
            pl.BlockSpec((TM, D_MODEL), lambda i, pos: (pidx(i), 0)),
            pl.BlockSpec((TM, D_MODEL), lambda i, pos: (sidx(i), 0)),
        ],
        scratch_shapes=[
            pltpu.VMEM((2, 2, TM * ROWS, LANES), PACKED),
            pltpu.SemaphoreType.DMA((2,)),
            pltpu.VMEM((TM, D_MODEL), F32),
            pltpu.VMEM((TM, D_MODEL), BF16),
            pltpu.VMEM((TM, D_MODEL), F32),
        ],
    )
    return pl.pallas_call(
        functools.partial(_tail_kernel, n_prompt_tiles=npt, n_tiles=npt + nst),
        grid_spec=grid_spec,
        out_shape=[jax.ShapeDtypeStruct((tp, D_MODEL), F32), jax.ShapeDtypeStruct((ts, D_MODEL), F32)],
        compiler_params=pltpu.CompilerParams(
            dimension_semantics=("arbitrary",), vmem_limit_bytes=VMEM_LIMIT),
        name="tail",
    )(pos, ys, h, route, pp, ps, gp, wpg, wpp, gf)


def _route_tables(route, n_tiles):
    e = route[:, 0:2].astype(jnp.int32).reshape(-1)
    onehot = (e[:, None] == jnp.arange(N_EXPERTS, dtype=jnp.int32)[None, :]).astype(jnp.int32)
    cum = jnp.cumsum(onehot, axis=0)
    counts = cum[-1]
    rank = jnp.sum((cum - onehot) * onehot, axis=1)
    padded = ((counts + TM_E - 1) // TM_E) * TM_E
    ends = jnp.cumsum(padded)
    offs = ends - padded
    pos = jnp.sum(onehot * offs[None, :], axis=1) + rank
    nused = ends[-1] // TM_E
    tile_id = jnp.arange(n_tiles, dtype=jnp.int32)
    te = jnp.sum((tile_id[:, None] * TM_E >= ends[None, :]).astype(jnp.int32), axis=1)
    te_last = jnp.sum((((nused - 1) * TM_E) >= ends).astype(jnp.int32))
    te = jnp.where(tile_id < nused, te, te_last)
    i32 = lambda a: a.astype(jnp.int32)
    return i32(te), i32(nused).reshape(1), i32(pos), i32(counts), i32(offs)


def kernel(x_prompt, x_sample, state_conv, p_prompt, p_sample, g_mix, w_in, ln_v_g, ln_v_b, w_spatial, b_spatial, conv_w, w_br_a, w_br_b, w_out, g_moe, w_router_group, b_router_group, w_router_expert, b_router_expert, w_exp_gate, w_exp_up, w_exp_down, g_ple, w_ple_gate, w_ple_proj, g_final):
    depth = w_in.shape[0]
    assert depth == 1, "single-layer step only"
    nb, seq, _ = x_prompt.shape
    ns, dseq, _ = x_sample.shape
    assert dseq == SAMPLE_SEQ and seq % TM == 0 and (ns * dseq) % TM_NORM == 0 and (nb * seq) % TM_NORM == 0
    assert (nb * seq + ns * dseq) % TM_IN == 0
    tp, ts = nb * seq, ns * dseq
    t = tp + ts
    xp = x_prompt.reshape(tp, D_MODEL)
    xs = x_sample.reshape(ts, D_MODEL)

    row = lambda a: a.reshape(1, -1)
    tril = jnp.tril(jnp.ones((CHUNK, CHUNK), dtype=bool))
    w_sp = jnp.where(tril[None], w_spatial[0], 0.0)
    eye = jnp.eye(CHUNK // dseq, dtype=F32)
    w_sp_s = jnp.einsum("ij,gts->gitjs", eye, w_sp[:, :dseq, :dseq]).reshape(G_A, CHUNK, CHUNK)
    wsp = jnp.stack([w_sp, w_sp_s]).astype(BF16)
    b_p = jnp.repeat(b_spatial[0].T, D_GROUP_A, axis=1)
    b_s = jnp.tile(b_p[:dseq], (CHUNK // dseq, 1))
    bsp = jnp.stack([b_p, b_s])
    lane_pad = LANES - N_GROUPS - N_EXPERTS
    wr = jnp.concatenate([w_router_group[0], w_router_expert[0], jnp.zeros((D_MODEL, lane_pad), F32)], axis=1)
    br = jnp.concatenate([b_router_group[0], b_router_expert[0], jnp.zeros((lane_pad,), F32)]).reshape(1, LANES)
    f = jnp.pad(state_conv[0], ((0, 0), (0, dseq - (CONV_W - 1)), (0, 0))).reshape(ts, D_CONV)

    n = _norm(xp, xs, row(g_mix[0]))
    z, (wa, wb, wo, wpg) = _inproj(n, w_in, [w_br_a, w_br_b, w_out, w_ple_gate])
    h, xm, route, vnp, vns, qp, qs = _mixer(
        z, xp, xs, f, row(ln_v_g[0]), row(ln_v_b[0]), conv_w[0], wsp, bsp, wa, wb, wo,
        row(g_moe[0]), wr.astype(BF16), br, n_batch=nb)

    n_tiles = (2 * t) // TM_E + N_EXPERTS
    te, nused, pos, counts, offs = _route_tables(route, n_tiles)
    xsort = _dispatch(pos, counts, offs, nused, xm, n_tiles=n_tiles)
    ys = _experts(te, nused, xsort, w_exp_gate, w_exp_up, w_exp_down, n_tiles=n_tiles)

    yp, ysm = _tail(pos, ys, h, route, p_prompt[0].reshape(tp, -1), p_sample[0].reshape(ts, -1),
                    row(g_ple[0]), wpg, w_ple_proj[0].astype(BF16), row(g_final))

    y_prompt = yp.reshape(nb, seq, D_MODEL)
    y_sample = ysm.reshape(ns, dseq, D_MODEL)
    conv_state_prompt = qp[:nb][None]
    conv_state_sample = qs.reshape(ns, dseq, D_CONV)[:, dseq - (CONV_W - 1):][None]
    v_rows_prompt = vnp[:nb][None]
    v_rows_sample = vns.reshape(ns, dseq, D_HALF)[None]
    return (y_prompt, y_sample, conv_state_prompt, conv_state_sample, v_rows_prompt, v_rows_sample)
```

```python
import functools

import jax
import jax.numpy as jnp
from jax import lax
from jax.experimental import pallas as pl
from jax.experimental.pallas import tpu as pltpu

D_MODEL = 2048
D_HALF = D_MODEL // 2
D_CONV = D_MODEL // 2
G_A = 4
D_GROUP_A = D_HALF // G_A
CHUNK = 128
CONV_W = 3
N_GROUPS = 4
EXPERTS_PER_GROUP = 8
N_EXPERTS = N_GROUPS * EXPERTS_PER_GROUP
D_EXPERT = D_MODEL // 4
D_IN_TOTAL = 2 * D_HALF + 3 * D_CONV + 2 * D_MODEL
EPS = 1e-6

BF16 = jnp.bfloat16
F32 = jnp.float32

LANES = 128
VMEM_LIMIT = 60 * 1024 * 1024

TM_NORM = 512
TM_IN = 1024
CAST_ROWS = 128
TN_IN = 1024
TM = 256
TM_E = 256
ROWS = D_MODEL // (2 * LANES)
PACKED = jnp.uint32
COPY_TILE = 256
DISPATCH_SLOTS = 3
TAIL_CHUNKS = 8
MERGE_CHUNKS = 8
SAMPLE_SEQ = 8
RANK_BLOCK = 256
TAIL_SUB = 2
TB = TAIL_SUB * TM


def _rms(x, g):
    return x * lax.rsqrt(jnp.mean(x * x, axis=-1, keepdims=True) + EPS) * g


def _sigmoid(x):
    return 1.0 / (1.0 + jnp.exp(-x))


def _store_token_major(ref, x):
    m = x.shape[0]
    for s in range(ROWS):
        lo = x[:, s * LANES:(s + 1) * LANES]
        hi = x[:, (ROWS + s) * LANES:(ROWS + s + 1) * LANES]
        ref[pl.ds(s, m, stride=ROWS), :] = pltpu.pack_elementwise([lo, hi], packed_dtype=BF16)


def _packed_zeros(shape):
    zero = jnp.zeros(shape, F32)
    return pltpu.pack_elementwise([zero, zero], packed_dtype=BF16)


def _unpack_words(words):
    half = lambda idx: [pltpu.unpack_elementwise(w, index=idx, packed_dtype=BF16, unpacked_dtype=F32)
                        for w in words]
    return jnp.concatenate(half(0) + half(1), axis=1)


def _load_token_major(ref, m):
    return _unpack_words([ref[pl.ds(s, m, stride=ROWS), :] for s in range(ROWS)])


def _norm_kernel(xp_ref, xs_ref, g_ref, o_ref, *, n_prompt_tiles):
    i = pl.program_id(0)

    def run(x_ref):
        o_ref[...] = _rms(x_ref[...], g_ref[...]).astype(BF16)

    @pl.when(i < n_prompt_tiles)
    def _():
        run(xp_ref)

    @pl.when(i >= n_prompt_tiles)
    def _():
        run(xs_ref)


def _norm(xp, xs, g):
    tp, ts = xp.shape[0], xs.shape[0]
    npt, nst = tp // TM_NORM, ts // TM_NORM
    return pl.pallas_call(
        functools.partial(_norm_kernel, n_prompt_tiles=npt),
        grid=(npt + nst,),
        in_specs=[
            pl.BlockSpec((TM_NORM, D_MODEL), lambda i: (jnp.minimum(i, npt - 1), 0)),
            pl.BlockSpec((TM_NORM, D_MODEL), lambda i: (jnp.maximum(i - npt, 0), 0)),
            pl.BlockSpec((1, D_MODEL), lambda i: (0, 0)),
        ],
        out_specs=pl.BlockSpec((TM_NORM, D_MODEL), lambda i: (i, 0)),
        out_shape=jax.ShapeDtypeStruct((tp + ts, D_MODEL), BF16),
        compiler_params=pltpu.CompilerParams(
            dimension_semantics=("arbitrary",), vmem_limit_bytes=VMEM_LIMIT),
        name="norm",
    )(xp, xs, g)


def _inproj_kernel(n_ref, w_ref, *refs, n_side):
    side_in, z_ref, side_out = refs[:n_side], refs[n_side], refs[n_side + 1:]

    z_ref[...] = jnp.dot(n_ref[...], w_ref[...].astype(BF16), preferred_element_type=F32).astype(BF16)
    for src, dst in zip(side_in, side_out):
        dst[...] = src[...].astype(BF16)


def _inproj(n, w_in, side_weights):
    t = n.shape[0]
    n_j, n_i = D_IN_TOTAL // TN_IN, t // TM_IN
    starts, start = [], 0
    for w in side_weights:
        assert w.shape[1] % CAST_ROWS == 0
        starts.append(start)
        start += w.shape[1] // CAST_ROWS
    assert start <= n_j * n_i, "not enough grid steps for the weight casts"

    def side_index(k):
        n_blk = side_weights[k].shape[1] // CAST_ROWS
        return lambda j, i: jnp.clip(j * n_i + i - starts[k], 0, n_blk - 1)

    side_in = [pl.BlockSpec((None, CAST_ROWS, w.shape[2]), lambda j, i, f=side_index(k): (0, f(j, i), 0))
               for k, w in enumerate(side_weights)]
    side_out = [pl.BlockSpec((CAST_ROWS, w.shape[2]), lambda j, i, f=side_index(k): (f(j, i), 0))
                for k, w in enumerate(side_weights)]
    outs = pl.pallas_call(
        functools.partial(_inproj_kernel, n_side=len(side_weights)),
        grid=(n_j, n_i),
        in_specs=[
            pl.BlockSpec((TM_IN, D_MODEL), lambda j, i: (i, 0)),
            pl.BlockSpec((None, D_MODEL, TN_IN), lambda j, i: (0, 0, j)),
        ] + side_in,
        out_specs=[pl.BlockSpec((TM_IN, TN_IN), lambda j, i: (i, j))] + side_out,
        out_shape=[jax.ShapeDtypeStruct((t, D_IN_TOTAL), BF16)]
        + [jax.ShapeDtypeStruct(w.shape[1:], BF16) for w in side_weights],
        compiler_params=pltpu.CompilerParams(
            dimension_semantics=("arbitrary", "arbitrary"), vmem_limit_bytes=VMEM_LIMIT),
        name="in_proj",
    )(n, w_in, *side_weights)
    return outs[0], outs[1:]


def _route(lg):
    shape = lg.shape
    lane = lax.broadcasted_iota(jnp.int32, shape, 1)
    lanef = lane.astype(F32)
    neg = jnp.float32(-jnp.inf)
    far = jnp.float32(LANES)
    is_g = lane < N_GROUPS
    lgm = jnp.where(is_g, lg, neg)
    gmax = jnp.max(lgm, axis=-1, keepdims=True)
    gsel = jnp.min(jnp.where(lgm == gmax, lanef, far), axis=-1, keepdims=True)
    p_group = 1.0 / jnp.sum(jnp.where(is_g, jnp.exp(lgm - gmax), 0.0), axis=-1, keepdims=True)
    lane_group = ((lane - N_GROUPS) >> (EXPERTS_PER_GROUP.bit_length() - 1)).astype(F32)
    is_e = lane_group == gsel
    le = jnp.where(is_e, lg, neg)
    m1 = jnp.max(le, axis=-1, keepdims=True)
    i1 = jnp.min(jnp.where(le == m1, lanef, far), axis=-1, keepdims=True)
    le2 = jnp.where(lanef == i1, neg, le)
    m2 = jnp.max(le2, axis=-1, keepdims=True)
    i2 = jnp.min(jnp.where(le2 == m2, lanef, far), axis=-1, keepdims=True)
    e21 = jnp.exp(m2 - m1)
    w1 = 1.0 / (1.0 + e21)
    w2 = e21 * w1
    out = jnp.where(lane == 0, i1 - N_GROUPS,
                    jnp.where(lane == 1, i2 - N_GROUPS,
                              jnp.where(lane == 2, p_group * w1,
                                        jnp.where(lane == 3, p_group * w2, 0.0))))
    return out


def _mixer_kernel(z_ref, xp_ref, xs_ref, f_ref, lng_ref, lnb_ref, cw_ref, wsp_ref, bsp_ref,
                  wa_ref, wb_ref, wo_ref, gm_ref, wr_ref, br_ref,
                  h_ref, xm_ref, route_ref, vnp_ref, vns_ref, qp_ref, qs_ref,
                  qprev_ref, ya_ref, yb_ref, m_ref, *, n_prompt_tiles, tiles_per_seq):
    i = pl.program_id(0)
    is_prompt = i < n_prompt_tiles
    c_u, c_v, c_b, c_c, c_x = (k * D_HALF for k in range(5))
    c_ga = c_x + D_CONV
    c_gb = c_ga + D_MODEL

    v = z_ref[:, c_v:c_v + D_HALF].astype(F32)
    mu = jnp.mean(v, axis=-1, keepdims=True)
    vc = v - mu
    vn = vc * lax.rsqrt(jnp.mean(vc * vc, axis=-1, keepdims=True) + EPS) * lng_ref[...] + lnb_ref[...]
    vnb = vn.astype(BF16)
    for c in range(TM // CHUNK):
        rows = slice(c * CHUNK, (c + 1) * CHUNK)
        for g in range(G_A):
            cols = slice(g * D_GROUP_A, (g + 1) * D_GROUP_A)
            s = jnp.dot(wsp_ref[g], vnb[rows, cols], preferred_element_type=F32) + bsp_ref[:, cols]
            u = z_ref[rows, c_u + g * D_GROUP_A:c_u + (g + 1) * D_GROUP_A].astype(F32)
            ya_ref[rows, cols] = (u * s).astype(BF16)

    q = z_ref[:, c_c:c_c + D_CONV].astype(F32) * z_ref[:, c_x:c_x + D_CONV].astype(F32)
    q1 = pltpu.roll(q, 1, 0)
    q2 = pltpu.roll(q, 2, 0)
    row = lax.broadcasted_iota(jnp.int32, q.shape, 0)
    cw0, cw1, cw2 = cw_ref[0:1, :], cw_ref[1:2, :], cw_ref[2:3, :]

    seq_start = i % tiles_per_seq == 0
    p6 = jnp.where(seq_start, 0.0, qprev_ref[6:7, :])
    p7 = jnp.where(seq_start, 0.0, qprev_ref[7:8, :])
    f = f_ref[...]
    pos_in_seq = jnp.where(is_prompt, row, row & (SAMPLE_SEQ - 1))
    fix1 = jnp.where(is_prompt, jnp.broadcast_to(p7, q.shape), pltpu.roll(f, TM - 1, 0))
    fix2 = jnp.where(is_prompt, jnp.where(row == 0, p6, p7), f)
    a1 = jnp.where(pos_in_seq == 0, fix1, q1)
    a2 = jnp.where(pos_in_seq < 2, fix2, q2)
    yc = cw0 * a2 + cw1 * a1 + cw2 * q
    qprev_ref[...] = q[TM - 8:TM, :]
    qp_ref[...] = q[TM - (CONV_W - 1):TM, :]
    vnp_ref[...] = vn[TM - CHUNK:TM, :]
    qs_ref[...] = q
    vns_ref[...] = vn

    yb_ref[...] = (z_ref[:, c_b:c_b + D_CONV].astype(F32) * yc).astype(BF16)

    width = D_MODEL // MERGE_CHUNKS
    for c in range(MERGE_CHUNKS):
        cols = slice(c * width, (c + 1) * width)
        a = jnp.dot(ya_ref[...], wa_ref[:, cols], preferred_element_type=F32)
        b = jnp.dot(yb_ref[...], wb_ref[:, cols], preferred_element_type=F32)
        ga = z_ref[:, c_ga + c * width:c_ga + (c + 1) * width].astype(F32)
        gb = z_ref[:, c_gb + c * width:c_gb + (c + 1) * width].astype(F32)
        m_ref[:, cols] = (_sigmoid(ga) * a + _sigmoid(gb) * b).astype(BF16)

    x = jnp.where(is_prompt, xp_ref[...], xs_ref[...])
    h = x + jnp.dot(m_ref[...], wo_ref[...], preferred_element_type=F32)
    h_ref[...] = h
    xm = _rms(h, gm_ref[...])
    _store_token_major(xm_ref, xm)
    lg = jnp.dot(xm.astype(BF16), wr_ref[...], preferred_element_type=F32) + br_ref[...]
    route_ref[...] = _route(lg)


def _mixer(z, xp, xs, f, lng, lnb, cw, wsp, bsp, wa, wb, wo, gm, wr, br, *, n_batch):
    tp, ts = xp.shape[0], xs.shape[0]
    t = tp + ts
    npt, nst = tp // TM, ts // TM
    tiles_per_seq = tp // n_batch // TM
    once = pl.Buffered(1)

    def pidx(i):
        return jnp.minimum(i, npt - 1)

    def sidx(i):
        return jnp.maximum(i - npt, 0)

    def seq_or_spare(i):
        return jnp.where(i < npt, i // tiles_per_seq, n_batch)

    const = lambda shape: pl.BlockSpec(shape, lambda i: (0,) * len(shape), pipeline_mode=once)
    in_specs = [
        pl.BlockSpec((TM, D_IN_TOTAL), lambda i: (i, 0)),
        pl.BlockSpec((TM, D_MODEL), lambda i: (pidx(i), 0)),
        pl.BlockSpec((TM, D_MODEL), lambda i: (sidx(i), 0)),
        pl.BlockSpec((TM, D_CONV), lambda i: (sidx(i), 0)),
        const((1, D_HALF)), const((1, D_HALF)), const((CONV_W, D_CONV)),
        pl.BlockSpec((None, G_A, CHUNK, CHUNK), lambda i: (jnp.where(i < npt, 0, 1), 0, 0, 0)),
        pl.BlockSpec((None, CHUNK, D_HALF), lambda i: (jnp.where(i < npt, 0, 1), 0, 0)),
        const((D_HALF, D_MODEL)), const((D_CONV, D_MODEL)), const((D_MODEL, D_MODEL)),
        const((1, D_MODEL)), const((D_MODEL, LANES)), const((1, LANES)),
    ]
    out_specs = [
        pl.BlockSpec((TM, D_MODEL), lambda i: (i, 0)),
        pl.BlockSpec((TM * ROWS, LANES), lambda i: (i, 0)),
        pl.BlockSpec((TM, LANES), lambda i: (i, 0)),
        pl.BlockSpec((None, CHUNK, D_HALF), lambda i: (seq_or_spare(i), 0, 0)),
        pl.BlockSpec((TM, D_HALF), lambda i: (sidx(i), 0)),
        pl.BlockSpec((None, CONV_W - 1, D_CONV), lambda i: (seq_or_spare(i), 0, 0)),
        pl.BlockSpec((TM, D_CONV), lambda i: (sidx(i), 0)),
    ]
    out_shape = [
        jax.ShapeDtypeStruct((t, D_MODEL), F32),
        jax.ShapeDtypeStruct((t * ROWS, LANES), PACKED),
        jax.ShapeDtypeStruct((t, LANES), F32),
        jax.ShapeDtypeStruct((n_batch + 1, CHUNK, D_HALF), F32),
        jax.ShapeDtypeStruct((ts, D_HALF), F32),
        jax.ShapeDtypeStruct((n_batch + 1, CONV_W - 1, D_CONV), F32),
        jax.ShapeDtypeStruct((ts, D_CONV), F32),
    ]
    return pl.pallas_call(
        functools.partial(_mixer_kernel, n_prompt_tiles=npt, tiles_per_seq=tiles_per_seq),
        grid=(npt + nst,),
        in_specs=in_specs,
        out_specs=out_specs,
        out_shape=out_shape,
        scratch_shapes=[
            pltpu.VMEM((8, D_CONV), F32),
            pltpu.VMEM((TM, D_HALF), BF16),
            pltpu.VMEM((TM, D_CONV), BF16),
            pltpu.VMEM((TM, D_MODEL), BF16),
        ],
        compiler_params=pltpu.CompilerParams(
            dimension_semantics=("arbitrary",), vmem_limit_bytes=VMEM_LIMIT),
        name="mixer",
    )(z, xp, xs, f, lng, lnb, cw, wsp, bsp, wa, wb, wo, gm, wr, br)


def _dispatch_kernel(pos_ref, cnt_ref, off_ref, nused_ref, xm_hbm, xs_hbm, inbuf, zbuf, isem, rsem, zsem,
                     *, n_steps, n_tiles):
    zbuf[...] = _packed_zeros(zbuf.shape)
    nused = nused_ref[0]
    pad_bits = TM_E.bit_length() - 1

    def pad_copies(e, fn):
        cnt = cnt_ref[e]
        npad = (TM_E - cnt % TM_E) % TM_E
        start = off_ref[e] + cnt
        for b in range(pad_bits):
            size = 1 << b
            at = start + ((npad >> (b + 1)) << (b + 1))

            @pl.when(((npad >> b) & 1) == 1)
            def _():
                fn(pltpu.make_async_copy(zbuf.at[pl.ds(0, size * ROWS), :],
                                         xs_hbm.at[pl.ds(at * ROWS, size * ROWS), :], zsem))

    def tile_copy(tile):
        return pltpu.make_async_copy(zbuf, xs_hbm.at[pl.ds(tile * TM_E * ROWS, TM_E * ROWS), :], zsem)

    def loop(lo, hi, body):
        lax.fori_loop(lo, hi, lambda k, c: (body(k), c)[1], 0)

    loop(0, N_EXPERTS, lambda e: pad_copies(e, lambda cp: cp.start()))
    loop(nused, n_tiles, lambda tile: tile_copy(tile).start())

    def in_copy(step, slot):
        return pltpu.make_async_copy(xm_hbm.at[pl.ds(step * COPY_TILE * ROWS, COPY_TILE * ROWS), :],
                                     inbuf.at[slot], isem.at[slot])

    def rows_wait(slot):
        for _ in range(2):
            pltpu.make_async_copy(inbuf.at[slot], xs_hbm.at[pl.ds(0, COPY_TILE * ROWS), :], rsem.at[slot]).wait()

    in_copy(0, 0).start()

    def step(s):
        slot = s % DISPATCH_SLOTS
        nxt = (s + 1) % DISPATCH_SLOTS

        @pl.when(s >= DISPATCH_SLOTS - 1)
        def _():
            rows_wait(nxt)

        @pl.when(s + 1 < n_steps)
        def _():
            in_copy(s + 1, nxt).start()

        in_copy(s, slot).wait()
        base = s * COPY_TILE * 2

        def body(r, carry):
            for k in range(2):
                p = pos_ref[base + 2 * r + k]
                pltpu.make_async_copy(inbuf.at[slot, pl.ds(r * ROWS, ROWS), :],
                                      xs_hbm.at[pl.ds(p * ROWS, ROWS), :], rsem.at[slot]).start(priority=k)
            return carry

        lax.fori_loop(0, COPY_TILE, body, 0, unroll=4)

    loop(0, n_steps, step)
    for back in range(DISPATCH_SLOTS - 1, 0, -1):
        rows_wait((n_steps - back) % DISPATCH_SLOTS)
    loop(0, N_EXPERTS, lambda e: pad_copies(e, lambda cp: cp.wait()))
    loop(nused, n_tiles, lambda tile: tile_copy(tile).wait())


def _dispatch(pos, cnt, off, nused, xm, *, n_tiles):
    n_tokens = xm.shape[0] // ROWS
    assert n_tokens % COPY_TILE == 0 and n_tokens // COPY_TILE >= DISPATCH_SLOTS
    smem = pl.BlockSpec(memory_space=pltpu.SMEM)
    return pl.pallas_call(
        functools.partial(_dispatch_kernel, n_steps=n_tokens // COPY_TILE, n_tiles=n_tiles),
        in_specs=[smem, smem, smem, smem, pl.BlockSpec(memory_space=pl.ANY)],
        out_specs=pl.BlockSpec(memory_space=pl.ANY),
        out_shape=jax.ShapeDtypeStruct((n_tiles * TM_E * ROWS, LANES), PACKED),
        scratch_shapes=[
            pltpu.VMEM((DISPATCH_SLOTS, COPY_TILE * ROWS, LANES), PACKED),
            pltpu.VMEM((TM_E * ROWS, LANES), PACKED),
            pltpu.SemaphoreType.DMA((DISPATCH_SLOTS,)),
            pltpu.SemaphoreType.DMA((DISPATCH_SLOTS,)),
            pltpu.SemaphoreType.DMA(()),
        ],
        compiler_params=pltpu.CompilerParams(vmem_limit_bytes=VMEM_LIMIT),
        name="dispatch",
    )(pos, cnt, off, nused, xm)


def _expert_kernel(te_ref, nused_ref, xs_ref, wg_hbm, wu_hbm, wd_hbm, y_ref,
                   wgf, wuf, wdf, wsem, seen_ref):
    i = pl.program_id(0)
    nused = nused_ref[0]

    def weight_copies(e, slot):
        return [pltpu.make_async_copy(src.at[0, e], dst.at[slot], wsem.at[slot, k])
                for k, (src, dst) in enumerate(((wg_hbm, wgf), (wu_hbm, wuf), (wd_hbm, wdf)))]

    @pl.when(i == 0)
    def _():
        seen_ref[0] = 0
        for cp in weight_copies(te_ref[0], 0):
            cp.start()

    @pl.when(i < nused)
    def _():
        e = te_ref[i]
        new_expert = jnp.logical_or(i == 0, e != te_ref[jnp.maximum(i - 1, 0)])

        @pl.when(new_expert)
        def _():
            slot = seen_ref[0] % 2
            seen_ref[0] = seen_ref[0] + 1
            seen_ref[1] = slot
            nxt = lax.while_loop(lambda j: jnp.logical_and(j < nused, te_ref[jnp.minimum(j, nused - 1)] == e),
                                 lambda j: j + 1, i + 1)

            @pl.when(nxt < nused)
            def _():
                for cp in weight_copies(te_ref[jnp.minimum(nxt, nused - 1)], 1 - slot):
                    cp.start(priority=1)

            for cp in weight_copies(e, slot):
                cp.wait()

        cur = seen_ref[1]
        x = _load_token_major(xs_ref, TM_E).astype(BF16)
        g = jnp.dot(x, wgf[cur].astype(BF16), preferred_element_type=F32)
        u = jnp.dot(x, wuf[cur].astype(BF16), preferred_element_type=F32)
        hid = (g * _sigmoid(g)) * u
        _store_token_major(y_ref, jnp.dot(hid.astype(BF16), wdf[cur].astype(BF16), preferred_element_type=F32))

    @pl.when(i >= nused)
    def _():
        y_ref[...] = _packed_zeros(y_ref.shape)


def _experts(te, nused, xs, wg, wu, wd, *, n_tiles):
    grid_spec = pltpu.PrefetchScalarGridSpec(
        num_scalar_prefetch=2,
        grid=(n_tiles,),
        in_specs=[
            pl.BlockSpec((TM_E * ROWS, LANES), lambda i, te, nu: (jnp.minimum(i, nu[0] - 1), 0)),
            pl.BlockSpec(memory_space=pl.ANY),
            pl.BlockSpec(memory_space=pl.ANY),
            pl.BlockSpec(memory_space=pl.ANY),
        ],
        out_specs=pl.BlockSpec((TM_E * ROWS, LANES), lambda i, te, nu: (i, 0)),
        scratch_shapes=[
            pltpu.VMEM((2, D_MODEL, D_EXPERT), F32),
            pltpu.VMEM((2, D_MODEL, D_EXPERT), F32),
            pltpu.VMEM((2, D_EXPERT, D_MODEL), F32),
            pltpu.SemaphoreType.DMA((2, 3)),
            pltpu.SMEM((2,), jnp.int32),
        ],
    )
    return pl.pallas_call(
        _expert_kernel,
        grid_spec=grid_spec,
        out_shape=jax.ShapeDtypeStruct((n_tiles * TM_E * ROWS, LANES), PACKED),
        compiler_params=pltpu.CompilerParams(
            dimension_semantics=("arbitrary",), vmem_limit_bytes=VMEM_LIMIT),
        name="experts",
    )(te, nused, xs, wg, wu, wd)


def _tail_kernel(pos_ref, ys_hbm, h_ref, route_ref, pp_ref, ps_ref, gp_ref, wpg_ref, wpp_ref, gf_ref,
                 yp_ref, ysm_ref, ybuf, sem, h2_ref, a_ref, proj_ref, *, n_prompt_tiles, n_tiles):
    i = pl.program_id(0)
    slot = i % 2
    is_prompt = i < n_prompt_tiles

    def issue(tile, dst_slot):
        base = tile * TB * 2

        def body(r, carry):
            for k in range(2):
                p = pos_ref[base + 2 * r + k]
                pltpu.make_async_copy(ys_hbm.at[pl.ds(p * ROWS, ROWS), :],
                                      ybuf.at[dst_slot, k, pl.ds(r * ROWS, ROWS), :],
                                      sem.at[dst_slot]).start()
            return carry

        lax.fori_loop(0, TB, body, 0, unroll=4)

    def slot_wait(s):
        for k in range(2):
            pltpu.make_async_copy(ys_hbm.at[pl.ds(0, TB * ROWS), :], ybuf.at[s, k], sem.at[s]).wait()

    def row_sumsq(acc):
        return jnp.sum(acc, axis=-1, keepdims=True) * (1.0 / D_MODEL)

    @pl.when(i == 0)
    def _():
        issue(0, 0)

    slot_wait(slot)

    nxt_base = jnp.minimum(i + 1, n_tiles - 1) * TB * 2
    rows_per_chunk = TB // TAIL_CHUNKS
    width = D_MODEL // TAIL_CHUNKS
    scales = []
    for sub in range(TAIL_SUB):
        rows = pl.ds(sub * TM, TM)
        w0 = jnp.broadcast_to(route_ref[rows, 2:3], (TM, LANES))
        w1 = jnp.broadcast_to(route_ref[rows, 3:4], (TM, LANES))
        acc = jnp.zeros((TM, LANES), F32)
        for s in range(ROWS):
            word0 = ybuf[slot, 0, pl.ds(sub * TM * ROWS + s, TM, stride=ROWS), :]
            word1 = ybuf[slot, 1, pl.ds(sub * TM * ROWS + s, TM, stride=ROWS), :]
            for idx in range(2):
                cols = slice((idx * ROWS + s) * LANES, (idx * ROWS + s + 1) * LANES)
                y0 = pltpu.unpack_elementwise(word0, index=idx, packed_dtype=BF16, unpacked_dtype=F32)
                y1 = pltpu.unpack_elementwise(word1, index=idx, packed_dtype=BF16, unpacked_dtype=F32)
                h2 = h_ref[rows, cols] + (w0 * y0 + w1 * y1)
                h2_ref[rows, cols] = h2
                acc = acc + h2 * h2
        r = lax.rsqrt(row_sumsq(acc) + EPS)
        a_ref[rows, :] = (h2_ref[rows, :] * r * gp_ref[...]).astype(BF16)
        p = jnp.where(is_prompt, pp_ref[rows, :], ps_ref[rows, :]).astype(BF16)
        proj_ref[rows, :] = jnp.dot(p, wpp_ref[...], preferred_element_type=F32)

        acc = jnp.zeros((TM, LANES), F32)
        for c in range(TAIL_CHUNKS):
            for rr in range(rows_per_chunk if sub == TAIL_SUB - 1 else 0):
                row = c * rows_per_chunk + rr
                for k in range(2):
                    p_row = pos_ref[nxt_base + 2 * row + k]
                    pltpu.make_async_copy(ys_hbm.at[pl.ds(p_row * ROWS, ROWS), :],
                                          ybuf.at[1 - slot, k, pl.ds(row * ROWS, ROWS), :],
                                          sem.at[1 - slot]).start()
            cols = slice(c * width, (c + 1) * width)
            gate = _sigmoid(jnp.dot(a_ref[rows, :], wpg_ref[:, cols], preferred_element_type=F32))
            h3 = h2_ref[rows, cols] + gate * proj_ref[rows, cols]
            h2_ref[rows, cols] = h3
            for j in range(width // LANES):
                part = h3[:, j * LANES:(j + 1) * LANES]
                acc = acc + part * part
        scales.append(lax.rsqrt(row_sumsq(acc) + EPS))

    def write_out(o_ref):
        for sub in range(TAIL_SUB):
            rows = pl.ds(sub * TM, TM)
            o_ref[rows, :] = h2_ref[rows, :] * scales[sub] * gf_ref[...]

    @pl.when(is_prompt)
    def _():
        write_out(yp_ref)

    @pl.when(jnp.logical_not(is_prompt))
    def _():
        write_out(ysm_ref)

    @pl.when(i == n_tiles - 1)
    def _():
        slot_wait(1 - slot)


def _tail(pos, ys, h, route, pp, ps, gp, wpg, wpp, gf):
    tp, ts = pp.shape[0], ps.shape[0]
    assert tp % TB == 0 and ts % TB == 0
    npt, nst = tp // TB, ts // TB
    ple = pp.shape[1]
    once = pl.Buffered(1)

    def pidx(i):
        return jnp.minimum(i, npt - 1)

    def sidx(i):
        return jnp.maximum(i - npt, 0)

    grid_spec = pltpu.PrefetchScalarGridSpec(
        num_scalar_prefetch=1,
        grid=(npt + nst,),
        in_specs=[
            pl.BlockSpec(memory_space=pl.ANY),
            pl.BlockSpec((TB, D_MODEL), lambda i, pos: (i, 0)),
            pl.BlockSpec((TB, LANES), lambda i, pos: (i, 0)),
            pl.BlockSpec((TB, ple), lambda i, pos: (pidx(i), 0)),
            pl.BlockSpec((TB, ple), lambda i, pos: (sidx(i), 0)),
            pl.BlockSpec((1, D_MODEL), lambda i, pos: (0, 0), pipeline_mode=once),
            pl.BlockSpec((D_MODEL, D_MODEL), lambda i, pos: (0, 0), pipeline_mode=once),
            pl.BlockSpec((ple, D_MODEL), lambda i, pos: (0, 0), pipeline_mode=once),
            pl.BlockSpec((1, D_MODEL), lambda i, pos: (0, 0), pipeline_mode=once),
        ],
        out_specs=[
            pl.BlockSpec((TB, D_MODEL), lambda i, pos: (pidx(i), 0)),
            pl.BlockSpec((TB, D_MODEL), lambda i, pos: (sidx(i), 0)),
        ],
        scratch_shapes=[
            pltpu.VMEM((2, 2, TB * ROWS, LANES), PACKED),
            pltpu.SemaphoreType.DMA((2,)),
            pltpu.VMEM((TB, D_MODEL), F32),
            pltpu.VMEM((TB, D_MODEL), BF16),
            pltpu.VMEM((TB, D_MODEL), F32),
        ],
    )
    return pl.pallas_call(
        functools.partial(_tail_kernel, n_prompt_tiles=npt, n_tiles=npt + nst),
        grid_spec=grid_spec,
        out_shape=[jax.ShapeDtypeStruct((tp, D_MODEL), F32), jax.ShapeDtypeStruct((ts, D_MODEL), F32)],
        compiler_params=pltpu.CompilerParams(
            dimension_semantics=("arbitrary",), vmem_limit_bytes=VMEM_LIMIT),
        name="tail",
    )(pos, ys, h, route, pp, ps, gp, wpg, wpp, gf)


def _route_tables(route, n_tiles):
    e = route[:, 0:2].astype(jnp.int32).reshape(-1)
    n_blocks = e.shape[0] // RANK_BLOCK
    onehot = (e[:, None] == jnp.arange(N_EXPERTS, dtype=jnp.int32)[None, :]).astype(BF16)
    onehot = onehot.reshape(n_blocks, RANK_BLOCK, N_EXPERTS)
    ltri = jnp.tril(jnp.ones((RANK_BLOCK, RANK_BLOCK), BF16), -1)
    within = jnp.einsum("ij,bjk->bik", ltri, onehot, preferred_element_type=F32)
    block_counts = jnp.sum(onehot.astype(F32), axis=1)
    before = jnp.cumsum(block_counts, axis=0) - block_counts
    counts = jnp.sum(block_counts, axis=0).astype(jnp.int32)
    padded = ((counts + TM_E - 1) // TM_E) * TM_E
    ends = jnp.cumsum(padded)
    offs = ends - padded
    pos = jnp.sum((within + before[:, None, :] + offs[None, None, :].astype(F32)) * onehot.astype(F32), axis=-1)
    pos = pos.reshape(-1)
    nused = ends[-1] // TM_E
    tile_id = jnp.arange(n_tiles, dtype=jnp.int32)
    te = jnp.sum((tile_id[:, None] * TM_E >= ends[None, :]).astype(jnp.int32), axis=1)
    te_last = jnp.sum((((nused - 1) * TM_E) >= ends).astype(jnp.int32))
    te = jnp.where(tile_id < nused, te, te_last)
    i32 = lambda a: a.astype(jnp.int32)
    return i32(te), i32(nused).reshape(1), i32(pos), i32(counts), i32(offs)


def kernel(x_prompt, x_sample, state_conv, p_prompt, p_sample, g_mix, w_in, ln_v_g, ln_v_b, w_spatial, b_spatial, conv_w, w_br_a, w_br_b, w_out, g_moe, w_router_group, b_router_group, w_router_expert, b_router_expert, w_exp_gate, w_exp_up, w_exp_down, g_ple, w_ple_gate, w_ple_proj, g_final):
    depth = w_in.shape[0]
    assert depth == 1, "single-layer step only"
    nb, seq, _ = x_prompt.shape
    ns, dseq, _ = x_sample.shape
    assert dseq == SAMPLE_SEQ and seq % TM == 0 and (ns * dseq) % TM_NORM == 0 and (nb * seq) % TM_NORM == 0
    assert (nb * seq + ns * dseq) % TM_IN == 0
    tp, ts = nb * seq, ns * dseq
    t = tp + ts
    xp = x_prompt.reshape(tp, D_MODEL)
    xs = x_sample.reshape(ts, D_MODEL)

    row = lambda a: a.reshape(1, -1)
    tril = jnp.tril(jnp.ones((CHUNK, CHUNK), dtype=bool))
    w_sp = jnp.where(tril[None], w_spatial[0], 0.0)
    eye = jnp.eye(CHUNK // dseq, dtype=F32)
    w_sp_s = jnp.einsum("ij,gts->gitjs", eye, w_sp[:, :dseq, :dseq]).reshape(G_A, CHUNK, CHUNK)
    wsp = jnp.stack([w_sp, w_sp_s]).astype(BF16)
    b_p = jnp.repeat(b_spatial[0].T, D_GROUP_A, axis=1)
    b_s = jnp.tile(b_p[:dseq], (CHUNK // dseq, 1))
    bsp = jnp.stack([b_p, b_s])
    lane_pad = LANES - N_GROUPS - N_EXPERTS
    wr = jnp.concatenate([w_router_group[0], w_router_expert[0], jnp.zeros((D_MODEL, lane_pad), F32)], axis=1)
    br = jnp.concatenate([b_router_group[0], b_router_expert[0], jnp.zeros((lane_pad,), F32)]).reshape(1, LANES)
    f = jnp.pad(state_conv[0], ((0, 0), (0, dseq - (CONV_W - 1)), (0, 0))).reshape(ts, D_CONV)

    n = _norm(xp, xs, row(g_mix[0]))
    z, (wa, wb, wo, wpg) = _inproj(n, w_in, [w_br_a, w_br_b, w_out, w_ple_gate])
    h, xm, route, vnp, vns, qp, qs = _mixer(
        z, xp, xs, f, row(ln_v_g[0]), row(ln_v_b[0]), conv_w[0], wsp, bsp, wa, wb, wo,
        row(g_moe[0]), wr.astype(BF16), br, n_batch=nb)

    n_tiles = (2 * t) // TM_E + N_EXPERTS
    te, nused, pos, counts, offs = _route_tables(route, n_tiles)
    xsort = _dispatch(pos, counts, offs, nused, xm, n_tiles=n_tiles)
    ys = _experts(te, nused, xsort, w_exp_gate, w_exp_up, w_exp_down, n_tiles=n_tiles)

    yp, ysm = _tail(pos, ys, h, route, p_prompt[0].reshape(tp, -1), p_sample[0].reshape(ts, -1),
                    row(g_ple[0]), wpg, w_ple_proj[0].astype(BF16), row(g_final))

    y_prompt = yp.reshape(nb, seq, D_MODEL)
    y_sample = ysm.reshape(ns, dseq, D_MODEL)
    conv_state_prompt = qp[:nb][None]
    conv_state_sample = qs.reshape(ns, dseq, D_CONV)[:, dseq - (CONV_W - 1):][None]
    v_rows_prompt = vnp[:nb][None]
    v_rows_sample = vns.reshape(ns, dseq, D_HALF)[None]
    return (y_prompt, y_sample, conv_state_prompt, conv_state_sample, v_rows_prompt, v_rows_sample)
```

```python
import functools

import jax
import jax.numpy as jnp
from jax import lax
from jax.experimental import pallas as pl
from jax.experimental.pallas import tpu as pltpu

D_MODEL = 2048
D_HALF = D_MODEL // 2
D_CONV = D_MODEL // 2
G_A = 4
D_GROUP_A = D_HALF // G_A
CHUNK = 128
CONV_W = 3
N_GROUPS = 4
EXPERTS_PER_GROUP = 8
N_EXPERTS = N_GROUPS * EXPERTS_PER_GROUP
D_EXPERT = D_MODEL // 4
D_IN_TOTAL = 2 * D_HALF + 3 * D_CONV + 2 * D_MODEL
EPS = 1e-6

BF16 = jnp.bfloat16
F32 = jnp.float32

LANES = 128
VMEM_LIMIT = 60 * 1024 * 1024

TM_NORM = 512
TM_IN = 1024
CAST_ROWS = 128
TN_IN = 1024
TM = 256
TM_E = 256
ROWS = D_MODEL // (2 * LANES)
PACKED = jnp.uint32
COPY_TILE = 256
DISPATCH_SLOTS = 3
TAIL_CHUNKS = 8
MERGE_CHUNKS = 8
SAMPLE_SEQ = 8
RANK_BLOCK = 256


def _rms(x, g):
    return x * lax.rsqrt(jnp.mean(x * x, axis=-1, keepdims=True) + EPS) * g


def _sigmoid(x):
    return 1.0 / (1.0 + jnp.exp(-x))


def _store_token_major(ref, x):
    m = x.shape[0]
    for s in range(ROWS):
        lo = x[:, s * LANES:(s + 1) * LANES]
        hi = x[:, (ROWS + s) * LANES:(ROWS + s + 1) * LANES]
        ref[pl.ds(s, m, stride=ROWS), :] = pltpu.pack_elementwise([lo, hi], packed_dtype=BF16)


def _packed_zeros(shape):
    zero = jnp.zeros(shape, F32)
    return pltpu.pack_elementwise([zero, zero], packed_dtype=BF16)


def _unpack_words(words):
    half = lambda idx: [pltpu.unpack_elementwise(w, index=idx, packed_dtype=BF16, unpacked_dtype=F32)
                        for w in words]
    return jnp.concatenate(half(0) + half(1), axis=1)


def _load_token_major(ref, m):
    return _unpack_words([ref[pl.ds(s, m, stride=ROWS), :] for s in range(ROWS)])


def _norm_kernel(xp_ref, xs_ref, g_ref, o_ref, *, n_prompt_tiles):
    i = pl.program_id(0)

    def run(x_ref):
        o_ref[...] = _rms(x_ref[...], g_ref[...]).astype(BF16)

    @pl.when(i < n_prompt_tiles)
    def _():
        run(xp_ref)

    @pl.when(i >= n_prompt_tiles)
    def _():
        run(xs_ref)


def _norm(xp, xs, g):
    tp, ts = xp.shape[0], xs.shape[0]
    npt, nst = tp // TM_NORM, ts // TM_NORM
    return pl.pallas_call(
        functools.partial(_norm_kernel, n_prompt_tiles=npt),
        grid=(npt + nst,),
        in_specs=[
            pl.BlockSpec((TM_NORM, D_MODEL), lambda i: (jnp.minimum(i, npt - 1), 0)),
            pl.BlockSpec((TM_NORM, D_MODEL), lambda i: (jnp.maximum(i - npt, 0), 0)),
            pl.BlockSpec((1, D_MODEL), lambda i: (0, 0)),
        ],
        out_specs=pl.BlockSpec((TM_NORM, D_MODEL), lambda i: (i, 0)),
        out_shape=jax.ShapeDtypeStruct((tp + ts, D_MODEL), BF16),
        compiler_params=pltpu.CompilerParams(
            dimension_semantics=("arbitrary",), vmem_limit_bytes=VMEM_LIMIT),
        name="norm",
    )(xp, xs, g)


def _inproj_kernel(n_ref, w_ref, *refs, n_side):
    side_in, z_ref, side_out = refs[:n_side], refs[n_side], refs[n_side + 1:]

    z_ref[...] = jnp.dot(n_ref[...], w_ref[...].astype(BF16), preferred_element_type=F32).astype(BF16)
    for src, dst in zip(side_in, side_out):
        dst[...] = src[...].astype(BF16)


def _inproj(n, w_in, side_weights):
    t = n.shape[0]
    n_j, n_i = D_IN_TOTAL // TN_IN, t // TM_IN
    starts, start = [], 0
    for w in side_weights:
        assert w.shape[1] % CAST_ROWS == 0
        starts.append(start)
        start += w.shape[1] // CAST_ROWS
    assert start <= n_j * n_i, "not enough grid steps for the weight casts"

    def side_index(k):
        n_blk = side_weights[k].shape[1] // CAST_ROWS
        return lambda j, i: jnp.clip(j * n_i + i - starts[k], 0, n_blk - 1)

    side_in = [pl.BlockSpec((None, CAST_ROWS, w.shape[2]), lambda j, i, f=side_index(k): (0, f(j, i), 0))
               for k, w in enumerate(side_weights)]
    side_out = [pl.BlockSpec((CAST_ROWS, w.shape[2]), lambda j, i, f=side_index(k): (f(j, i), 0))
                for k, w in enumerate(side_weights)]
    outs = pl.pallas_call(
        functools.partial(_inproj_kernel, n_side=len(side_weights)),
        grid=(n_j, n_i),
        in_specs=[
            pl.BlockSpec((TM_IN, D_MODEL), lambda j, i: (i, 0)),
            pl.BlockSpec((None, D_MODEL, TN_IN), lambda j, i: (0, 0, j)),
        ] + side_in,
        out_specs=[pl.BlockSpec((TM_IN, TN_IN), lambda j, i: (i, j))] + side_out,
        out_shape=[jax.ShapeDtypeStruct((t, D_IN_TOTAL), BF16)]
        + [jax.ShapeDtypeStruct(w.shape[1:], BF16) for w in side_weights],
        compiler_params=pltpu.CompilerParams(
            dimension_semantics=("arbitrary", "arbitrary"), vmem_limit_bytes=VMEM_LIMIT),
        name="in_proj",
    )(n, w_in, *side_weights)
    return outs[0], outs[1:]


def _route(lg):
    shape = lg.shape
    lane = lax.broadcasted_iota(jnp.int32, shape, 1)
    lanef = lane.astype(F32)
    neg = jnp.float32(-jnp.inf)
    far = jnp.float32(LANES)
    is_g = lane < N_GROUPS
    lgm = jnp.where(is_g, lg, neg)
    gmax = jnp.max(lgm, axis=-1, keepdims=True)
    gsel = jnp.min(jnp.where(lgm == gmax, lanef, far), axis=-1, keepdims=True)
    p_group = 1.0 / jnp.sum(jnp.where(is_g, jnp.exp(lgm - gmax), 0.0), axis=-1, keepdims=True)
    lane_group = ((lane - N_GROUPS) >> (EXPERTS_PER_GROUP.bit_length() - 1)).astype(F32)
    is_e = lane_group == gsel
    le = jnp.where(is_e, lg, neg)
    m1 = jnp.max(le, axis=-1, keepdims=True)
    i1 = jnp.min(jnp.where(le == m1, lanef, far), axis=-1, keepdims=True)
    le2 = jnp.where(lanef == i1, neg, le)
    m2 = jnp.max(le2, axis=-1, keepdims=True)
    i2 = jnp.min(jnp.where(le2 == m2, lanef, far), axis=-1, keepdims=True)
    e21 = jnp.exp(m2 - m1)
    w1 = 1.0 / (1.0 + e21)
    w2 = e21 * w1
    out = jnp.where(lane == 0, i1 - N_GROUPS,
                    jnp.where(lane == 1, i2 - N_GROUPS,
                              jnp.where(lane == 2, p_group * w1,
                                        jnp.where(lane == 3, p_group * w2, 0.0))))
    return out


def _mixer_kernel(z_ref, xp_ref, xs_ref, f_ref, lng_ref, lnb_ref, cw_ref, wsp_ref, bsp_ref,
                  wa_ref, wb_ref, wo_ref, gm_ref, wr_ref, br_ref,
                  h_ref, xm_ref, route_ref, vnp_ref, vns_ref, qp_ref, qs_ref,
                  qprev_ref, ya_ref, yb_ref, m_ref, *, n_prompt_tiles, tiles_per_seq):
    i = pl.program_id(0)
    is_prompt = i < n_prompt_tiles
    c_u, c_v, c_b, c_c, c_x = (k * D_HALF for k in range(5))
    c_ga = c_x + D_CONV
    c_gb = c_ga + D_MODEL

    v = z_ref[:, c_v:c_v + D_HALF].astype(F32)
    mu = jnp.mean(v, axis=-1, keepdims=True)
    vc = v - mu
    vn = vc * lax.rsqrt(jnp.mean(vc * vc, axis=-1, keepdims=True) + EPS) * lng_ref[...] + lnb_ref[...]
    vnb = vn.astype(BF16)
    for c in range(TM // CHUNK):
        rows = slice(c * CHUNK, (c + 1) * CHUNK)
        for g in range(G_A):
            cols = slice(g * D_GROUP_A, (g + 1) * D_GROUP_A)
            s = jnp.dot(wsp_ref[g], vnb[rows, cols], preferred_element_type=F32) + bsp_ref[:, cols]
            u = z_ref[rows, c_u + g * D_GROUP_A:c_u + (g + 1) * D_GROUP_A].astype(F32)
            ya_ref[rows, cols] = (u * s).astype(BF16)

    q = z_ref[:, c_c:c_c + D_CONV].astype(F32) * z_ref[:, c_x:c_x + D_CONV].astype(F32)
    q1 = pltpu.roll(q, 1, 0)
    q2 = pltpu.roll(q, 2, 0)
    row = lax.broadcasted_iota(jnp.int32, q.shape, 0)
    cw0, cw1, cw2 = cw_ref[0:1, :], cw_ref[1:2, :], cw_ref[2:3, :]

    seq_start = i % tiles_per_seq == 0
    p6 = jnp.where(seq_start, 0.0, qprev_ref[6:7, :])
    p7 = jnp.where(seq_start, 0.0, qprev_ref[7:8, :])
    f = f_ref[...]
    pos_in_seq = jnp.where(is_prompt, row, row & (SAMPLE_SEQ - 1))
    fix1 = jnp.where(is_prompt, jnp.broadcast_to(p7, q.shape), pltpu.roll(f, TM - 1, 0))
    fix2 = jnp.where(is_prompt, jnp.where(row == 0, p6, p7), f)
    a1 = jnp.where(pos_in_seq == 0, fix1, q1)
    a2 = jnp.where(pos_in_seq < 2, fix2, q2)
    yc = cw0 * a2 + cw1 * a1 + cw2 * q
    qprev_ref[...] = q[TM - 8:TM, :]
    qp_ref[...] = q[TM - (CONV_W - 1):TM, :]
    vnp_ref[...] = vn[TM - CHUNK:TM, :]
    qs_ref[...] = q
    vns_ref[...] = vn

    yb_ref[...] = (z_ref[:, c_b:c_b + D_CONV].astype(F32) * yc).astype(BF16)

    width = D_MODEL // MERGE_CHUNKS
    for c in range(MERGE_CHUNKS):
        cols = slice(c * width, (c + 1) * width)
        a = jnp.dot(ya_ref[...], wa_ref[:, cols], preferred_element_type=F32)
        b = jnp.dot(yb_ref[...], wb_ref[:, cols], preferred_element_type=F32)
        ga = z_ref[:, c_ga + c * width:c_ga + (c + 1) * width].astype(F32)
        gb = z_ref[:, c_gb + c * width:c_gb + (c + 1) * width].astype(F32)
        m_ref[:, cols] = (_sigmoid(ga) * a + _sigmoid(gb) * b).astype(BF16)

    x = jnp.where(is_prompt, xp_ref[...], xs_ref[...])
    h = x + jnp.dot(m_ref[...], wo_ref[...], preferred_element_type=F32)
    h_ref[...] = h
    xm = _rms(h, gm_ref[...])
    _store_token_major(xm_ref, xm)
    lg = jnp.dot(xm.astype(BF16), wr_ref[...], preferred_element_type=F32) + br_ref[...]
    route_ref[...] = _route(lg)


def _mixer(z, xp, xs, f, lng, lnb, cw, wsp, bsp, wa, wb, wo, gm, wr, br, *, n_batch):
    tp, ts = xp.shape[0], xs.shape[0]
    t = tp + ts
    npt, nst = tp // TM, ts // TM
    tiles_per_seq = tp // n_batch // TM
    once = pl.Buffered(1)

    def pidx(i):
        return jnp.minimum(i, npt - 1)

    def sidx(i):
        return jnp.maximum(i - npt, 0)

    def seq_or_spare(i):
        return jnp.where(i < npt, i // tiles_per_seq, n_batch)

    const = lambda shape: pl.BlockSpec(shape, lambda i: (0,) * len(shape), pipeline_mode=once)
    in_specs = [
        pl.BlockSpec((TM, D_IN_TOTAL), lambda i: (i, 0)),
        pl.BlockSpec((TM, D_MODEL), lambda i: (pidx(i), 0)),
        pl.BlockSpec((TM, D_MODEL), lambda i: (sidx(i), 0)),
        pl.BlockSpec((TM, D_CONV), lambda i: (sidx(i), 0)),
        const((1, D_HALF)), const((1, D_HALF)), const((CONV_W, D_CONV)),
        pl.BlockSpec((None, G_A, CHUNK, CHUNK), lambda i: (jnp.where(i < npt, 0, 1), 0, 0, 0)),
        pl.BlockSpec((None, CHUNK, D_HALF), lambda i: (jnp.where(i < npt, 0, 1), 0, 0)),
        const((D_HALF, D_MODEL)), const((D_CONV, D_MODEL)), const((D_MODEL, D_MODEL)),
        const((1, D_MODEL)), const((D_MODEL, LANES)), const((1, LANES)),
    ]
    out_specs = [
        pl.BlockSpec((TM, D_MODEL), lambda i: (i, 0)),
        pl.BlockSpec((TM * ROWS, LANES), lambda i: (i, 0)),
        pl.BlockSpec((TM, LANES), lambda i: (i, 0)),
        pl.BlockSpec((None, CHUNK, D_HALF), lambda i: (seq_or_spare(i), 0, 0)),
        pl.BlockSpec((TM, D_HALF), lambda i: (sidx(i), 0)),
        pl.BlockSpec((None, CONV_W - 1, D_CONV), lambda i: (seq_or_spare(i), 0, 0)),
        pl.BlockSpec((TM, D_CONV), lambda i: (sidx(i), 0)),
    ]
    out_shape = [
        jax.ShapeDtypeStruct((t, D_MODEL), F32),
        jax.ShapeDtypeStruct((t * ROWS, LANES), PACKED),
        jax.ShapeDtypeStruct((t, LANES), F32),
        jax.ShapeDtypeStruct((n_batch + 1, CHUNK, D_HALF), F32),
        jax.ShapeDtypeStruct((ts, D_HALF), F32),
        jax.ShapeDtypeStruct((n_batch + 1, CONV_W - 1, D_CONV), F32),
        jax.ShapeDtypeStruct((ts, D_CONV), F32),
    ]
    return pl.pallas_call(
        functools.partial(_mixer_kernel, n_prompt_tiles=npt, tiles_per_seq=tiles_per_seq),
        grid=(npt + nst,),
        in_specs=in_specs,
        out_specs=out_specs,
        out_shape=out_shape,
        scratch_shapes=[
            pltpu.VMEM((8, D_CONV), F32),
            pltpu.VMEM((TM, D_HALF), BF16),
            pltpu.VMEM((TM, D_CONV), BF16),
            pltpu.VMEM((TM, D_MODEL), BF16),
        ],
        compiler_params=pltpu.CompilerParams(
            dimension_semantics=("arbitrary",), vmem_limit_bytes=VMEM_LIMIT),
        name="mixer",
    )(z, xp, xs, f, lng, lnb, cw, wsp, bsp, wa, wb, wo, gm, wr, br)


def _dispatch_kernel(pos_ref, cnt_ref, off_ref, nused_ref, xm_hbm, xs_hbm, inbuf, zbuf, isem, rsem, zsem,
                     *, n_steps, n_tiles):
    zbuf[...] = _packed_zeros(zbuf.shape)
    nused = nused_ref[0]
    pad_bits = TM_E.bit_length() - 1

    def pad_copies(e, fn):
        cnt = cnt_ref[e]
        npad = (TM_E - cnt % TM_E) % TM_E
        start = off_ref[e] + cnt
        for b in range(pad_bits):
            size = 1 << b
            at = start + ((npad >> (b + 1)) << (b + 1))

            @pl.when(((npad >> b) & 1) == 1)
            def _():
                fn(pltpu.make_async_copy(zbuf.at[pl.ds(0, size * ROWS), :],
                                         xs_hbm.at[pl.ds(at * ROWS, size * ROWS), :], zsem))

    def tile_copy(tile):
        return pltpu.make_async_copy(zbuf, xs_hbm.at[pl.ds(tile * TM_E * ROWS, TM_E * ROWS), :], zsem)

    def loop(lo, hi, body):
        lax.fori_loop(lo, hi, lambda k, c: (body(k), c)[1], 0)

    loop(0, N_EXPERTS, lambda e: pad_copies(e, lambda cp: cp.start()))
    loop(nused, n_tiles, lambda tile: tile_copy(tile).start())

    def in_copy(step, slot):
        return pltpu.make_async_copy(xm_hbm.at[pl.ds(step * COPY_TILE * ROWS, COPY_TILE * ROWS), :],
                                     inbuf.at[slot], isem.at[slot])

    def rows_wait(slot):
        for _ in range(2):
            pltpu.make_async_copy(inbuf.at[slot], xs_hbm.at[pl.ds(0, COPY_TILE * ROWS), :], rsem.at[slot]).wait()

    in_copy(0, 0).start()

    def step(s):
        slot = s % DISPATCH_SLOTS
        nxt = (s + 1) % DISPATCH_SLOTS

        @pl.when(s >= DISPATCH_SLOTS - 1)
        def _():
            rows_wait(nxt)

        @pl.when(s + 1 < n_steps)
        def _():
            in_copy(s + 1, nxt).start()

        in_copy(s, slot).wait()
        base = s * COPY_TILE * 2

        def body(r, carry):
            for k in range(2):
                p = pos_ref[base + 2 * r + k]
                pltpu.make_async_copy(inbuf.at[slot, pl.ds(r * ROWS, ROWS), :],
                                      xs_hbm.at[pl.ds(p * ROWS, ROWS), :], rsem.at[slot]).start(priority=k)
            return carry

        lax.fori_loop(0, COPY_TILE, body, 0, unroll=4)

    loop(0, n_steps, step)
    for back in range(DISPATCH_SLOTS - 1, 0, -1):
        rows_wait((n_steps - back) % DISPATCH_SLOTS)
    loop(0, N_EXPERTS, lambda e: pad_copies(e, lambda cp: cp.wait()))
    loop(nused, n_tiles, lambda tile: tile_copy(tile).wait())


def _dispatch(pos, cnt, off, nused, xm, *, n_tiles):
    n_tokens = xm.shape[0] // ROWS
    assert n_tokens % COPY_TILE == 0 and n_tokens // COPY_TILE >= DISPATCH_SLOTS
    smem = pl.BlockSpec(memory_space=pltpu.SMEM)
    return pl.pallas_call(
        functools.partial(_dispatch_kernel, n_steps=n_tokens // COPY_TILE, n_tiles=n_tiles),
        in_specs=[smem, smem, smem, smem, pl.BlockSpec(memory_space=pl.ANY)],
        out_specs=pl.BlockSpec(memory_space=pl.ANY),
        out_shape=jax.ShapeDtypeStruct((n_tiles * TM_E * ROWS, LANES), PACKED),
        scratch_shapes=[
            pltpu.VMEM((DISPATCH_SLOTS, COPY_TILE * ROWS, LANES), PACKED),
            pltpu.VMEM((TM_E * ROWS, LANES), PACKED),
            pltpu.SemaphoreType.DMA((DISPATCH_SLOTS,)),
            pltpu.SemaphoreType.DMA((DISPATCH_SLOTS,)),
            pltpu.SemaphoreType.DMA(()),
        ],
        compiler_params=pltpu.CompilerParams(vmem_limit_bytes=VMEM_LIMIT),
        name="dispatch",
    )(pos, cnt, off, nused, xm)


def _expert_kernel(te_ref, nused_ref, xs_ref, wg_hbm, wu_hbm, wd_hbm, y_ref,
                   wgf, wuf, wdf, wsem, seen_ref):
    i = pl.program_id(0)
    nused = nused_ref[0]

    def weight_copies(e, slot):
        return [pltpu.make_async_copy(src.at[0, e], dst.at[slot], wsem.at[slot, k])
                for k, (src, dst) in enumerate(((wg_hbm, wgf), (wu_hbm, wuf), (wd_hbm, wdf)))]

    @pl.when(i == 0)
    def _():
        seen_ref[0] = 0
        for cp in weight_copies(te_ref[0], 0):
            cp.start()

    @pl.when(i < nused)
    def _():
        e = te_ref[i]
        new_expert = jnp.logical_or(i == 0, e != te_ref[jnp.maximum(i - 1, 0)])

        @pl.when(new_expert)
        def _():
            slot = seen_ref[0] % 2
            seen_ref[0] = seen_ref[0] + 1
            seen_ref[1] = slot
            nxt = lax.while_loop(lambda j: jnp.logical_and(j < nused, te_ref[jnp.minimum(j, nused - 1)] == e),
                                 lambda j: j + 1, i + 1)

            @pl.when(nxt < nused)
            def _():
                for cp in weight_copies(te_ref[jnp.minimum(nxt, nused - 1)], 1 - slot):
                    cp.start(priority=1)

            for cp in weight_copies(e, slot):
                cp.wait()

        cur = seen_ref[1]
        x = _load_token_major(xs_ref, TM_E).astype(BF16)
        g = jnp.dot(x, wgf[cur].astype(BF16), preferred_element_type=F32)
        u = jnp.dot(x, wuf[cur].astype(BF16), preferred_element_type=F32)
        hid = (g * _sigmoid(g)) * u
        _store_token_major(y_ref, jnp.dot(hid.astype(BF16), wdf[cur].astype(BF16), preferred_element_type=F32))

    @pl.when(i >= nused)
    def _():
        y_ref[...] = _packed_zeros(y_ref.shape)


def _experts(te, nused, xs, wg, wu, wd, *, n_tiles):
    grid_spec = pltpu.PrefetchScalarGridSpec(
        num_scalar_prefetch=2,
        grid=(n_tiles,),
        in_specs=[
            pl.BlockSpec((TM_E * ROWS, LANES), lambda i, te, nu: (jnp.minimum(i, nu[0] - 1), 0)),
            pl.BlockSpec(memory_space=pl.ANY),
            pl.BlockSpec(memory_space=pl.ANY),
            pl.BlockSpec(memory_space=pl.ANY),
        ],
        out_specs=pl.BlockSpec((TM_E * ROWS, LANES), lambda i, te, nu: (i, 0)),
        scratch_shapes=[
            pltpu.VMEM((2, D_MODEL, D_EXPERT), F32),
            pltpu.VMEM((2, D_MODEL, D_EXPERT), F32),
            pltpu.VMEM((2, D_EXPERT, D_MODEL), F32),
            pltpu.SemaphoreType.DMA((2, 3)),
            pltpu.SMEM((2,), jnp.int32),
        ],
    )
    return pl.pallas_call(
        _expert_kernel,
        grid_spec=grid_spec,
        out_shape=jax.ShapeDtypeStruct((n_tiles * TM_E * ROWS, LANES), PACKED),
        compiler_params=pltpu.CompilerParams(
            dimension_semantics=("arbitrary",), vmem_limit_bytes=VMEM_LIMIT),
        name="experts",
    )(te, nused, xs, wg, wu, wd)


def _tail_kernel(pos_ref, ys_hbm, h_ref, route_ref, pp_ref, ps_ref, gp_ref, wpg_ref, wpp_ref, gf_ref,
                 yp_ref, ysm_ref, ybuf, sem, h2_ref, a_ref, proj_ref, *, n_prompt_tiles, n_tiles):
    i = pl.program_id(0)
    slot = i % 2
    is_prompt = i < n_prompt_tiles

    def issue(tile, dst_slot):
        base = tile * TM * 2

        def body(r, carry):
            for k in range(2):
                p = pos_ref[base + 2 * r + k]
                pltpu.make_async_copy(ys_hbm.at[pl.ds(p * ROWS, ROWS), :],
                                      ybuf.at[dst_slot, k, pl.ds(r * ROWS, ROWS), :],
                                      sem.at[dst_slot]).start()
            return carry

        lax.fori_loop(0, TM, body, 0, unroll=4)

    def slot_wait(s):
        for k in range(2):
            pltpu.make_async_copy(ys_hbm.at[pl.ds(0, TM * ROWS), :], ybuf.at[s, k], sem.at[s]).wait()

    def row_sumsq(acc):
        return jnp.sum(acc, axis=-1, keepdims=True) * (1.0 / D_MODEL)

    @pl.when(i == 0)
    def _():
        issue(0, 0)

    slot_wait(slot)

    w0 = jnp.broadcast_to(route_ref[:, 2:3], (TM, LANES))
    w1 = jnp.broadcast_to(route_ref[:, 3:4], (TM, LANES))
    acc = jnp.zeros((TM, LANES), F32)
    for s in range(ROWS):
        word0 = ybuf[slot, 0, pl.ds(s, TM, stride=ROWS), :]
        word1 = ybuf[slot, 1, pl.ds(s, TM, stride=ROWS), :]
        for idx in range(2):
            cols = slice((idx * ROWS + s) * LANES, (idx * ROWS + s + 1) * LANES)
            y0 = pltpu.unpack_elementwise(word0, index=idx, packed_dtype=BF16, unpacked_dtype=F32)
            y1 = pltpu.unpack_elementwise(word1, index=idx, packed_dtype=BF16, unpacked_dtype=F32)
            h2 = h_ref[:, cols] + (w0 * y0 + w1 * y1)
            h2_ref[:, cols] = h2
            acc = acc + h2 * h2
    r = lax.rsqrt(row_sumsq(acc) + EPS)
    a_ref[...] = (h2_ref[...] * r * gp_ref[...]).astype(BF16)
    p = jnp.where(is_prompt, pp_ref[...], ps_ref[...]).astype(BF16)
    proj_ref[...] = jnp.dot(p, wpp_ref[...], preferred_element_type=F32)

    nxt_base = jnp.minimum(i + 1, n_tiles - 1) * TM * 2
    rows_per_chunk = TM // TAIL_CHUNKS
    width = D_MODEL // TAIL_CHUNKS
    acc = jnp.zeros((TM, LANES), F32)
    for c in range(TAIL_CHUNKS):
        for rr in range(rows_per_chunk):
            row = c * rows_per_chunk + rr
            for k in range(2):
                p_row = pos_ref[nxt_base + 2 * row + k]
                pltpu.make_async_copy(ys_hbm.at[pl.ds(p_row * ROWS, ROWS), :],
                                      ybuf.at[1 - slot, k, pl.ds(row * ROWS, ROWS), :],
                                      sem.at[1 - slot]).start()
        cols = slice(c * width, (c + 1) * width)
        gate = _sigmoid(jnp.dot(a_ref[...], wpg_ref[:, cols], preferred_element_type=F32))
        h3 = h2_ref[:, cols] + gate * proj_ref[:, cols]
        h2_ref[:, cols] = h3
        for j in range(width // LANES):
            part = h3[:, j * LANES:(j + 1) * LANES]
            acc = acc + part * part
    scale = lax.rsqrt(row_sumsq(acc) + EPS)

    @pl.when(is_prompt)
    def _():
        yp_ref[...] = h2_ref[...] * scale * gf_ref[...]

    @pl.when(jnp.logical_not(is_prompt))
    def _():
        ysm_ref[...] = h2_ref[...] * scale * gf_ref[...]

    @pl.when(i == n_tiles - 1)
    def _():
        slot_wait(1 - slot)


def _tail(pos, ys, h, route, pp, ps, gp, wpg, wpp, gf):
    tp, ts = pp.shape[0], ps.shape[0]
    npt, nst = tp // TM, ts // TM
    ple = pp.shape[1]
    once = pl.Buffered(1)

    def pidx(i):
        return jnp.minimum(i, npt - 1)

    def sidx(i):
        return jnp.maximum(i - npt, 0)

    grid_spec = pltpu.PrefetchScalarGridSpec(
        num_scalar_prefetch=1,
        grid=(npt + nst,),
        in_specs=[
            pl.BlockSpec(memory_space=pl.ANY),
            pl.BlockSpec((TM, D_MODEL), lambda i, pos: (i, 0)),
            pl.BlockSpec((TM, LANES), lambda i, pos: (i, 0)),
            pl.BlockSpec((TM, ple), lambda i, pos: (pidx(i), 0)),
            pl.BlockSpec((TM, ple), lambda i, pos: (sidx(i), 0)),
            pl.BlockSpec((1, D_MODEL), lambda i, pos: (0, 0), pipeline_mode=once),
            pl.BlockSpec((D_MODEL, D_MODEL), lambda i, pos: (0, 0), pipeline_mode=once),
            pl.BlockSpec((ple, D_MODEL), lambda i, pos: (0, 0), pipeline_mode=once),
            pl.BlockSpec((1, D_MODEL), lambda i, pos: (0, 0), pipeline_mode=once),
        ],
        out_specs=[
            pl.BlockSpec((TM, D_MODEL), lambda i, pos: (pidx(i), 0)),
            pl.BlockSpec((TM, D_MODEL), lambda i, pos: (sidx(i), 0)),
        ],
        scratch_shapes=[
            pltpu.VMEM((2, 2, TM * ROWS, LANES), PACKED),
            pltpu.SemaphoreType.DMA((2,)),
            pltpu.VMEM((TM, D_MODEL), F32),
            pltpu.VMEM((TM, D_MODEL), BF16),
            pltpu.VMEM((TM, D_MODEL), F32),
        ],
    )
    return pl.pallas_call(
        functools.partial(_tail_kernel, n_prompt_tiles=npt, n_tiles=npt + nst),
        grid_spec=grid_spec,
        out_shape=[jax.ShapeDtypeStruct((tp, D_MODEL), F32), jax.ShapeDtypeStruct((ts, D_MODEL), F32)],
        compiler_params=pltpu.CompilerParams(
            dimension_semantics=("arbitrary",), vmem_limit_bytes=VMEM_LIMIT),
        name="tail",
    )(pos, ys, h, route, pp, ps, gp, wpg, wpp, gf)


def _route_tables(route, n_tiles):
    e = route[:, 0:2].astype(jnp.int32).reshape(-1)
    n_blocks = e.shape[0] // RANK_BLOCK
    onehot = (e[:, None] == jnp.arange(N_EXPERTS, dtype=jnp.int32)[None, :]).astype(BF16)
    onehot = onehot.reshape(n_blocks, RANK_BLOCK, N_EXPERTS)
    ltri = jnp.tril(jnp.ones((RANK_BLOCK, RANK_BLOCK), BF16), -1)
    within = jnp.einsum("ij,bjk->bik", ltri, onehot, preferred_element_type=F32)
    block_counts = jnp.sum(onehot.astype(F32), axis=1)
    before = jnp.cumsum(block_counts, axis=0) - block_counts
    counts = jnp.sum(block_counts, axis=0).astype(jnp.int32)
    padded = ((counts + TM_E - 1) // TM_E) * TM_E
    ends = jnp.cumsum(padded)
    offs = ends - padded
    pos = jnp.sum((within + before[:, None, :] + offs[None, None, :].astype(F32)) * onehot.astype(F32), axis=-1)
    pos = pos.reshape(-1)
    nused = ends[-1] // TM_E
    tile_id = jnp.arange(n_tiles, dtype=jnp.int32)
    te = jnp.sum((tile_id[:, None] * TM_E >= ends[None, :]).astype(jnp.int32), axis=1)
    te_last = jnp.sum((((nused - 1) * TM_E) >= ends).astype(jnp.int32))
    te = jnp.where(tile_id < nused, te, te_last)
    i32 = lambda a: a.astype(jnp.int32)
    return i32(te), i32(nused).reshape(1), i32(pos), i32(counts), i32(offs)


def kernel(x_prompt, x_sample, state_conv, p_prompt, p_sample, g_mix, w_in, ln_v_g, ln_v_b, w_spatial, b_spatial, conv_w, w_br_a, w_br_b, w_out, g_moe, w_router_group, b_router_group, w_router_expert, b_router_expert, w_exp_gate, w_exp_up, w_exp_down, g_ple, w_ple_gate, w_ple_proj, g_final):
    depth = w_in.shape[0]
    assert depth == 1, "single-layer step only"
    nb, seq, _ = x_prompt.shape
    ns, dseq, _ = x_sample.shape
    assert dseq == SAMPLE_SEQ and seq % TM == 0 and (ns * dseq) % TM_NORM == 0 and (nb * seq) % TM_NORM == 0
    assert (nb * seq + ns * dseq) % TM_IN == 0
    tp, ts = nb * seq, ns * dseq
    t = tp + ts
    xp = x_prompt.reshape(tp, D_MODEL)
    xs = x_sample.reshape(ts, D_MODEL)

    row = lambda a: a.reshape(1, -1)
    tril = jnp.tril(jnp.ones((CHUNK, CHUNK), dtype=bool))
    w_sp = jnp.where(tril[None], w_spatial[0], 0.0)
    eye = jnp.eye(CHUNK // dseq, dtype=F32)
    w_sp_s = jnp.einsum("ij,gts->gitjs", eye, w_sp[:, :dseq, :dseq]).reshape(G_A, CHUNK, CHUNK)
    wsp = jnp.stack([w_sp, w_sp_s]).astype(BF16)
    b_p = jnp.repeat(b_spatial[0].T, D_GROUP_A, axis=1)
    b_s = jnp.tile(b_p[:dseq], (CHUNK // dseq, 1))
    bsp = jnp.stack([b_p, b_s])
    lane_pad = LANES - N_GROUPS - N_EXPERTS
    wr = jnp.concatenate([w_router_group[0], w_router_expert[0], jnp.zeros((D_MODEL, lane_pad), F32)], axis=1)
    br = jnp.concatenate([b_router_group[0], b_router_expert[0], jnp.zeros((lane_pad,), F32)]).reshape(1, LANES)
    f = jnp.pad(state_conv[0], ((0, 0), (0, dseq - (CONV_W - 1)), (0, 0))).reshape(ts, D_CONV)

    n = _norm(xp, xs, row(g_mix[0]))
    z, (wa, wb, wo, wpg) = _inproj(n, w_in, [w_br_a, w_br_b, w_out, w_ple_gate])
    h, xm, route, vnp, vns, qp, qs = _mixer(
        z, xp, xs, f, row(ln_v_g[0]), row(ln_v_b[0]), conv_w[0], wsp, bsp, wa, wb, wo,
        row(g_moe[0]), wr.astype(BF16), br, n_batch=nb)

    n_tiles = (2 * t) // TM_E + N_EXPERTS
    te, nused, pos, counts, offs = _route_tables(route, n_tiles)
    xsort = _dispatch(pos, counts, offs, nused, xm, n_tiles=n_tiles)
    ys = _experts(te, nused, xsort, w_exp_gate, w_exp_up, w_exp_down, n_tiles=n_tiles)

    yp, ysm = _tail(pos, ys, h, route, p_prompt[0].reshape(tp, -1), p_sample[0].reshape(ts, -1),
                    row(g_ple[0]), wpg, w_ple_proj[0].astype(BF16), row(g_final))

    y_prompt = yp.reshape(nb, seq, D_MODEL)
    y_sample = ysm.reshape(ns, dseq, D_MODEL)
    conv_state_prompt = qp[:nb][None]
    conv_state_sample = qs.reshape(ns, dseq, D_CONV)[:, dseq - (CONV_W - 1):][None]
    v_rows_prompt = vnp[:nb][None]
    v_rows_sample = vns.reshape(ns, dseq, D_HALF)[None]
    return (y_prompt, y_sample, conv_state_prompt, conv_state_sample, v_rows_prompt, v_rows_sample)
```

```python
import functools

import jax
import jax.numpy as jnp
from jax import lax
from jax.experimental import pallas as pl
from jax.experimental.pallas import tpu as pltpu

D_MODEL = 2048
D_HALF = D_MODEL // 2
D_CONV = D_MODEL // 2
G_A = 4
D_GROUP_A = D_HALF // G_A
CHUNK = 128
CONV_W = 3
N_GROUPS = 4
EXPERTS_PER_GROUP = 8
N_EXPERTS = N_GROUPS * EXPERTS_PER_GROUP
D_EXPERT = D_MODEL // 4
D_IN_TOTAL = 2 * D_HALF + 3 * D_CONV + 2 * D_MODEL
EPS = 1e-6

BF16 = jnp.bfloat16
F32 = jnp.float32

LANES = 128
VMEM_LIMIT = 60 * 1024 * 1024

TM_NORM = 512
TM_IN = 1024
CAST_ROWS = 128
TN_IN = 1024
TM = 256
TM_E = 256
ROWS = D_MODEL // (2 * LANES)
PACKED = jnp.uint32
COPY_TILE = 256
DISPATCH_SLOTS = 3
TAIL_CHUNKS = 8
MERGE_CHUNKS = 8
SAMPLE_SEQ = 8
RANK_BLOCK = 256


def _rms(x, g):
    return x * lax.rsqrt(jnp.mean(x * x, axis=-1, keepdims=True) + EPS) * g


def _sigmoid(x):
    return 1.0 / (1.0 + jnp.exp(-x))


def _store_token_major(ref, x):
    m = x.shape[0]
    for s in range(ROWS):
        lo = x[:, s * LANES:(s + 1) * LANES]
        hi = x[:, (ROWS + s) * LANES:(ROWS + s + 1) * LANES]
        ref[pl.ds(s, m, stride=ROWS), :] = pltpu.pack_elementwise([lo, hi], packed_dtype=BF16)


def _packed_zeros(shape):
    zero = jnp.zeros(shape, F32)
    return pltpu.pack_elementwise([zero, zero], packed_dtype=BF16)


def _unpack_words(words):
    half = lambda idx: [pltpu.unpack_elementwise(w, index=idx, packed_dtype=BF16, unpacked_dtype=F32)
                        for w in words]
    return jnp.concatenate(half(0) + half(1), axis=1)


def _load_token_major(ref, m):
    return _unpack_words([ref[pl.ds(s, m, stride=ROWS), :] for s in range(ROWS)])


def _norm_kernel(xp_ref, xs_ref, g_ref, o_ref, *, n_prompt_tiles):
    i = pl.program_id(0)

    def run(x_ref):
        o_ref[...] = _rms(x_ref[...], g_ref[...]).astype(BF16)

    @pl.when(i < n_prompt_tiles)
    def _():
        run(xp_ref)

    @pl.when(i >= n_prompt_tiles)
    def _():
        run(xs_ref)


def _norm(xp, xs, g):
    tp, ts = xp.shape[0], xs.shape[0]
    npt, nst = tp // TM_NORM, ts // TM_NORM
    return pl.pallas_call(
        functools.partial(_norm_kernel, n_prompt_tiles=npt),
        grid=(npt + nst,),
        in_specs=[
            pl.BlockSpec((TM_NORM, D_MODEL), lambda i: (jnp.minimum(i, npt - 1), 0)),
            pl.BlockSpec((TM_NORM, D_MODEL), lambda i: (jnp.maximum(i - npt, 0), 0)),
            pl.BlockSpec((1, D_MODEL), lambda i: (0, 0)),
        ],
        out_specs=pl.BlockSpec((TM_NORM, D_MODEL), lambda i: (i, 0)),
        out_shape=jax.ShapeDtypeStruct((tp + ts, D_MODEL), BF16),
        compiler_params=pltpu.CompilerParams(
            dimension_semantics=("arbitrary",), vmem_limit_bytes=VMEM_LIMIT),
        name="norm",
    )(xp, xs, g)


def _inproj_kernel(n_ref, w_ref, *refs, n_side):
    side_in, z_ref, side_out = refs[:n_side], refs[n_side], refs[n_side + 1:]

    z_ref[...] = jnp.dot(n_ref[...], w_ref[...].astype(BF16), preferred_element_type=F32).astype(BF16)
    for src, dst in zip(side_in, side_out):
        dst[...] = src[...].astype(BF16)


def _inproj(n, w_in, side_weights):
    t = n.shape[0]
    n_j, n_i = D_IN_TOTAL // TN_IN, t // TM_IN
    starts, start = [], 0
    for w in side_weights:
        assert w.shape[1] % CAST_ROWS == 0
        starts.append(start)
        start += w.shape[1] // CAST_ROWS
    assert start <= n_j * n_i, "not enough grid steps for the weight casts"

    def side_index(k):
        n_blk = side_weights[k].shape[1] // CAST_ROWS
        return lambda j, i: jnp.clip(j * n_i + i - starts[k], 0, n_blk - 1)

    side_in = [pl.BlockSpec((None, CAST_ROWS, w.shape[2]), lambda j, i, f=side_index(k): (0, f(j, i), 0))
               for k, w in enumerate(side_weights)]
    side_out = [pl.BlockSpec((CAST_ROWS, w.shape[2]), lambda j, i, f=side_index(k): (f(j, i), 0))
                for k, w in enumerate(side_weights)]
    outs = pl.pallas_call(
        functools.partial(_inproj_kernel, n_side=len(side_weights)),
        grid=(n_j, n_i),
        in_specs=[
            pl.BlockSpec((TM_IN, D_MODEL), lambda j, i: (i, 0)),
            pl.BlockSpec((None, D_MODEL, TN_IN), lambda j, i: (0, 0, j)),
        ] + side_in,
        out_specs=[pl.BlockSpec((TM_IN, TN_IN), lambda j, i: (i, j))] + side_out,
        out_shape=[jax.ShapeDtypeStruct((t, D_IN_TOTAL), BF16)]
        + [jax.ShapeDtypeStruct(w.shape[1:], BF16) for w in side_weights],
        compiler_params=pltpu.CompilerParams(
            dimension_semantics=("arbitrary", "arbitrary"), vmem_limit_bytes=VMEM_LIMIT),
        name="in_proj",
    )(n, w_in, *side_weights)
    return outs[0], outs[1:]


def _route(lg):
    shape = lg.shape
    lane = lax.broadcasted_iota(jnp.int32, shape, 1)
    lanef = lane.astype(F32)
    neg = jnp.float32(-jnp.inf)
    far = jnp.float32(LANES)
    is_g = lane < N_GROUPS
    lgm = jnp.where(is_g, lg, neg)
    gmax = jnp.max(lgm, axis=-1, keepdims=True)
    gsel = jnp.min(jnp.where(lgm == gmax, lanef, far), axis=-1, keepdims=True)
    p_group = 1.0 / jnp.sum(jnp.where(is_g, jnp.exp(lgm - gmax), 0.0), axis=-1, keepdims=True)
    lane_group = ((lane - N_GROUPS) >> (EXPERTS_PER_GROUP.bit_length() - 1)).astype(F32)
    is_e = lane_group == gsel
    le = jnp.where(is_e, lg, neg)
    m1 = jnp.max(le, axis=-1, keepdims=True)
    i1 = jnp.min(jnp.where(le == m1, lanef, far), axis=-1, keepdims=True)
    le2 = jnp.where(lanef == i1, neg, le)
    m2 = jnp.max(le2, axis=-1, keepdims=True)
    i2 = jnp.min(jnp.where(le2 == m2, lanef, far), axis=-1, keepdims=True)
    e21 = jnp.exp(m2 - m1)
    w1 = 1.0 / (1.0 + e21)
    w2 = e21 * w1
    out = jnp.where(lane == 0, i1 - N_GROUPS,
                    jnp.where(lane == 1, i2 - N_GROUPS,
                              jnp.where(lane == 2, p_group * w1,
                                        jnp.where(lane == 3, p_group * w2, 0.0))))
    return out


def _mixer_kernel(z_ref, xp_ref, xs_ref, f_ref, lng_ref, lnb_ref, cw_ref, wsp_ref, bsp_ref,
                  wa_ref, wb_ref, wo_ref, gm_ref, wr_ref, br_ref,
                  h_ref, xm_ref, route_ref, vnp_ref, vns_ref, qp_ref, qs_ref,
                  qprev_ref, ya_ref, yb_ref, m_ref, *, n_prompt_tiles, tiles_per_seq):
    i = pl.program_id(0)
    is_prompt = i < n_prompt_tiles
    c_u, c_v, c_b, c_c, c_x = (k * D_HALF for k in range(5))
    c_ga = c_x + D_CONV
    c_gb = c_ga + D_MODEL

    v = z_ref[:, c_v:c_v + D_HALF].astype(F32)
    mu = jnp.mean(v, axis=-1, keepdims=True)
    vc = v - mu
    vn = vc * lax.rsqrt(jnp.mean(vc * vc, axis=-1, keepdims=True) + EPS) * lng_ref[...] + lnb_ref[...]
    vnb = vn.astype(BF16)
    for c in range(TM // CHUNK):
        rows = slice(c * CHUNK, (c + 1) * CHUNK)
        for g in range(G_A):
            cols = slice(g * D_GROUP_A, (g + 1) * D_GROUP_A)
            s = jnp.dot(wsp_ref[g], vnb[rows, cols], preferred_element_type=F32) + bsp_ref[:, cols]
            ya_ref[rows, cols] = z_ref[rows, c_u + g * D_GROUP_A:c_u + (g + 1) * D_GROUP_A] * s.astype(BF16)

    q = z_ref[:, c_c:c_c + D_CONV].astype(F32) * z_ref[:, c_x:c_x + D_CONV].astype(F32)
    q1 = pltpu.roll(q, 1, 0)
    q2 = pltpu.roll(q, 2, 0)
    row = lax.broadcasted_iota(jnp.int32, q.shape, 0)
    cw0, cw1, cw2 = cw_ref[0:1, :], cw_ref[1:2, :], cw_ref[2:3, :]

    seq_start = i % tiles_per_seq == 0
    p6 = jnp.where(seq_start, 0.0, qprev_ref[6:7, :])
    p7 = jnp.where(seq_start, 0.0, qprev_ref[7:8, :])
    f = f_ref[...]
    pos_in_seq = jnp.where(is_prompt, row, row & (SAMPLE_SEQ - 1))
    fix1 = jnp.where(is_prompt, jnp.broadcast_to(p7, q.shape), pltpu.roll(f, TM - 1, 0))
    fix2 = jnp.where(is_prompt, jnp.where(row == 0, p6, p7), f)
    a1 = jnp.where(pos_in_seq == 0, fix1, q1)
    a2 = jnp.where(pos_in_seq < 2, fix2, q2)
    yc = cw0 * a2 + cw1 * a1 + cw2 * q
    qprev_ref[...] = q[TM - 8:TM, :]
    qp_ref[...] = q[TM - (CONV_W - 1):TM, :]
    vnp_ref[...] = vn[TM - CHUNK:TM, :]
    qs_ref[...] = q
    vns_ref[...] = vn

    yb_ref[...] = z_ref[:, c_b:c_b + D_CONV] * yc.astype(BF16)

    width = D_MODEL // MERGE_CHUNKS
    for c in range(MERGE_CHUNKS):
        cols = slice(c * width, (c + 1) * width)
        a = jnp.dot(ya_ref[...], wa_ref[:, cols], preferred_element_type=F32)
        b = jnp.dot(yb_ref[...], wb_ref[:, cols], preferred_element_type=F32)
        ga = z_ref[:, c_ga + c * width:c_ga + (c + 1) * width].astype(F32)
        gb = z_ref[:, c_gb + c * width:c_gb + (c + 1) * width].astype(F32)
        m_ref[:, cols] = (_sigmoid(ga) * a + _sigmoid(gb) * b).astype(BF16)

    x = jnp.where(is_prompt, xp_ref[...], xs_ref[...])
    h = x + jnp.dot(m_ref[...], wo_ref[...], preferred_element_type=F32)
    h_ref[...] = h
    xm = _rms(h, gm_ref[...])
    _store_token_major(xm_ref, xm)
    lg = jnp.dot(xm.astype(BF16), wr_ref[...], preferred_element_type=F32) + br_ref[...]
    route_ref[...] = _route(lg)


def _mixer(z, xp, xs, f, lng, lnb, cw, wsp, bsp, wa, wb, wo, gm, wr, br, *, n_batch):
    tp, ts = xp.shape[0], xs.shape[0]
    t = tp + ts
    npt, nst = tp // TM, ts // TM
    tiles_per_seq = tp // n_batch // TM
    once = pl.Buffered(1)

    def pidx(i):
        return jnp.minimum(i, npt - 1)

    def sidx(i):
        return jnp.maximum(i - npt, 0)

    def seq_or_spare(i):
        return jnp.where(i < npt, i // tiles_per_seq, n_batch)

    const = lambda shape: pl.BlockSpec(shape, lambda i: (0,) * len(shape), pipeline_mode=once)
    in_specs = [
        pl.BlockSpec((TM, D_IN_TOTAL), lambda i: (i, 0)),
        pl.BlockSpec((TM, D_MODEL), lambda i: (pidx(i), 0)),
        pl.BlockSpec((TM, D_MODEL), lambda i: (sidx(i), 0)),
        pl.BlockSpec((TM, D_CONV), lambda i: (sidx(i), 0)),
        const((1, D_HALF)), const((1, D_HALF)), const((CONV_W, D_CONV)),
        pl.BlockSpec((None, G_A, CHUNK, CHUNK), lambda i: (jnp.where(i < npt, 0, 1), 0, 0, 0)),
        pl.BlockSpec((None, CHUNK, D_HALF), lambda i: (jnp.where(i < npt, 0, 1), 0, 0)),
        const((D_HALF, D_MODEL)), const((D_CONV, D_MODEL)), const((D_MODEL, D_MODEL)),
        const((1, D_MODEL)), const((D_MODEL, LANES)), const((1, LANES)),
    ]
    out_specs = [
        pl.BlockSpec((TM, D_MODEL), lambda i: (i, 0)),
        pl.BlockSpec((TM * ROWS, LANES), lambda i: (i, 0)),
        pl.BlockSpec((TM, LANES), lambda i: (i, 0)),
        pl.BlockSpec((None, CHUNK, D_HALF), lambda i: (seq_or_spare(i), 0, 0)),
        pl.BlockSpec((TM, D_HALF), lambda i: (sidx(i), 0)),
        pl.BlockSpec((None, CONV_W - 1, D_CONV), lambda i: (seq_or_spare(i), 0, 0)),
        pl.BlockSpec((TM, D_CONV), lambda i: (sidx(i), 0)),
    ]
    out_shape = [
        jax.ShapeDtypeStruct((t, D_MODEL), F32),
        jax.ShapeDtypeStruct((t * ROWS, LANES), PACKED),
        jax.ShapeDtypeStruct((t, LANES), F32),
        jax.ShapeDtypeStruct((n_batch + 1, CHUNK, D_HALF), F32),
        jax.ShapeDtypeStruct((ts, D_HALF), F32),
        jax.ShapeDtypeStruct((n_batch + 1, CONV_W - 1, D_CONV), F32),
        jax.ShapeDtypeStruct((ts, D_CONV), F32),
    ]
    return pl.pallas_call(
        functools.partial(_mixer_kernel, n_prompt_tiles=npt, tiles_per_seq=tiles_per_seq),
        grid=(npt + nst,),
        in_specs=in_specs,
        out_specs=out_specs,
        out_shape=out_shape,
        scratch_shapes=[
            pltpu.VMEM((8, D_CONV), F32),
            pltpu.VMEM((TM, D_HALF), BF16),
            pltpu.VMEM((TM, D_CONV), BF16),
            pltpu.VMEM((TM, D_MODEL), BF16),
        ],
        compiler_params=pltpu.CompilerParams(
            dimension_semantics=("arbitrary",), vmem_limit_bytes=VMEM_LIMIT),
        name="mixer",
    )(z, xp, xs, f, lng, lnb, cw, wsp, bsp, wa, wb, wo, gm, wr, br)


def _dispatch_kernel(pos_ref, cnt_ref, off_ref, nused_ref, xm_hbm, xs_hbm, inbuf, zbuf, isem, rsem, zsem,
                     *, n_steps, n_tiles):
    zbuf[...] = _packed_zeros(zbuf.shape)
    nused = nused_ref[0]
    pad_bits = TM_E.bit_length() - 1

    def pad_copies(e, fn):
        cnt = cnt_ref[e]
        npad = (TM_E - cnt % TM_E) % TM_E
        start = off_ref[e] + cnt
        for b in range(pad_bits):
            size = 1 << b
            at = start + ((npad >> (b + 1)) << (b + 1))

            @pl.when(((npad >> b) & 1) == 1)
            def _():
                fn(pltpu.make_async_copy(zbuf.at[pl.ds(0, size * ROWS), :],
                                         xs_hbm.at[pl.ds(at * ROWS, size * ROWS), :], zsem))

    def tile_copy(tile):
        return pltpu.make_async_copy(zbuf, xs_hbm.at[pl.ds(tile * TM_E * ROWS, TM_E * ROWS), :], zsem)

    def loop(lo, hi, body):
        lax.fori_loop(lo, hi, lambda k, c: (body(k), c)[1], 0)

    loop(0, N_EXPERTS, lambda e: pad_copies(e, lambda cp: cp.start()))
    loop(nused, n_tiles, lambda tile: tile_copy(tile).start())

    def in_copy(step, slot):
        return pltpu.make_async_copy(xm_hbm.at[pl.ds(step * COPY_TILE * ROWS, COPY_TILE * ROWS), :],
                                     inbuf.at[slot], isem.at[slot])

    def rows_wait(slot):
        for _ in range(2):
            pltpu.make_async_copy(inbuf.at[slot], xs_hbm.at[pl.ds(0, COPY_TILE * ROWS), :], rsem.at[slot]).wait()

    in_copy(0, 0).start()

    def step(s):
        slot = s % DISPATCH_SLOTS
        nxt = (s + 1) % DISPATCH_SLOTS

        @pl.when(s >= DISPATCH_SLOTS - 1)
        def _():
            rows_wait(nxt)

        @pl.when(s + 1 < n_steps)
        def _():
            in_copy(s + 1, nxt).start()

        in_copy(s, slot).wait()
        base = s * COPY_TILE * 2

        def body(r, carry):
            for k in range(2):
                p = pos_ref[base + 2 * r + k]
                pltpu.make_async_copy(inbuf.at[slot, pl.ds(r * ROWS, ROWS), :],
                                      xs_hbm.at[pl.ds(p * ROWS, ROWS), :], rsem.at[slot]).start(priority=k)
            return carry

        lax.fori_loop(0, COPY_TILE, body, 0, unroll=4)

    loop(0, n_steps, step)
    for back in range(DISPATCH_SLOTS - 1, 0, -1):
        rows_wait((n_steps - back) % DISPATCH_SLOTS)
    loop(0, N_EXPERTS, lambda e: pad_copies(e, lambda cp: cp.wait()))
    loop(nused, n_tiles, lambda tile: tile_copy(tile).wait())


def _dispatch(pos, cnt, off, nused, xm, *, n_tiles):
    n_tokens = xm.shape[0] // ROWS
    assert n_tokens % COPY_TILE == 0 and n_tokens // COPY_TILE >= DISPATCH_SLOTS
    smem = pl.BlockSpec(memory_space=pltpu.SMEM)
    return pl.pallas_call(
        functools.partial(_dispatch_kernel, n_steps=n_tokens // COPY_TILE, n_tiles=n_tiles),
        in_specs=[smem, smem, smem, smem, pl.BlockSpec(memory_space=pl.ANY)],
        out_specs=pl.BlockSpec(memory_space=pl.ANY),
        out_shape=jax.ShapeDtypeStruct((n_tiles * TM_E * ROWS, LANES), PACKED),
        scratch_shapes=[
            pltpu.VMEM((DISPATCH_SLOTS, COPY_TILE * ROWS, LANES), PACKED),
            pltpu.VMEM((TM_E * ROWS, LANES), PACKED),
            pltpu.SemaphoreType.DMA((DISPATCH_SLOTS,)),
            pltpu.SemaphoreType.DMA((DISPATCH_SLOTS,)),
            pltpu.SemaphoreType.DMA(()),
        ],
        compiler_params=pltpu.CompilerParams(vmem_limit_bytes=VMEM_LIMIT),
        name="dispatch",
    )(pos, cnt, off, nused, xm)


def _expert_kernel(te_ref, nused_ref, xs_ref, wg_hbm, wu_hbm, wd_hbm, y_ref,
                   wgf, wuf, wdf, wsem, seen_ref):
    i = pl.program_id(0)
    nused = nused_ref[0]

    def weight_copies(e, slot):
        return [pltpu.make_async_copy(src.at[0, e], dst.at[slot], wsem.at[slot, k])
                for k, (src, dst) in enumerate(((wg_hbm, wgf), (wu_hbm, wuf), (wd_hbm, wdf)))]

    @pl.when(i == 0)
    def _():
        seen_ref[0] = 0
        for cp in weight_copies(te_ref[0], 0):
            cp.start()

    @pl.when(i < nused)
    def _():
        e = te_ref[i]
        new_expert = jnp.logical_or(i == 0, e != te_ref[jnp.maximum(i - 1, 0)])

        @pl.when(new_expert)
        def _():
            slot = seen_ref[0] % 2
            seen_ref[0] = seen_ref[0] + 1
            seen_ref[1] = slot
            nxt = lax.while_loop(lambda j: jnp.logical_and(j < nused, te_ref[jnp.minimum(j, nused - 1)] == e),
                                 lambda j: j + 1, i + 1)

            @pl.when(nxt < nused)
            def _():
                for cp in weight_copies(te_ref[jnp.minimum(nxt, nused - 1)], 1 - slot):
                    cp.start(priority=1)

            for cp in weight_copies(e, slot):
                cp.wait()

        cur = seen_ref[1]
        x = _load_token_major(xs_ref, TM_E).astype(BF16)
        g = jnp.dot(x, wgf[cur].astype(BF16), preferred_element_type=F32)
        u = jnp.dot(x, wuf[cur].astype(BF16), preferred_element_type=F32)
        hid = (g * _sigmoid(g)) * u
        _store_token_major(y_ref, jnp.dot(hid.astype(BF16), wdf[cur].astype(BF16), preferred_element_type=F32))

    @pl.when(i >= nused)
    def _():
        y_ref[...] = _packed_zeros(y_ref.shape)


def _experts(te, nused, xs, wg, wu, wd, *, n_tiles):
    grid_spec = pltpu.PrefetchScalarGridSpec(
        num_scalar_prefetch=2,
        grid=(n_tiles,),
        in_specs=[
            pl.BlockSpec((TM_E * ROWS, LANES), lambda i, te, nu: (jnp.minimum(i, nu[0] - 1), 0)),
            pl.BlockSpec(memory_space=pl.ANY),
            pl.BlockSpec(memory_space=pl.ANY),
            pl.BlockSpec(memory_space=pl.ANY),
        ],
        out_specs=pl.BlockSpec((TM_E * ROWS, LANES), lambda i, te, nu: (i, 0)),
        scratch_shapes=[
            pltpu.VMEM((2, D_MODEL, D_EXPERT), F32),
            pltpu.VMEM((2, D_MODEL, D_EXPERT), F32),
            pltpu.VMEM((2, D_EXPERT, D_MODEL), F32),
            pltpu.SemaphoreType.DMA((2, 3)),
            pltpu.SMEM((2,), jnp.int32),
        ],
    )
    return pl.pallas_call(
        _expert_kernel,
        grid_spec=grid_spec,
        out_shape=jax.ShapeDtypeStruct((n_tiles * TM_E * ROWS, LANES), PACKED),
        compiler_params=pltpu.CompilerParams(
            dimension_semantics=("arbitrary",), vmem_limit_bytes=VMEM_LIMIT),
        name="experts",
    )(te, nused, xs, wg, wu, wd)


def _tail_kernel(pos_ref, ys_hbm, h_ref, route_ref, pp_ref, ps_ref, gp_ref, wpg_ref, wpp_ref, gf_ref,
                 yp_ref, ysm_ref, ybuf, sem, h2_ref, a_ref, proj_ref, *, n_prompt_tiles, n_tiles):
    i = pl.program_id(0)
    slot = i % 2
    is_prompt = i < n_prompt_tiles

    def issue(tile, dst_slot):
        base = tile * TM * 2

        def body(r, carry):
            for k in range(2):
                p = pos_ref[base + 2 * r + k]
                pltpu.make_async_copy(ys_hbm.at[pl.ds(p * ROWS, ROWS), :],
                                      ybuf.at[dst_slot, k, pl.ds(r * ROWS, ROWS), :],
                                      sem.at[dst_slot]).start()
            return carry

        lax.fori_loop(0, TM, body, 0, unroll=4)

    def slot_wait(s):
        for k in range(2):
            pltpu.make_async_copy(ys_hbm.at[pl.ds(0, TM * ROWS), :], ybuf.at[s, k], sem.at[s]).wait()

    def row_sumsq(acc):
        return jnp.sum(acc, axis=-1, keepdims=True) * (1.0 / D_MODEL)

    @pl.when(i == 0)
    def _():
        issue(0, 0)

    slot_wait(slot)

    w0 = jnp.broadcast_to(route_ref[:, 2:3], (TM, LANES))
    w1 = jnp.broadcast_to(route_ref[:, 3:4], (TM, LANES))
    acc = jnp.zeros((TM, LANES), F32)
    for s in range(ROWS):
        word0 = ybuf[slot, 0, pl.ds(s, TM, stride=ROWS), :]
        word1 = ybuf[slot, 1, pl.ds(s, TM, stride=ROWS), :]
        for idx in range(2):
            cols = slice((idx * ROWS + s) * LANES, (idx * ROWS + s + 1) * LANES)
            y0 = pltpu.unpack_elementwise(word0, index=idx, packed_dtype=BF16, unpacked_dtype=F32)
            y1 = pltpu.unpack_elementwise(word1, index=idx, packed_dtype=BF16, unpacked_dtype=F32)
            h2 = h_ref[:, cols] + (w0 * y0 + w1 * y1)
            h2_ref[:, cols] = h2
            acc = acc + h2 * h2
    r = lax.rsqrt(row_sumsq(acc) + EPS)
    a_ref[...] = (h2_ref[...] * r * gp_ref[...]).astype(BF16)
    p = jnp.where(is_prompt, pp_ref[...], ps_ref[...]).astype(BF16)
    proj_ref[...] = jnp.dot(p, wpp_ref[...], preferred_element_type=F32)

    nxt_base = jnp.minimum(i + 1, n_tiles - 1) * TM * 2
    rows_per_chunk = TM // TAIL_CHUNKS
    width = D_MODEL // TAIL_CHUNKS
    acc = jnp.zeros((TM, LANES), F32)
    for c in range(TAIL_CHUNKS):
        for rr in range(rows_per_chunk):
            row = c * rows_per_chunk + rr
            for k in range(2):
                p_row = pos_ref[nxt_base + 2 * row + k]
                pltpu.make_async_copy(ys_hbm.at[pl.ds(p_row * ROWS, ROWS), :],
                                      ybuf.at[1 - slot, k, pl.ds(row * ROWS, ROWS), :],
                                      sem.at[1 - slot]).start()
        cols = slice(c * width, (c + 1) * width)
        gate = _sigmoid(jnp.dot(a_ref[...], wpg_ref[:, cols], preferred_element_type=F32))
        h3 = h2_ref[:, cols] + gate * proj_ref[:, cols]
        h2_ref[:, cols] = h3
        for j in range(width // LANES):
            part = h3[:, j * LANES:(j + 1) * LANES]
            acc = acc + part * part
    scale = lax.rsqrt(row_sumsq(acc) + EPS)

    @pl.when(is_prompt)
    def _():
        yp_ref[...] = h2_ref[...] * scale * gf_ref[...]

    @pl.when(jnp.logical_not(is_prompt))
    def _():
        ysm_ref[...] = h2_ref[...] * scale * gf_ref[...]

    @pl.when(i == n_tiles - 1)
    def _():
        slot_wait(1 - slot)


def _tail(pos, ys, h, route, pp, ps, gp, wpg, wpp, gf):
    tp, ts = pp.shape[0], ps.shape[0]
    npt, nst = tp // TM, ts // TM
    ple = pp.shape[1]
    once = pl.Buffered(1)

    def pidx(i):
        return jnp.minimum(i, npt - 1)

    def sidx(i):
        return jnp.maximum(i - npt, 0)

    grid_spec = pltpu.PrefetchScalarGridSpec(
        num_scalar_prefetch=1,
        grid=(npt + nst,),
        in_specs=[
            pl.BlockSpec(memory_space=pl.ANY),
            pl.BlockSpec((TM, D_MODEL), lambda i, pos: (i, 0)),
            pl.BlockSpec((TM, LANES), lambda i, pos: (i, 0)),
            pl.BlockSpec((TM, ple), lambda i, pos: (pidx(i), 0)),
            pl.BlockSpec((TM, ple), lambda i, pos: (sidx(i), 0)),
            pl.BlockSpec((1, D_MODEL), lambda i, pos: (0, 0), pipeline_mode=once),
            pl.BlockSpec((D_MODEL, D_MODEL), lambda i, pos: (0, 0), pipeline_mode=once),
            pl.BlockSpec((ple, D_MODEL), lambda i, pos: (0, 0), pipeline_mode=once),
            pl.BlockSpec((1, D_MODEL), lambda i, pos: (0, 0), pipeline_mode=once),
        ],
        out_specs=[
            pl.BlockSpec((TM, D_MODEL), lambda i, pos: (pidx(i), 0)),
            pl.BlockSpec((TM, D_MODEL), lambda i, pos: (sidx(i), 0)),
        ],
        scratch_shapes=[
            pltpu.VMEM((2, 2, TM * ROWS, LANES), PACKED),
            pltpu.SemaphoreType.DMA((2,)),
            pltpu.VMEM((TM, D_MODEL), F32),
            pltpu.VMEM((TM, D_MODEL), BF16),
            pltpu.VMEM((TM, D_MODEL), F32),
        ],
    )
    return pl.pallas_call(
        functools.partial(_tail_kernel, n_prompt_tiles=npt, n_tiles=npt + nst),
        grid_spec=grid_spec,
        out_shape=[jax.ShapeDtypeStruct((tp, D_MODEL), F32), jax.ShapeDtypeStruct((ts, D_MODEL), F32)],
        compiler_params=pltpu.CompilerParams(
            dimension_semantics=("arbitrary",), vmem_limit_bytes=VMEM_LIMIT),
        name="tail",
    )(pos, ys, h, route, pp, ps, gp, wpg, wpp, gf)


def _route_tables(route, n_tiles):
    e = route[:, 0:2].astype(jnp.int32).reshape(-1)
    n_blocks = e.shape[0] // RANK_BLOCK
    blocks = e.reshape(n_blocks, RANK_BLOCK)
    onehot = (blocks[:, :, None] == jnp.arange(N_EXPERTS, dtype=jnp.int32)[None, None, :]).astype(BF16)
    ltri = jnp.tril(jnp.ones((RANK_BLOCK, RANK_BLOCK), BF16), -1)
    within = jnp.einsum("ij,bjk->bik", ltri, onehot, preferred_element_type=F32)
    block_counts = jnp.sum(onehot.astype(F32), axis=1)
    before = jnp.cumsum(block_counts, axis=0) - block_counts
    counts = jnp.sum(block_counts, axis=0).astype(jnp.int32)
    padded = ((counts + TM_E - 1) // TM_E) * TM_E
    ends = jnp.cumsum(padded)
    offs = ends - padded
    pos = jnp.sum((within + before[:, None, :] + offs[None, None, :].astype(F32)) * onehot.astype(F32), axis=-1)
    pos = pos.reshape(-1)
    nused = ends[-1] // TM_E
    tile_id = jnp.arange(n_tiles, dtype=jnp.int32)
    te = jnp.sum((tile_id[:, None] * TM_E >= ends[None, :]).astype(jnp.int32), axis=1)
    te_last = jnp.sum((((nused - 1) * TM_E) >= ends).astype(jnp.int32))
    te = jnp.where(tile_id < nused, te, te_last)
    i32 = lambda a: a.astype(jnp.int32)
    return i32(te), i32(nused).reshape(1), i32(pos), i32(counts), i32(offs)


def kernel(x_prompt, x_sample, state_conv, p_prompt, p_sample, g_mix, w_in, ln_v_g, ln_v_b, w_spatial, b_spatial, conv_w, w_br_a, w_br_b, w_out, g_moe, w_router_group, b_router_group, w_router_expert, b_router_expert, w_exp_gate, w_exp_up, w_exp_down, g_ple, w_ple_gate, w_ple_proj, g_final):
    depth = w_in.shape[0]
    assert depth == 1, "single-layer step only"
    nb, seq, _ = x_prompt.shape
    ns, dseq, _ = x_sample.shape
    assert dseq == SAMPLE_SEQ and seq % TM == 0 and (ns * dseq) % TM_NORM == 0 and (nb * seq) % TM_NORM == 0
    assert (nb * seq + ns * dseq) % TM_IN == 0
    tp, ts = nb * seq, ns * dseq
    t = tp + ts
    xp = x_prompt.reshape(tp, D_MODEL)
    xs = x_sample.reshape(ts, D_MODEL)

    row = lambda a: a.reshape(1, -1)
    tril = jnp.tril(jnp.ones((CHUNK, CHUNK), dtype=bool))
    w_sp = jnp.where(tril[None], w_spatial[0], 0.0)
    eye = jnp.eye(CHUNK // dseq, dtype=F32)
    w_sp_s = jnp.einsum("ij,gts->gitjs", eye, w_sp[:, :dseq, :dseq]).reshape(G_A, CHUNK, CHUNK)
    wsp = jnp.stack([w_sp, w_sp_s]).astype(BF16)
    b_p = jnp.repeat(b_spatial[0].T, D_GROUP_A, axis=1)
    b_s = jnp.tile(b_p[:dseq], (CHUNK // dseq, 1))
    bsp = jnp.stack([b_p, b_s])
    lane_pad = LANES - N_GROUPS - N_EXPERTS
    wr = jnp.concatenate([w_router_group[0], w_router_expert[0], jnp.zeros((D_MODEL, lane_pad), F32)], axis=1)
    br = jnp.concatenate([b_router_group[0], b_router_expert[0], jnp.zeros((lane_pad,), F32)]).reshape(1, LANES)
    f = jnp.pad(state_conv[0], ((0, 0), (0, dseq - (CONV_W - 1)), (0, 0))).reshape(ts, D_CONV)

    n = _norm(xp, xs, row(g_mix[0]))
    z, (wa, wb, wo, wpg) = _inproj(n, w_in, [w_br_a, w_br_b, w_out, w_ple_gate])
    h, xm, route, vnp, vns, qp, qs = _mixer(
        z, xp, xs, f, row(ln_v_g[0]), row(ln_v_b[0]), conv_w[0], wsp, bsp, wa, wb, wo,
        row(g_moe[0]), wr.astype(BF16), br, n_batch=nb)

    n_tiles = (2 * t) // TM_E + N_EXPERTS
    te, nused, pos, counts, offs = _route_tables(route, n_tiles)
    xsort = _dispatch(pos, counts, offs, nused, xm, n_tiles=n_tiles)
    ys = _experts(te, nused, xsort, w_exp_gate, w_exp_up, w_exp_down, n_tiles=n_tiles)

    yp, ysm = _tail(pos, ys, h, route, p_prompt[0].reshape(tp, -1), p_sample[0].reshape(ts, -1),
                    row(g_ple[0]), wpg, w_ple_proj[0].astype(BF16), row(g_final))

    y_prompt = yp.reshape(nb, seq, D_MODEL)
    y_sample = ysm.reshape(ns, dseq, D_MODEL)
    conv_state_prompt = qp[:nb][None]
    conv_state_sample = qs.reshape(ns, dseq, D_CONV)[:, dseq - (CONV_W - 1):][None]
    v_rows_prompt = vnp[:nb][None]
    v_rows_sample = vns.reshape(ns, dseq, D_HALF)[None]
    return (y_prompt, y_sample, conv_state_prompt, conv_state_sample, v_rows_prompt, v_rows_sample)
```

```python
import functools

import jax
import jax.numpy as jnp
from jax import lax
from jax.experimental import pallas as pl
from jax.experimental.pallas import tpu as pltpu

D_MODEL = 2048
D_HALF = D_MODEL // 2
D_CONV = D_MODEL // 2
G_A = 4
D_GROUP_A = D_HALF // G_A
CHUNK = 128
CONV_W = 3
N_GROUPS = 4
EXPERTS_PER_GROUP = 8
N_EXPERTS = N_GROUPS * EXPERTS_PER_GROUP
D_EXPERT = D_MODEL // 4
D_IN_TOTAL = 2 * D_HALF + 3 * D_CONV + 2 * D_MODEL
EPS = 1e-6

BF16 = jnp.bfloat16
F32 = jnp.float32

LANES = 128
VMEM_LIMIT = 60 * 1024 * 1024

TM_NORM = 512
TM_IN = 1024
CAST_ROWS = 128
TN_IN = 1024
TM = 256
TM_E = 256
ROWS = D_MODEL // (2 * LANES)
PACKED = jnp.uint32
COPY_TILE = 256
DISPATCH_SLOTS = 3
TAIL_CHUNKS = 8
MERGE_CHUNKS = 8
SAMPLE_SEQ = 8
RANK_BLOCK = 256
WEIGHT_SLOTS = 3


def _rms(x, g):
    return x * lax.rsqrt(jnp.mean(x * x, axis=-1, keepdims=True) + EPS) * g


def _sigmoid(x):
    return 1.0 / (1.0 + jnp.exp(-x))


def _store_token_major(ref, x):
    m = x.shape[0]
    for s in range(ROWS):
        lo = x[:, s * LANES:(s + 1) * LANES]
        hi = x[:, (ROWS + s) * LANES:(ROWS + s + 1) * LANES]
        ref[pl.ds(s, m, stride=ROWS), :] = pltpu.pack_elementwise([lo, hi], packed_dtype=BF16)


def _packed_zeros(shape):
    zero = jnp.zeros(shape, F32)
    return pltpu.pack_elementwise([zero, zero], packed_dtype=BF16)


def _unpack_words(words):
    half = lambda idx: [pltpu.unpack_elementwise(w, index=idx, packed_dtype=BF16, unpacked_dtype=F32)
                        for w in words]
    return jnp.concatenate(half(0) + half(1), axis=1)


def _load_token_major(ref, m):
    return _unpack_words([ref[pl.ds(s, m, stride=ROWS), :] for s in range(ROWS)])


def _norm_kernel(xp_ref, xs_ref, g_ref, o_ref, *, n_prompt_tiles):
    i = pl.program_id(0)

    def run(x_ref):
        o_ref[...] = _rms(x_ref[...], g_ref[...]).astype(BF16)

    @pl.when(i < n_prompt_tiles)
    def _():
        run(xp_ref)

    @pl.when(i >= n_prompt_tiles)
    def _():
        run(xs_ref)


def _norm(xp, xs, g):
    tp, ts = xp.shape[0], xs.shape[0]
    npt, nst = tp // TM_NORM, ts // TM_NORM
    return pl.pallas_call(
        functools.partial(_norm_kernel, n_prompt_tiles=npt),
        grid=(npt + nst,),
        in_specs=[
            pl.BlockSpec((TM_NORM, D_MODEL), lambda i: (jnp.minimum(i, npt - 1), 0)),
            pl.BlockSpec((TM_NORM, D_MODEL), lambda i: (jnp.maximum(i - npt, 0), 0)),
            pl.BlockSpec((1, D_MODEL), lambda i: (0, 0)),
        ],
        out_specs=pl.BlockSpec((TM_NORM, D_MODEL), lambda i: (i, 0)),
        out_shape=jax.ShapeDtypeStruct((tp + ts, D_MODEL), BF16),
        compiler_params=pltpu.CompilerParams(
            dimension_semantics=("arbitrary",), vmem_limit_bytes=VMEM_LIMIT),
        name="norm",
    )(xp, xs, g)


def _inproj_kernel(n_ref, w_ref, *refs, n_side):
    side_in, z_ref, side_out = refs[:n_side], refs[n_side], refs[n_side + 1:]

    z_ref[...] = jnp.dot(n_ref[...], w_ref[...].astype(BF16), preferred_element_type=F32).astype(BF16)
    for src, dst in zip(side_in, side_out):
        dst[...] = src[...].astype(BF16)


def _inproj(n, w_in, side_weights):
    t = n.shape[0]
    n_j, n_i = D_IN_TOTAL // TN_IN, t // TM_IN
    starts, start = [], 0
    for w in side_weights:
        assert w.shape[1] % CAST_ROWS == 0
        starts.append(start)
        start += w.shape[1] // CAST_ROWS
    assert start <= n_j * n_i, "not enough grid steps for the weight casts"

    def side_index(k):
        n_blk = side_weights[k].shape[1] // CAST_ROWS
        return lambda j, i: jnp.clip(j * n_i + i - starts[k], 0, n_blk - 1)

    side_in = [pl.BlockSpec((None, CAST_ROWS, w.shape[2]), lambda j, i, f=side_index(k): (0, f(j, i), 0))
               for k, w in enumerate(side_weights)]
    side_out = [pl.BlockSpec((CAST_ROWS, w.shape[2]), lambda j, i, f=side_index(k): (f(j, i), 0))
                for k, w in enumerate(side_weights)]
    outs = pl.pallas_call(
        functools.partial(_inproj_kernel, n_side=len(side_weights)),
        grid=(n_j, n_i),
        in_specs=[
            pl.BlockSpec((TM_IN, D_MODEL), lambda j, i: (i, 0)),
            pl.BlockSpec((None, D_MODEL, TN_IN), lambda j, i: (0, 0, j)),
        ] + side_in,
        out_specs=[pl.BlockSpec((TM_IN, TN_IN), lambda j, i: (i, j))] + side_out,
        out_shape=[jax.ShapeDtypeStruct((t, D_IN_TOTAL), BF16)]
        + [jax.ShapeDtypeStruct(w.shape[1:], BF16) for w in side_weights],
        compiler_params=pltpu.CompilerParams(
            dimension_semantics=("arbitrary", "arbitrary"), vmem_limit_bytes=VMEM_LIMIT),
        name="in_proj",
    )(n, w_in, *side_weights)
    return outs[0], outs[1:]


def _route(lg):
    shape = lg.shape
    lane = lax.broadcasted_iota(jnp.int32, shape, 1)
    lanef = lane.astype(F32)
    neg = jnp.float32(-jnp.inf)
    far = jnp.float32(LANES)
    is_g = lane < N_GROUPS
    lgm = jnp.where(is_g, lg, neg)
    gmax = jnp.max(lgm, axis=-1, keepdims=True)
    gsel = jnp.min(jnp.where(lgm == gmax, lanef, far), axis=-1, keepdims=True)
    p_group = 1.0 / jnp.sum(jnp.where(is_g, jnp.exp(lgm - gmax), 0.0), axis=-1, keepdims=True)
    lane_group = ((lane - N_GROUPS) >> (EXPERTS_PER_GROUP.bit_length() - 1)).astype(F32)
    is_e = lane_group == gsel
    le = jnp.where(is_e, lg, neg)
    m1 = jnp.max(le, axis=-1, keepdims=True)
    i1 = jnp.min(jnp.where(le == m1, lanef, far), axis=-1, keepdims=True)
    le2 = jnp.where(lanef == i1, neg, le)
    m2 = jnp.max(le2, axis=-1, keepdims=True)
    i2 = jnp.min(jnp.where(le2 == m2, lanef, far), axis=-1, keepdims=True)
    e21 = jnp.exp(m2 - m1)
    w1 = 1.0 / (1.0 + e21)
    w2 = e21 * w1
    out = jnp.where(lane == 0, i1 - N_GROUPS,
                    jnp.where(lane == 1, i2 - N_GROUPS,
                              jnp.where(lane == 2, p_group * w1,
                                        jnp.where(lane == 3, p_group * w2, 0.0))))
    return out


def _mixer_kernel(z_ref, xp_ref, xs_ref, f_ref, lng_ref, lnb_ref, cw_ref, wsp_ref, bsp_ref,
                  wa_ref, wb_ref, wo_ref, gm_ref, wr_ref, br_ref,
                  h_ref, xm_ref, route_ref, vnp_ref, vns_ref, qp_ref, qs_ref,
                  qprev_ref, ya_ref, yb_ref, m_ref, *, n_prompt_tiles, tiles_per_seq):
    i = pl.program_id(0)
    is_prompt = i < n_prompt_tiles
    c_u, c_v, c_b, c_c, c_x = (k * D_HALF for k in range(5))
    c_ga = c_x + D_CONV
    c_gb = c_ga + D_MODEL

    v = z_ref[:, c_v:c_v + D_HALF].astype(F32)
    mu = jnp.mean(v, axis=-1, keepdims=True)
    vc = v - mu
    vn = vc * lax.rsqrt(jnp.mean(vc * vc, axis=-1, keepdims=True) + EPS) * lng_ref[...] + lnb_ref[...]
    vnb = vn.astype(BF16)
    for c in range(TM // CHUNK):
        rows = slice(c * CHUNK, (c + 1) * CHUNK)
        for g in range(G_A):
            cols = slice(g * D_GROUP_A, (g + 1) * D_GROUP_A)
            s = jnp.dot(wsp_ref[g], vnb[rows, cols], preferred_element_type=F32) + bsp_ref[:, cols]
            u = z_ref[rows, c_u + g * D_GROUP_A:c_u + (g + 1) * D_GROUP_A].astype(F32)
            ya_ref[rows, cols] = (u * s).astype(BF16)

    q = z_ref[:, c_c:c_c + D_CONV].astype(F32) * z_ref[:, c_x:c_x + D_CONV].astype(F32)
    q1 = pltpu.roll(q, 1, 0)
    q2 = pltpu.roll(q, 2, 0)
    row = lax.broadcasted_iota(jnp.int32, q.shape, 0)
    cw0, cw1, cw2 = cw_ref[0:1, :], cw_ref[1:2, :], cw_ref[2:3, :]

    seq_start = i % tiles_per_seq == 0
    p6 = jnp.where(seq_start, 0.0, qprev_ref[6:7, :])
    p7 = jnp.where(seq_start, 0.0, qprev_ref[7:8, :])
    f = f_ref[...]
    pos_in_seq = jnp.where(is_prompt, row, row & (SAMPLE_SEQ - 1))
    fix1 = jnp.where(is_prompt, jnp.broadcast_to(p7, q.shape), pltpu.roll(f, TM - 1, 0))
    fix2 = jnp.where(is_prompt, jnp.where(row == 0, p6, p7), f)
    a1 = jnp.where(pos_in_seq == 0, fix1, q1)
    a2 = jnp.where(pos_in_seq < 2, fix2, q2)
    yc = cw0 * a2 + cw1 * a1 + cw2 * q
    qprev_ref[...] = q[TM - 8:TM, :]
    qp_ref[...] = q[TM - (CONV_W - 1):TM, :]
    vnp_ref[...] = vn[TM - CHUNK:TM, :]
    qs_ref[...] = q
    vns_ref[...] = vn

    yb_ref[...] = (z_ref[:, c_b:c_b + D_CONV].astype(F32) * yc).astype(BF16)

    width = D_MODEL // MERGE_CHUNKS
    for c in range(MERGE_CHUNKS):
        cols = slice(c * width, (c + 1) * width)
        a = jnp.dot(ya_ref[...], wa_ref[:, cols], preferred_element_type=F32)
        b = jnp.dot(yb_ref[...], wb_ref[:, cols], preferred_element_type=F32)
        ga = z_ref[:, c_ga + c * width:c_ga + (c + 1) * width].astype(F32)
        gb = z_ref[:, c_gb + c * width:c_gb + (c + 1) * width].astype(F32)
        m_ref[:, cols] = (_sigmoid(ga) * a + _sigmoid(gb) * b).astype(BF16)

    x = jnp.where(is_prompt, xp_ref[...], xs_ref[...])
    h = x + jnp.dot(m_ref[...], wo_ref[...], preferred_element_type=F32)
    h_ref[...] = h
    xm = _rms(h, gm_ref[...])
    _store_token_major(xm_ref, xm)
    lg = jnp.dot(xm.astype(BF16), wr_ref[...], preferred_element_type=F32) + br_ref[...]
    route_ref[...] = _route(lg)


def _mixer(z, xp, xs, f, lng, lnb, cw, wsp, bsp, wa, wb, wo, gm, wr, br, *, n_batch):
    tp, ts = xp.shape[0], xs.shape[0]
    t = tp + ts
    npt, nst = tp // TM, ts // TM
    tiles_per_seq = tp // n_batch // TM
    once = pl.Buffered(1)

    def pidx(i):
        return jnp.minimum(i, npt - 1)

    def sidx(i):
        return jnp.maximum(i - npt, 0)

    def seq_or_spare(i):
        return jnp.where(i < npt, i // tiles_per_seq, n_batch)

    const = lambda shape: pl.BlockSpec(shape, lambda i: (0,) * len(shape), pipeline_mode=once)
    in_specs = [
        pl.BlockSpec((TM, D_IN_TOTAL), lambda i: (i, 0)),
        pl.BlockSpec((TM, D_MODEL), lambda i: (pidx(i), 0)),
        pl.BlockSpec((TM, D_MODEL), lambda i: (sidx(i), 0)),
        pl.BlockSpec((TM, D_CONV), lambda i: (sidx(i), 0)),
        const((1, D_HALF)), const((1, D_HALF)), const((CONV_W, D_CONV)),
        pl.BlockSpec((None, G_A, CHUNK, CHUNK), lambda i: (jnp.where(i < npt, 0, 1), 0, 0, 0)),
        pl.BlockSpec((None, CHUNK, D_HALF), lambda i: (jnp.where(i < npt, 0, 1), 0, 0)),
        const((D_HALF, D_MODEL)), const((D_CONV, D_MODEL)), const((D_MODEL, D_MODEL)),
        const((1, D_MODEL)), const((D_MODEL, LANES)), const((1, LANES)),
    ]
    out_specs = [
        pl.BlockSpec((TM, D_MODEL), lambda i: (i, 0)),
        pl.BlockSpec((TM * ROWS, LANES), lambda i: (i, 0)),
        pl.BlockSpec((TM, LANES), lambda i: (i, 0)),
        pl.BlockSpec((None, CHUNK, D_HALF), lambda i: (seq_or_spare(i), 0, 0)),
        pl.BlockSpec((TM, D_HALF), lambda i: (sidx(i), 0)),
        pl.BlockSpec((None, CONV_W - 1, D_CONV), lambda i: (seq_or_spare(i), 0, 0)),
        pl.BlockSpec((TM, D_CONV), lambda i: (sidx(i), 0)),
    ]
    out_shape = [
        jax.ShapeDtypeStruct((t, D_MODEL), F32),
        jax.ShapeDtypeStruct((t * ROWS, LANES), PACKED),
        jax.ShapeDtypeStruct((t, LANES), F32),
        jax.ShapeDtypeStruct((n_batch + 1, CHUNK, D_HALF), F32),
        jax.ShapeDtypeStruct((ts, D_HALF), F32),
        jax.ShapeDtypeStruct((n_batch + 1, CONV_W - 1, D_CONV), F32),
        jax.ShapeDtypeStruct((ts, D_CONV), F32),
    ]
    return pl.pallas_call(
        functools.partial(_mixer_kernel, n_prompt_tiles=npt, tiles_per_seq=tiles_per_seq),
        grid=(npt + nst,),
        in_specs=in_specs,
        out_specs=out_specs,
        out_shape=out_shape,
        scratch_shapes=[
            pltpu.VMEM((8, D_CONV), F32),
            pltpu.VMEM((TM, D_HALF), BF16),
            pltpu.VMEM((TM, D_CONV), BF16),
            pltpu.VMEM((TM, D_MODEL), BF16),
        ],
        compiler_params=pltpu.CompilerParams(
            dimension_semantics=("arbitrary",), vmem_limit_bytes=VMEM_LIMIT),
        name="mixer",
    )(z, xp, xs, f, lng, lnb, cw, wsp, bsp, wa, wb, wo, gm, wr, br)


def _dispatch_kernel(pos_ref, cnt_ref, off_ref, nused_ref, xm_hbm, xs_hbm, inbuf, zbuf, isem, rsem, zsem,
                     *, n_steps, n_tiles):
    zbuf[...] = _packed_zeros(zbuf.shape)
    nused = nused_ref[0]
    pad_bits = TM_E.bit_length() - 1

    def pad_copies(e, fn):
        cnt = cnt_ref[e]
        npad = (TM_E - cnt % TM_E) % TM_E
        start = off_ref[e] + cnt
        for b in range(pad_bits):
            size = 1 << b
            at = start + ((npad >> (b + 1)) << (b + 1))

            @pl.when(((npad >> b) & 1) == 1)
            def _():
                fn(pltpu.make_async_copy(zbuf.at[pl.ds(0, size * ROWS), :],
                                         xs_hbm.at[pl.ds(at * ROWS, size * ROWS), :], zsem))

    def tile_copy(tile):
        return pltpu.make_async_copy(zbuf, xs_hbm.at[pl.ds(tile * TM_E * ROWS, TM_E * ROWS), :], zsem)

    def loop(lo, hi, body):
        lax.fori_loop(lo, hi, lambda k, c: (body(k), c)[1], 0)

    loop(0, N_EXPERTS, lambda e: pad_copies(e, lambda cp: cp.start()))
    loop(nused, n_tiles, lambda tile: tile_copy(tile).start())

    def in_copy(step, slot):
        return pltpu.make_async_copy(xm_hbm.at[pl.ds(step * COPY_TILE * ROWS, COPY_TILE * ROWS), :],
                                     inbuf.at[slot], isem.at[slot])

    def rows_wait(slot):
        for _ in range(2):
            pltpu.make_async_copy(inbuf.at[slot], xs_hbm.at[pl.ds(0, COPY_TILE * ROWS), :], rsem.at[slot]).wait()

    in_copy(0, 0).start()

    def step(s):
        slot = s % DISPATCH_SLOTS
        nxt = (s + 1) % DISPATCH_SLOTS

        @pl.when(s >= DISPATCH_SLOTS - 1)
        def _():
            rows_wait(nxt)

        @pl.when(s + 1 < n_steps)
        def _():
            in_copy(s + 1, nxt).start()

        in_copy(s, slot).wait()
        base = s * COPY_TILE * 2

        def body(r, carry):
            for k in range(2):
                p = pos_ref[base + 2 * r + k]
                pltpu.make_async_copy(inbuf.at[slot, pl.ds(r * ROWS, ROWS), :],
                                      xs_hbm.at[pl.ds(p * ROWS, ROWS), :], rsem.at[slot]).start(priority=k)
            return carry

        lax.fori_loop(0, COPY_TILE, body, 0, unroll=4)

    loop(0, n_steps, step)
    for back in range(DISPATCH_SLOTS - 1, 0, -1):
        rows_wait((n_steps - back) % DISPATCH_SLOTS)
    loop(0, N_EXPERTS, lambda e: pad_copies(e, lambda cp: cp.wait()))
    loop(nused, n_tiles, lambda tile: tile_copy(tile).wait())


def _dispatch(pos, cnt, off, nused, xm, *, n_tiles):
    n_tokens = xm.shape[0] // ROWS
    assert n_tokens % COPY_TILE == 0 and n_tokens // COPY_TILE >= DISPATCH_SLOTS
    smem = pl.BlockSpec(memory_space=pltpu.SMEM)
    return pl.pallas_call(
        functools.partial(_dispatch_kernel, n_steps=n_tokens // COPY_TILE, n_tiles=n_tiles),
        in_specs=[smem, smem, smem, smem, pl.BlockSpec(memory_space=pl.ANY)],
        out_specs=pl.BlockSpec(memory_space=pl.ANY),
        out_shape=jax.ShapeDtypeStruct((n_tiles * TM_E * ROWS, LANES), PACKED),
        scratch_shapes=[
            pltpu.VMEM((DISPATCH_SLOTS, COPY_TILE * ROWS, LANES), PACKED),
            pltpu.VMEM((TM_E * ROWS, LANES), PACKED),
            pltpu.SemaphoreType.DMA((DISPATCH_SLOTS,)),
            pltpu.SemaphoreType.DMA((DISPATCH_SLOTS,)),
            pltpu.SemaphoreType.DMA(()),
        ],
        compiler_params=pltpu.CompilerParams(vmem_limit_bytes=VMEM_LIMIT),
        name="dispatch",
    )(pos, cnt, off, nused, xm)


def _expert_kernel(te_ref, nused_ref, xs_ref, wg_hbm, wu_hbm, wd_hbm, y_ref,
                   wgf, wuf, wdf, wsem, seen_ref):
    i = pl.program_id(0)
    nused = nused_ref[0]

    def weight_copies(e, slot):
        return [pltpu.make_async_copy(src.at[0, e], dst.at[slot], wsem.at[slot, k])
                for k, (src, dst) in enumerate(((wg_hbm, wgf), (wu_hbm, wuf), (wd_hbm, wdf)))]

    def tile_at(j):
        return te_ref[jnp.minimum(j, nused - 1)]

    def next_expert_tile(start):
        e0 = tile_at(start)
        return lax.while_loop(lambda j: jnp.logical_and(j < nused, tile_at(j) == e0), lambda j: j + 1, start + 1)

    def prefetch(tile, slot):
        @pl.when(tile < nused)
        def _():
            for cp in weight_copies(tile_at(tile), slot):
                cp.start(priority=1)

    @pl.when(i == 0)
    def _():
        seen_ref[0] = 0
        for cp in weight_copies(te_ref[0], 0):
            cp.start()
        prefetch(next_expert_tile(0), 1)

    @pl.when(i < nused)
    def _():
        e = te_ref[i]
        new_expert = jnp.logical_or(i == 0, e != te_ref[jnp.maximum(i - 1, 0)])

        @pl.when(new_expert)
        def _():
            k = seen_ref[0]
            slot = k % WEIGHT_SLOTS
            seen_ref[0] = k + 1
            seen_ref[1] = slot
            prefetch(next_expert_tile(next_expert_tile(i)), (k + 2) % WEIGHT_SLOTS)
            for cp in weight_copies(e, slot):
                cp.wait()

        cur = seen_ref[1]
        x = _load_token_major(xs_ref, TM_E).astype(BF16)
        g = jnp.dot(x, wgf[cur].astype(BF16), preferred_element_type=F32)
        u = jnp.dot(x, wuf[cur].astype(BF16), preferred_element_type=F32)
        hid = (g * _sigmoid(g)) * u
        _store_token_major(y_ref, jnp.dot(hid.astype(BF16), wdf[cur].astype(BF16), preferred_element_type=F32))

    @pl.when(i >= nused)
    def _():
        y_ref[...] = _packed_zeros(y_ref.shape)


def _experts(te, nused, xs, wg, wu, wd, *, n_tiles):
    grid_spec = pltpu.PrefetchScalarGridSpec(
        num_scalar_prefetch=2,
        grid=(n_tiles,),
        in_specs=[
            pl.BlockSpec((TM_E * ROWS, LANES), lambda i, te, nu: (jnp.minimum(i, nu[0] - 1), 0)),
            pl.BlockSpec(memory_space=pl.ANY),
            pl.BlockSpec(memory_space=pl.ANY),
            pl.BlockSpec(memory_space=pl.ANY),
        ],
        out_specs=pl.BlockSpec((TM_E * ROWS, LANES), lambda i, te, nu: (i, 0)),
        scratch_shapes=[
            pltpu.VMEM((WEIGHT_SLOTS, D_MODEL, D_EXPERT), F32),
            pltpu.VMEM((WEIGHT_SLOTS, D_MODEL, D_EXPERT), F32),
            pltpu.VMEM((WEIGHT_SLOTS, D_EXPERT, D_MODEL), F32),
            pltpu.SemaphoreType.DMA((WEIGHT_SLOTS, 3)),
            pltpu.SMEM((2,), jnp.int32),
        ],
    )
    return pl.pallas_call(
        _expert_kernel,
        grid_spec=grid_spec,
        out_shape=jax.ShapeDtypeStruct((n_tiles * TM_E * ROWS, LANES), PACKED),
        compiler_params=pltpu.CompilerParams(
            dimension_semantics=("arbitrary",), vmem_limit_bytes=VMEM_LIMIT),
        name="experts",
    )(te, nused, xs, wg, wu, wd)


def _tail_kernel(pos_ref, ys_hbm, h_ref, route_ref, pp_ref, ps_ref, gp_ref, wpg_ref, wpp_ref, gf_ref,
                 yp_ref, ysm_ref, ybuf, sem, h2_ref, a_ref, proj_ref, *, n_prompt_tiles, n_tiles):
    i = pl.program_id(0)
    slot = i % 2
    is_prompt = i < n_prompt_tiles

    def issue(tile, dst_slot):
        base = tile * TM * 2

        def body(r, carry):
            for k in range(2):
                p = pos_ref[base + 2 * r + k]
                pltpu.make_async_copy(ys_hbm.at[pl.ds(p * ROWS, ROWS), :],
                                      ybuf.at[dst_slot, k, pl.ds(r * ROWS, ROWS), :],
                                      sem.at[dst_slot]).start()
            return carry

        lax.fori_loop(0, TM, body, 0, unroll=4)

    def slot_wait(s):
        for k in range(2):
            pltpu.make_async_copy(ys_hbm.at[pl.ds(0, TM * ROWS), :], ybuf.at[s, k], sem.at[s]).wait()

    def row_sumsq(acc):
        return jnp.sum(acc, axis=-1, keepdims=True) * (1.0 / D_MODEL)

    @pl.when(i == 0)
    def _():
        issue(0, 0)

    slot_wait(slot)

    w0 = jnp.broadcast_to(route_ref[:, 2:3], (TM, LANES))
    w1 = jnp.broadcast_to(route_ref[:, 3:4], (TM, LANES))
    acc = jnp.zeros((TM, LANES), F32)
    for s in range(ROWS):
        word0 = ybuf[slot, 0, pl.ds(s, TM, stride=ROWS), :]
        word1 = ybuf[slot, 1, pl.ds(s, TM, stride=ROWS), :]
        for idx in range(2):
            cols = slice((idx * ROWS + s) * LANES, (idx * ROWS + s + 1) * LANES)
            y0 = pltpu.unpack_elementwise(word0, index=idx, packed_dtype=BF16, unpacked_dtype=F32)
            y1 = pltpu.unpack_elementwise(word1, index=idx, packed_dtype=BF16, unpacked_dtype=F32)
            h2 = h_ref[:, cols] + (w0 * y0 + w1 * y1)
            h2_ref[:, cols] = h2
            acc = acc + h2 * h2
    r = lax.rsqrt(row_sumsq(acc) + EPS)
    a_ref[...] = (h2_ref[...] * r * gp_ref[...]).astype(BF16)
    p = jnp.where(is_prompt, pp_ref[...], ps_ref[...]).astype(BF16)
    proj_ref[...] = jnp.dot(p, wpp_ref[...], preferred_element_type=F32)

    nxt_base = jnp.minimum(i + 1, n_tiles - 1) * TM * 2
    rows_per_chunk = TM // TAIL_CHUNKS
    width = D_MODEL // TAIL_CHUNKS
    acc = jnp.zeros((TM, LANES), F32)
    for c in range(TAIL_CHUNKS):
        for rr in range(rows_per_chunk):
            row = c * rows_per_chunk + rr
            for k in range(2):
                p_row = pos_ref[nxt_base + 2 * row + k]
                pltpu.make_async_copy(ys_hbm.at[pl.ds(p_row * ROWS, ROWS), :],
                                      ybuf.at[1 - slot, k, pl.ds(row * ROWS, ROWS), :],
                                      sem.at[1 - slot]).start()
        cols = slice(c * width, (c + 1) * width)
        gate = _sigmoid(jnp.dot(a_ref[...], wpg_ref[:, cols], preferred_element_type=F32))
        h3 = h2_ref[:, cols] + gate * proj_ref[:, cols]
        h2_ref[:, cols] = h3
        for j in range(width // LANES):
            part = h3[:, j * LANES:(j + 1) * LANES]
            acc = acc + part * part
    scale = lax.rsqrt(row_sumsq(acc) + EPS)

    @pl.when(is_prompt)
    def _():
        yp_ref[...] = h2_ref[...] * scale * gf_ref[...]

    @pl.when(jnp.logical_not(is_prompt))
    def _():
        ysm_ref[...] = h2_ref[...] * scale * gf_ref[...]

    @pl.when(i == n_tiles - 1)
    def _():
        slot_wait(1 - slot)


def _tail(pos, ys, h, route, pp, ps, gp, wpg, wpp, gf):
    tp, ts = pp.shape[0], ps.shape[0]
    npt, nst = tp // TM, ts // TM
    ple = pp.shape[1]
    once = pl.Buffered(1)

    def pidx(i):
        return jnp.minimum(i, npt - 1)

    def sidx(i):
        return jnp.maximum(i - npt, 0)

    grid_spec = pltpu.PrefetchScalarGridSpec(
        num_scalar_prefetch=1,
        grid=(npt + nst,),
        in_specs=[
            pl.BlockSpec(memory_space=pl.ANY),
            pl.BlockSpec((TM, D_MODEL), lambda i, pos: (i, 0)),
            pl.BlockSpec((TM, LANES), lambda i, pos: (i, 0)),
            pl.BlockSpec((TM, ple), lambda i, pos: (pidx(i), 0)),
            pl.BlockSpec((TM, ple), lambda i, pos: (sidx(i), 0)),
            pl.BlockSpec((1, D_MODEL), lambda i, pos: (0, 0), pipeline_mode=once),
            pl.BlockSpec((D_MODEL, D_MODEL), lambda i, pos: (0, 0), pipeline_mode=once),
            pl.BlockSpec((ple, D_MODEL), lambda i, pos: (0, 0), pipeline_mode=once),
            pl.BlockSpec((1, D_MODEL), lambda i, pos: (0, 0), pipeline_mode=once),
        ],
        out_specs=[
            pl.BlockSpec((TM, D_MODEL), lambda i, pos: (pidx(i), 0)),
            pl.BlockSpec((TM, D_MODEL), lambda i, pos: (sidx(i), 0)),
        ],
        scratch_shapes=[
            pltpu.VMEM((2, 2, TM * ROWS, LANES), PACKED),
            pltpu.SemaphoreType.DMA((2,)),
            pltpu.VMEM((TM, D_MODEL), F32),
            pltpu.VMEM((TM, D_MODEL), BF16),
            pltpu.VMEM((TM, D_MODEL), F32),
        ],
    )
    return pl.pallas_call(
        functools.partial(_tail_kernel, n_prompt_tiles=npt, n_tiles=npt + nst),
        grid_spec=grid_spec,
        out_shape=[jax.ShapeDtypeStruct((tp, D_MODEL), F32), jax.ShapeDtypeStruct((ts, D_MODEL), F32)],
        compiler_params=pltpu.CompilerParams(
            dimension_semantics=("arbitrary",), vmem_limit_bytes=VMEM_LIMIT),
        name="tail",
    )(pos, ys, h, route, pp, ps, gp, wpg, wpp, gf)


def _route_tables(route, n_tiles):
    e = route[:, 0:2].astype(jnp.int32).reshape(-1)
    n_blocks = e.shape[0] // RANK_BLOCK
    onehot = (e[:, None] == jnp.arange(N_EXPERTS, dtype=jnp.int32)[None, :]).astype(BF16)
    onehot = onehot.reshape(n_blocks, RANK_BLOCK, N_EXPERTS)
    ltri = jnp.tril(jnp.ones((RANK_BLOCK, RANK_BLOCK), BF16), -1)
    within = jnp.einsum("ij,bjk->bik", ltri, onehot, preferred_element_type=F32)
    block_counts = jnp.sum(onehot.astype(F32), axis=1)
    before = jnp.cumsum(block_counts, axis=0) - block_counts
    counts = jnp.sum(block_counts, axis=0).astype(jnp.int32)
    padded = ((counts + TM_E - 1) // TM_E) * TM_E
    ends = jnp.cumsum(padded)
    offs = ends - padded
    pos = jnp.sum((within + before[:, None, :] + offs[None, None, :].astype(F32)) * onehot.astype(F32), axis=-1)
    pos = pos.reshape(-1)
    nused = ends[-1] // TM_E
    tile_id = jnp.arange(n_tiles, dtype=jnp.int32)
    te = jnp.sum((tile_id[:, None] * TM_E >= ends[None, :]).astype(jnp.int32), axis=1)
    te_last = jnp.sum((((nused - 1) * TM_E) >= ends).astype(jnp.int32))
    te = jnp.where(tile_id < nused, te, te_last)
    i32 = lambda a: a.astype(jnp.int32)
    return i32(te), i32(nused).reshape(1), i32(pos), i32(counts), i32(offs)


def kernel(x_prompt, x_sample, state_conv, p_prompt, p_sample, g_mix, w_in, ln_v_g, ln_v_b, w_spatial, b_spatial, conv_w, w_br_a, w_br_b, w_out, g_moe, w_router_group, b_router_group, w_router_expert, b_router_expert, w_exp_gate, w_exp_up, w_exp_down, g_ple, w_ple_gate, w_ple_proj, g_final):
    depth = w_in.shape[0]
    assert depth == 1, "single-layer step only"
    nb, seq, _ = x_prompt.shape
    ns, dseq, _ = x_sample.shape
    assert dseq == SAMPLE_SEQ and seq % TM == 0 and (ns * dseq) % TM_NORM == 0 and (nb * seq) % TM_NORM == 0
    assert (nb * seq + ns * dseq) % TM_IN == 0
    tp, ts = nb * seq, ns * dseq
    t = tp + ts
    xp = x_prompt.reshape(tp, D_MODEL)
    xs = x_sample.reshape(ts, D_MODEL)

    row = lambda a: a.reshape(1, -1)
    tril = jnp.tril(jnp.ones((CHUNK, CHUNK), dtype=bool))
    w_sp = jnp.where(tril[None], w_spatial[0], 0.0)
    eye = jnp.eye(CHUNK // dseq, dtype=F32)
    w_sp_s = jnp.einsum("ij,gts->gitjs", eye, w_sp[:, :dseq, :dseq]).reshape(G_A, CHUNK, CHUNK)
    wsp = jnp.stack([w_sp, w_sp_s]).astype(BF16)
    b_p = jnp.repeat(b_spatial[0].T, D_GROUP_A, axis=1)
    b_s = jnp.tile(b_p[:dseq], (CHUNK // dseq, 1))
    bsp = jnp.stack([b_p, b_s])
    lane_pad = LANES - N_GROUPS - N_EXPERTS
    wr = jnp.concatenate([w_router_group[0], w_router_expert[0], jnp.zeros((D_MODEL, lane_pad), F32)], axis=1)
    br = jnp.concatenate([b_router_group[0], b_router_expert[0], jnp.zeros((lane_pad,), F32)]).reshape(1, LANES)
    f = jnp.pad(state_conv[0], ((0, 0), (0, dseq - (CONV_W - 1)), (0, 0))).reshape(ts, D_CONV)

    n = _norm(xp, xs, row(g_mix[0]))
    z, (wa, wb, wo, wpg) = _inproj(n, w_in, [w_br_a, w_br_b, w_out, w_ple_gate])
    h, xm, route, vnp, vns, qp, qs = _mixer(
        z, xp, xs, f, row(ln_v_g[0]), row(ln_v_b[0]), conv_w[0], wsp, bsp, wa, wb, wo,
        row(g_moe[0]), wr.astype(BF16), br, n_batch=nb)

    n_tiles = (2 * t) // TM_E + N_EXPERTS
    te, nused, pos, counts, offs = _route_tables(route, n_tiles)
    xsort = _dispatch(pos, counts, offs, nused, xm, n_tiles=n_tiles)
    ys = _experts(te, nused, xsort, w_exp_gate, w_exp_up, w_exp_down, n_tiles=n_tiles)

    yp, ysm = _tail(pos, ys, h, route, p_prompt[0].reshape(tp, -1), p_sample[0].reshape(ts, -1),
                    row(g_ple[0]), wpg, w_ple_proj[0].astype(BF16), row(g_final))

    y_prompt = yp.reshape(nb, seq, D_MODEL)
    y_sample = ysm.reshape(ns, dseq, D_MODEL)
    conv_state_prompt = qp[:nb][None]
    conv_state_sample = qs.reshape(ns, dseq, D_CONV)[:, dseq - (CONV_W - 1):][None]
    v_rows_prompt = vnp[:nb][None]
    v_rows_sample = vns.reshape(ns, dseq, D_HALF)[None]
    return (y_prompt, y_sample, conv_state_prompt, conv_state_sample, v_rows_prompt, v_rows_sample)
```

```python
import functools

import jax
import jax.numpy as jnp
from jax import lax
from jax.experimental import pallas as pl
from jax.experimental.pallas import tpu as pltpu

D_MODEL = 2048
D_HALF = D_MODEL // 2
D_CONV = D_MODEL // 2
G_A = 4
D_GROUP_A = D_HALF // G_A
CHUNK = 128
CONV_W = 3
N_GROUPS = 4
EXPERTS_PER_GROUP = 8
N_EXPERTS = N_GROUPS * EXPERTS_PER_GROUP
D_EXPERT = D_MODEL // 4
D_IN_TOTAL = 2 * D_HALF + 3 * D_CONV + 2 * D_MODEL
EPS = 1e-6

BF16 = jnp.bfloat16
F32 = jnp.float32

LANES = 128
VMEM_LIMIT = 60 * 1024 * 1024

TM_NORM = 512
TM_IN = 1024
CAST_ROWS = 128
TN_IN = 1536
TM = 256
TM_E = 256
ROWS = D_MODEL // (2 * LANES)
PACKED = jnp.uint32
COPY_TILE = 256
DISPATCH_SLOTS = 3
TAIL_CHUNKS = 8
MERGE_CHUNKS = 8
SAMPLE_SEQ = 8
RANK_BLOCK = 256


def _rms(x, g):
    return x * lax.rsqrt(jnp.mean(x * x, axis=-1, keepdims=True) + EPS) * g


def _sigmoid(x):
    return 1.0 / (1.0 + jnp.exp(-x))


def _store_token_major(ref, x):
    m = x.shape[0]
    for s in range(ROWS):
        lo = x[:, s * LANES:(s + 1) * LANES]
        hi = x[:, (ROWS + s) * LANES:(ROWS + s + 1) * LANES]
        ref[pl.ds(s, m, stride=ROWS), :] = pltpu.pack_elementwise([lo, hi], packed_dtype=BF16)


def _packed_zeros(shape):
    zero = jnp.zeros(shape, F32)
    return pltpu.pack_elementwise([zero, zero], packed_dtype=BF16)


def _unpack_words(words):
    half = lambda idx: [pltpu.unpack_elementwise(w, index=idx, packed_dtype=BF16, unpacked_dtype=F32)
                        for w in words]
    return jnp.concatenate(half(0) + half(1), axis=1)


def _load_token_major(ref, m):
    return _unpack_words([ref[pl.ds(s, m, stride=ROWS), :] for s in range(ROWS)])


def _norm_kernel(xp_ref, xs_ref, g_ref, o_ref, *, n_prompt_tiles):
    i = pl.program_id(0)

    def run(x_ref):
        o_ref[...] = _rms(x_ref[...], g_ref[...]).astype(BF16)

    @pl.when(i < n_prompt_tiles)
    def _():
        run(xp_ref)

    @pl.when(i >= n_prompt_tiles)
    def _():
        run(xs_ref)


def _norm(xp, xs, g):
    tp, ts = xp.shape[0], xs.shape[0]
    npt, nst = tp // TM_NORM, ts // TM_NORM
    return pl.pallas_call(
        functools.partial(_norm_kernel, n_prompt_tiles=npt),
        grid=(npt + nst,),
        in_specs=[
            pl.BlockSpec((TM_NORM, D_MODEL), lambda i: (jnp.minimum(i, npt - 1), 0)),
            pl.BlockSpec((TM_NORM, D_MODEL), lambda i: (jnp.maximum(i - npt, 0), 0)),
            pl.BlockSpec((1, D_MODEL), lambda i: (0, 0)),
        ],
        out_specs=pl.BlockSpec((TM_NORM, D_MODEL), lambda i: (i, 0)),
        out_shape=jax.ShapeDtypeStruct((tp + ts, D_MODEL), BF16),
        compiler_params=pltpu.CompilerParams(
            dimension_semantics=("arbitrary",), vmem_limit_bytes=VMEM_LIMIT),
        name="norm",
    )(xp, xs, g)


def _inproj_kernel(n_ref, w_ref, *refs, n_side):
    side_in, z_ref, side_out = refs[:n_side], refs[n_side], refs[n_side + 1:]

    z_ref[...] = jnp.dot(n_ref[...], w_ref[...].astype(BF16), preferred_element_type=F32).astype(BF16)
    for src, dst in zip(side_in, side_out):
        dst[...] = src[...].astype(BF16)


def _inproj(n, w_in, side_weights):
    t = n.shape[0]
    n_j, n_i = D_IN_TOTAL // TN_IN, t // TM_IN
    starts, start = [], 0
    for w in side_weights:
        assert w.shape[1] % CAST_ROWS == 0
        starts.append(start)
        start += w.shape[1] // CAST_ROWS
    assert start <= n_j * n_i, "not enough grid steps for the weight casts"

    def side_index(k):
        n_blk = side_weights[k].shape[1] // CAST_ROWS
        return lambda j, i: jnp.clip(j * n_i + i - starts[k], 0, n_blk - 1)

    side_in = [pl.BlockSpec((None, CAST_ROWS, w.shape[2]), lambda j, i, f=side_index(k): (0, f(j, i), 0))
               for k, w in enumerate(side_weights)]
    side_out = [pl.BlockSpec((CAST_ROWS, w.shape[2]), lambda j, i, f=side_index(k): (f(j, i), 0))
                for k, w in enumerate(side_weights)]
    outs = pl.pallas_call(
        functools.partial(_inproj_kernel, n_side=len(side_weights)),
        grid=(n_j, n_i),
        in_specs=[
            pl.BlockSpec((TM_IN, D_MODEL), lambda j, i: (i, 0)),
            pl.BlockSpec((None, D_MODEL, TN_IN), lambda j, i: (0, 0, j)),
        ] + side_in,
        out_specs=[pl.BlockSpec((TM_IN, TN_IN), lambda j, i: (i, j))] + side_out,
        out_shape=[jax.ShapeDtypeStruct((t, D_IN_TOTAL), BF16)]
        + [jax.ShapeDtypeStruct(w.shape[1:], BF16) for w in side_weights],
        compiler_params=pltpu.CompilerParams(
            dimension_semantics=("arbitrary", "arbitrary"), vmem_limit_bytes=VMEM_LIMIT),
        name="in_proj",
    )(n, w_in, *side_weights)
    return outs[0], outs[1:]


def _route(lg):
    shape = lg.shape
    lane = lax.broadcasted_iota(jnp.int32, shape, 1)
    lanef = lane.astype(F32)
    neg = jnp.float32(-jnp.inf)
    far = jnp.float32(LANES)
    is_g = lane < N_GROUPS
    lgm = jnp.where(is_g, lg, neg)
    gmax = jnp.max(lgm, axis=-1, keepdims=True)
    gsel = jnp.min(jnp.where(lgm == gmax, lanef, far), axis=-1, keepdims=True)
    p_group = 1.0 / jnp.sum(jnp.where(is_g, jnp.exp(lgm - gmax), 0.0), axis=-1, keepdims=True)
    lane_group = ((lane - N_GROUPS) >> (EXPERTS_PER_GROUP.bit_length() - 1)).astype(F32)
    is_e = lane_group == gsel
    le = jnp.where(is_e, lg, neg)
    m1 = jnp.max(le, axis=-1, keepdims=True)
    i1 = jnp.min(jnp.where(le == m1, lanef, far), axis=-1, keepdims=True)
    le2 = jnp.where(lanef == i1, neg, le)
    m2 = jnp.max(le2, axis=-1, keepdims=True)
    i2 = jnp.min(jnp.where(le2 == m2, lanef, far), axis=-1, keepdims=True)
    e21 = jnp.exp(m2 - m1)
    w1 = 1.0 / (1.0 + e21)
    w2 = e21 * w1
    out = jnp.where(lane == 0, i1 - N_GROUPS,
                    jnp.where(lane == 1, i2 - N_GROUPS,
                              jnp.where(lane == 2, p_group * w1,
                                        jnp.where(lane == 3, p_group * w2, 0.0))))
    return out


def _mixer_kernel(z_ref, xp_ref, xs_ref, f_ref, lng_ref, lnb_ref, cw_ref, wsp_ref, bsp_ref,
                  wa_ref, wb_ref, wo_ref, gm_ref, wr_ref, br_ref,
                  h_ref, xm_ref, route_ref, vnp_ref, vns_ref, qp_ref, qs_ref,
                  qprev_ref, ya_ref, yb_ref, m_ref, *, n_prompt_tiles, tiles_per_seq):
    i = pl.program_id(0)
    is_prompt = i < n_prompt_tiles
    c_u, c_v, c_b, c_c, c_x = (k * D_HALF for k in range(5))
    c_ga = c_x + D_CONV
    c_gb = c_ga + D_MODEL

    v = z_ref[:, c_v:c_v + D_HALF].astype(F32)
    mu = jnp.mean(v, axis=-1, keepdims=True)
    vc = v - mu
    vn = vc * lax.rsqrt(jnp.mean(vc * vc, axis=-1, keepdims=True) + EPS) * lng_ref[...] + lnb_ref[...]
    vnb = vn.astype(BF16)
    for c in range(TM // CHUNK):
        rows = slice(c * CHUNK, (c + 1) * CHUNK)
        for g in range(G_A):
            cols = slice(g * D_GROUP_A, (g + 1) * D_GROUP_A)
            s = jnp.dot(wsp_ref[g], vnb[rows, cols], preferred_element_type=F32) + bsp_ref[:, cols]
            u = z_ref[rows, c_u + g * D_GROUP_A:c_u + (g + 1) * D_GROUP_A].astype(F32)
            ya_ref[rows, cols] = (u * s).astype(BF16)

    q = z_ref[:, c_c:c_c + D_CONV].astype(F32) * z_ref[:, c_x:c_x + D_CONV].astype(F32)
    q1 = pltpu.roll(q, 1, 0)
    q2 = pltpu.roll(q, 2, 0)
    row = lax.broadcasted_iota(jnp.int32, q.shape, 0)
    cw0, cw1, cw2 = cw_ref[0:1, :], cw_ref[1:2, :], cw_ref[2:3, :]

    seq_start = i % tiles_per_seq == 0
    p6 = jnp.where(seq_start, 0.0, qprev_ref[6:7, :])
    p7 = jnp.where(seq_start, 0.0, qprev_ref[7:8, :])
    f = f_ref[...]
    pos_in_seq = jnp.where(is_prompt, row, row & (SAMPLE_SEQ - 1))
    fix1 = jnp.where(is_prompt, jnp.broadcast_to(p7, q.shape), pltpu.roll(f, TM - 1, 0))
    fix2 = jnp.where(is_prompt, jnp.where(row == 0, p6, p7), f)
    a1 = jnp.where(pos_in_seq == 0, fix1, q1)
    a2 = jnp.where(pos_in_seq < 2, fix2, q2)
    yc = cw0 * a2 + cw1 * a1 + cw2 * q
    qprev_ref[...] = q[TM - 8:TM, :]
    qp_ref[...] = q[TM - (CONV_W - 1):TM, :]
    vnp_ref[...] = vn[TM - CHUNK:TM, :]
    qs_ref[...] = q
    vns_ref[...] = vn

    yb_ref[...] = (z_ref[:, c_b:c_b + D_CONV].astype(F32) * yc).astype(BF16)

    width = D_MODEL // MERGE_CHUNKS
    for c in range(MERGE_CHUNKS):
        cols = slice(c * width, (c + 1) * width)
        a = jnp.dot(ya_ref[...], wa_ref[:, cols], preferred_element_type=F32)
        b = jnp.dot(yb_ref[...], wb_ref[:, cols], preferred_element_type=F32)
        ga = z_ref[:, c_ga + c * width:c_ga + (c + 1) * width].astype(F32)
        gb = z_ref[:, c_gb + c * width:c_gb + (c + 1) * width].astype(F32)
        m_ref[:, cols] = (_sigmoid(ga) * a + _sigmoid(gb) * b).astype(BF16)

    x = jnp.where(is_prompt, xp_ref[...], xs_ref[...])
    h = x + jnp.dot(m_ref[...], wo_ref[...], preferred_element_type=F32)
    h_ref[...] = h
    xm = _rms(h, gm_ref[...])
    _store_token_major(xm_ref, xm)
    lg = jnp.dot(xm.astype(BF16), wr_ref[...], preferred_element_type=F32) + br_ref[...]
    route_ref[...] = _route(lg)


def _mixer(z, xp, xs, f, lng, lnb, cw, wsp, bsp, wa, wb, wo, gm, wr, br, *, n_batch):
    tp, ts = xp.shape[0], xs.shape[0]
    t = tp + ts
    npt, nst = tp // TM, ts // TM
    tiles_per_seq = tp // n_batch // TM
    once = pl.Buffered(1)

    def pidx(i):
        return jnp.minimum(i, npt - 1)

    def sidx(i):
        return jnp.maximum(i - npt, 0)

    def seq_or_spare(i):
        return jnp.where(i < npt, i // tiles_per_seq, n_batch)

    const = lambda shape: pl.BlockSpec(shape, lambda i: (0,) * len(shape), pipeline_mode=once)
    in_specs = [
        pl.BlockSpec((TM, D_IN_TOTAL), lambda i: (i, 0)),
        pl.BlockSpec((TM, D_MODEL), lambda i: (pidx(i), 0)),
        pl.BlockSpec((TM, D_MODEL), lambda i: (sidx(i), 0)),
        pl.BlockSpec((TM, D_CONV), lambda i: (sidx(i), 0)),
        const((1, D_HALF)), const((1, D_HALF)), const((CONV_W, D_CONV)),
        pl.BlockSpec((None, G_A, CHUNK, CHUNK), lambda i: (jnp.where(i < npt, 0, 1), 0, 0, 0)),
        pl.BlockSpec((None, CHUNK, D_HALF), lambda i: (jnp.where(i < npt, 0, 1), 0, 0)),
        const((D_HALF, D_MODEL)), const((D_CONV, D_MODEL)), const((D_MODEL, D_MODEL)),
        const((1, D_MODEL)), const((D_MODEL, LANES)), const((1, LANES)),
    ]
    out_specs = [
        pl.BlockSpec((TM, D_MODEL), lambda i: (i, 0)),
        pl.BlockSpec((TM * ROWS, LANES), lambda i: (i, 0)),
        pl.BlockSpec((TM, LANES), lambda i: (i, 0)),
        pl.BlockSpec((None, CHUNK, D_HALF), lambda i: (seq_or_spare(i), 0, 0)),
        pl.BlockSpec((TM, D_HALF), lambda i: (sidx(i), 0)),
        pl.BlockSpec((None, CONV_W - 1, D_CONV), lambda i: (seq_or_spare(i), 0, 0)),
        pl.BlockSpec((TM, D_CONV), lambda i: (sidx(i), 0)),
    ]
    out_shape = [
        jax.ShapeDtypeStruct((t, D_MODEL), F32),
        jax.ShapeDtypeStruct((t * ROWS, LANES), PACKED),
        jax.ShapeDtypeStruct((t, LANES), F32),
        jax.ShapeDtypeStruct((n_batch + 1, CHUNK, D_HALF), F32),
        jax.ShapeDtypeStruct((ts, D_HALF), F32),
        jax.ShapeDtypeStruct((n_batch + 1, CONV_W - 1, D_CONV), F32),
        jax.ShapeDtypeStruct((ts, D_CONV), F32),
    ]
    return pl.pallas_call(
        functools.partial(_mixer_kernel, n_prompt_tiles=npt, tiles_per_seq=tiles_per_seq),
        grid=(npt + nst,),
        in_specs=in_specs,
        out_specs=out_specs,
        out_shape=out_shape,
        scratch_shapes=[
            pltpu.VMEM((8, D_CONV), F32),
            pltpu.VMEM((TM, D_HALF), BF16),
            pltpu.VMEM((TM, D_CONV), BF16),
            pltpu.VMEM((TM, D_MODEL), BF16),
        ],
        compiler_params=pltpu.CompilerParams(
            dimension_semantics=("arbitrary",), vmem_limit_bytes=VMEM_LIMIT),
        name="mixer",
    )(z, xp, xs, f, lng, lnb, cw, wsp, bsp, wa, wb, wo, gm, wr, br)


def _dispatch_kernel(pos_ref, cnt_ref, off_ref, nused_ref, xm_hbm, xs_hbm, inbuf, zbuf, isem, rsem, zsem,
                     *, n_steps, n_tiles):
    zbuf[...] = _packed_zeros(zbuf.shape)
    nused = nused_ref[0]
    pad_bits = TM_E.bit_length() - 1

    def pad_copies(e, fn):
        cnt = cnt_ref[e]
        npad = (TM_E - cnt % TM_E) % TM_E
        start = off_ref[e] + cnt
        for b in range(pad_bits):
            size = 1 << b
            at = start + ((npad >> (b + 1)) << (b + 1))

            @pl.when(((npad >> b) & 1) == 1)
            def _():
                fn(pltpu.make_async_copy(zbuf.at[pl.ds(0, size * ROWS), :],
                                         xs_hbm.at[pl.ds(at * ROWS, size * ROWS), :], zsem))

    def tile_copy(tile):
        return pltpu.make_async_copy(zbuf, xs_hbm.at[pl.ds(tile * TM_E * ROWS, TM_E * ROWS), :], zsem)

    def loop(lo, hi, body):
        lax.fori_loop(lo, hi, lambda k, c: (body(k), c)[1], 0)

    loop(0, N_EXPERTS, lambda e: pad_copies(e, lambda cp: cp.start()))
    loop(nused, n_tiles, lambda tile: tile_copy(tile).start())

    def in_copy(step, slot):
        return pltpu.make_async_copy(xm_hbm.at[pl.ds(step * COPY_TILE * ROWS, COPY_TILE * ROWS), :],
                                     inbuf.at[slot], isem.at[slot])

    def rows_wait(slot):
        for _ in range(2):
            pltpu.make_async_copy(inbuf.at[slot], xs_hbm.at[pl.ds(0, COPY_TILE * ROWS), :], rsem.at[slot]).wait()

    in_copy(0, 0).start()

    def step(s):
        slot = s % DISPATCH_SLOTS
        nxt = (s + 1) % DISPATCH_SLOTS

        @pl.when(s >= DISPATCH_SLOTS - 1)
        def _():
            rows_wait(nxt)

        @pl.when(s + 1 < n_steps)
        def _():
            in_copy(s + 1, nxt).start()

        in_copy(s, slot).wait()
        base = s * COPY_TILE * 2

        def body(r, carry):
            for k in range(2):
                p = pos_ref[base + 2 * r + k]
                pltpu.make_async_copy(inbuf.at[slot, pl.ds(r * ROWS, ROWS), :],
                                      xs_hbm.at[pl.ds(p * ROWS, ROWS), :], rsem.at[slot]).start(priority=k)
            return carry

        lax.fori_loop(0, COPY_TILE, body, 0, unroll=4)

    loop(0, n_steps, step)
    for back in range(DISPATCH_SLOTS - 1, 0, -1):
        rows_wait((n_steps - back) % DISPATCH_SLOTS)
    loop(0, N_EXPERTS, lambda e: pad_copies(e, lambda cp: cp.wait()))
    loop(nused, n_tiles, lambda tile: tile_copy(tile).wait())


def _dispatch(pos, cnt, off, nused, xm, *, n_tiles):
    n_tokens = xm.shape[0] // ROWS
    assert n_tokens % COPY_TILE == 0 and n_tokens // COPY_TILE >= DISPATCH_SLOTS
    smem = pl.BlockSpec(memory_space=pltpu.SMEM)
    return pl.pallas_call(
        functools.partial(_dispatch_kernel, n_steps=n_tokens // COPY_TILE, n_tiles=n_tiles),
        in_specs=[smem, smem, smem, smem, pl.BlockSpec(memory_space=pl.ANY)],
        out_specs=pl.BlockSpec(memory_space=pl.ANY),
        out_shape=jax.ShapeDtypeStruct((n_tiles * TM_E * ROWS, LANES), PACKED),
        scratch_shapes=[
            pltpu.VMEM((DISPATCH_SLOTS, COPY_TILE * ROWS, LANES), PACKED),
            pltpu.VMEM((TM_E * ROWS, LANES), PACKED),
            pltpu.SemaphoreType.DMA((DISPATCH_SLOTS,)),
            pltpu.SemaphoreType.DMA((DISPATCH_SLOTS,)),
            pltpu.SemaphoreType.DMA(()),
        ],
        compiler_params=pltpu.CompilerParams(vmem_limit_bytes=VMEM_LIMIT),
        name="dispatch",
    )(pos, cnt, off, nused, xm)


def _expert_kernel(te_ref, nused_ref, xs_ref, wg_hbm, wu_hbm, wd_hbm, y_ref,
                   wgf, wuf, wdf, wsem, seen_ref):
    i = pl.program_id(0)
    nused = nused_ref[0]

    def weight_copies(e, slot):
        return [pltpu.make_async_copy(src.at[0, e], dst.at[slot], wsem.at[slot, k])
                for k, (src, dst) in enumerate(((wg_hbm, wgf), (wu_hbm, wuf), (wd_hbm, wdf)))]

    @pl.when(i == 0)
    def _():
        seen_ref[0] = 0
        for cp in weight_copies(te_ref[0], 0):
            cp.start()

    @pl.when(i < nused)
    def _():
        e = te_ref[i]
        new_expert = jnp.logical_or(i == 0, e != te_ref[jnp.maximum(i - 1, 0)])

        @pl.when(new_expert)
        def _():
            slot = seen_ref[0] % 2
            seen_ref[0] = seen_ref[0] + 1
            seen_ref[1] = slot
            nxt = lax.while_loop(lambda j: jnp.logical_and(j < nused, te_ref[jnp.minimum(j, nused - 1)] == e),
                                 lambda j: j + 1, i + 1)

            @pl.when(nxt < nused)
            def _():
                for cp in weight_copies(te_ref[jnp.minimum(nxt, nused - 1)], 1 - slot):
                    cp.start(priority=1)

            for cp in weight_copies(e, slot):
                cp.wait()

        cur = seen_ref[1]
        x = _load_token_major(xs_ref, TM_E).astype(BF16)
        g = jnp.dot(x, wgf[cur].astype(BF16), preferred_element_type=F32)
        u = jnp.dot(x, wuf[cur].astype(BF16), preferred_element_type=F32)
        hid = (g * _sigmoid(g)) * u
        _store_token_major(y_ref, jnp.dot(hid.astype(BF16), wdf[cur].astype(BF16), preferred_element_type=F32))

    @pl.when(i >= nused)
    def _():
        y_ref[...] = _packed_zeros(y_ref.shape)


def _experts(te, nused, xs, wg, wu, wd, *, n_tiles):
    grid_spec = pltpu.PrefetchScalarGridSpec(
        num_scalar_prefetch=2,
        grid=(n_tiles,),
        in_specs=[
            pl.BlockSpec((TM_E * ROWS, LANES), lambda i, te, nu: (jnp.minimum(i, nu[0] - 1), 0)),
            pl.BlockSpec(memory_space=pl.ANY),
            pl.BlockSpec(memory_space=pl.ANY),
            pl.BlockSpec(memory_space=pl.ANY),
        ],
        out_specs=pl.BlockSpec((TM_E * ROWS, LANES), lambda i, te, nu: (i, 0)),
        scratch_shapes=[
            pltpu.VMEM((2, D_MODEL, D_EXPERT), F32),
            pltpu.VMEM((2, D_MODEL, D_EXPERT), F32),
            pltpu.VMEM((2, D_EXPERT, D_MODEL), F32),
            pltpu.SemaphoreType.DMA((2, 3)),
            pltpu.SMEM((2,), jnp.int32),
        ],
    )
    return pl.pallas_call(
        _expert_kernel,
        grid_spec=grid_spec,
        out_shape=jax.ShapeDtypeStruct((n_tiles * TM_E * ROWS, LANES), PACKED),
        compiler_params=pltpu.CompilerParams(
            dimension_semantics=("arbitrary",), vmem_limit_bytes=VMEM_LIMIT),
        name="experts",
    )(te, nused, xs, wg, wu, wd)


def _tail_kernel(pos_ref, ys_hbm, h_ref, route_ref, pp_ref, ps_ref, gp_ref, wpg_ref, wpp_ref, gf_ref,
                 yp_ref, ysm_ref, ybuf, sem, h2_ref, a_ref, proj_ref, *, n_prompt_tiles, n_tiles):
    i = pl.program_id(0)
    slot = i % 2
    is_prompt = i < n_prompt_tiles

    def issue(tile, dst_slot):
        base = tile * TM * 2

        def body(r, carry):
            for k in range(2):
                p = pos_ref[base + 2 * r + k]
                pltpu.make_async_copy(ys_hbm.at[pl.ds(p * ROWS, ROWS), :],
                                      ybuf.at[dst_slot, k, pl.ds(r * ROWS, ROWS), :],
                                      sem.at[dst_slot]).start()
            return carry

        lax.fori_loop(0, TM, body, 0, unroll=4)

    def slot_wait(s):
        for k in range(2):
            pltpu.make_async_copy(ys_hbm.at[pl.ds(0, TM * ROWS), :], ybuf.at[s, k], sem.at[s]).wait()

    def row_sumsq(acc):
        return jnp.sum(acc, axis=-1, keepdims=True) * (1.0 / D_MODEL)

    @pl.when(i == 0)
    def _():
        issue(0, 0)

    slot_wait(slot)

    w0 = jnp.broadcast_to(route_ref[:, 2:3], (TM, LANES))
    w1 = jnp.broadcast_to(route_ref[:, 3:4], (TM, LANES))
    acc = jnp.zeros((TM, LANES), F32)
    for s in range(ROWS):
        word0 = ybuf[slot, 0, pl.ds(s, TM, stride=ROWS), :]
        word1 = ybuf[slot, 1, pl.ds(s, TM, stride=ROWS), :]
        for idx in range(2):
            cols = slice((idx * ROWS + s) * LANES, (idx * ROWS + s + 1) * LANES)
            y0 = pltpu.unpack_elementwise(word0, index=idx, packed_dtype=BF16, unpacked_dtype=F32)
            y1 = pltpu.unpack_elementwise(word1, index=idx, packed_dtype=BF16, unpacked_dtype=F32)
            h2 = h_ref[:, cols] + (w0 * y0 + w1 * y1)
            h2_ref[:, cols] = h2
            acc = acc + h2 * h2
    r = lax.rsqrt(row_sumsq(acc) + EPS)
    a_ref[...] = (h2_ref[...] * r * gp_ref[...]).astype(BF16)
    p = jnp.where(is_prompt, pp_ref[...], ps_ref[...]).astype(BF16)
    proj_ref[...] = jnp.dot(p, wpp_ref[...], preferred_element_type=F32)

    nxt_base = jnp.minimum(i + 1, n_tiles - 1) * TM * 2
    rows_per_chunk = TM // TAIL_CHUNKS
    width = D_MODEL // TAIL_CHUNKS
    acc = jnp.zeros((TM, LANES), F32)
    for c in range(TAIL_CHUNKS):
        for rr in range(rows_per_chunk):
            row = c * rows_per_chunk + rr
            for k in range(2):
                p_row = pos_ref[nxt_base + 2 * row + k]
                pltpu.make_async_copy(ys_hbm.at[pl.ds(p_row * ROWS, ROWS), :],
                                      ybuf.at[1 - slot, k, pl.ds(row * ROWS, ROWS), :],
                                      sem.at[1 - slot]).start()
        cols = slice(c * width, (c + 1) * width)
        gate = _sigmoid(jnp.dot(a_ref[...], wpg_ref[:, cols], preferred_element_type=F32))
        h3 = h2_ref[:, cols] + gate * proj_ref[:, cols]
        h2_ref[:, cols] = h3
        for j in range(width // LANES):
            part = h3[:, j * LANES:(j + 1) * LANES]
            acc = acc + part * part
    scale = lax.rsqrt(row_sumsq(acc) + EPS)

    @pl.when(is_prompt)
    def _():
        yp_ref[...] = h2_ref[...] * scale * gf_ref[...]

    @pl.when(jnp.logical_not(is_prompt))
    def _():
        ysm_ref[...] = h2_ref[...] * scale * gf_ref[...]

    @pl.when(i == n_tiles - 1)
    def _():
        slot_wait(1 - slot)


def _tail(pos, ys, h, route, pp, ps, gp, wpg, wpp, gf):
    tp, ts = pp.shape[0], ps.shape[0]
    npt, nst = tp // TM, ts // TM
    ple = pp.shape[1]
    once = pl.Buffered(1)

    def pidx(i):
        return jnp.minimum(i, npt - 1)

    def sidx(i):
        return jnp.maximum(i - npt, 0)

    grid_spec = pltpu.PrefetchScalarGridSpec(
        num_scalar_prefetch=1,
        grid=(npt + nst,),
        in_specs=[
            pl.BlockSpec(memory_space=pl.ANY),
            pl.BlockSpec((TM, D_MODEL), lambda i, pos: (i, 0)),
            pl.BlockSpec((TM, LANES), lambda i, pos: (i, 0)),
            pl.BlockSpec((TM, ple), lambda i, pos: (pidx(i), 0)),
            pl.BlockSpec((TM, ple), lambda i, pos: (sidx(i), 0)),
            pl.BlockSpec((1, D_MODEL), lambda i, pos: (0, 0), pipeline_mode=once),
            pl.BlockSpec((D_MODEL, D_MODEL), lambda i, pos: (0, 0), pipeline_mode=once),
            pl.BlockSpec((ple, D_MODEL), lambda i, pos: (0, 0), pipeline_mode=once),
            pl.BlockSpec((1, D_MODEL), lambda i, pos: (0, 0), pipeline_mode=once),
        ],
        out_specs=[
            pl.BlockSpec((TM, D_MODEL), lambda i, pos: (pidx(i), 0)),
            pl.BlockSpec((TM, D_MODEL), lambda i, pos: (sidx(i), 0)),
        ],
        scratch_shapes=[
            pltpu.VMEM((2, 2, TM * ROWS, LANES), PACKED),
            pltpu.SemaphoreType.DMA((2,)),
            pltpu.VMEM((TM, D_MODEL), F32),
            pltpu.VMEM((TM, D_MODEL), BF16),
            pltpu.VMEM((TM, D_MODEL), F32),
        ],
    )
    return pl.pallas_call(
        functools.partial(_tail_kernel, n_prompt_tiles=npt, n_tiles=npt + nst),
        grid_spec=grid_spec,
        out_shape=[jax.ShapeDtypeStruct((tp, D_MODEL), F32), jax.ShapeDtypeStruct((ts, D_MODEL), F32)],
        compiler_params=pltpu.CompilerParams(
            dimension_semantics=("arbitrary",), vmem_limit_bytes=VMEM_LIMIT),
        name="tail",
    )(pos, ys, h, route, pp, ps, gp, wpg, wpp, gf)


def _route_tables(route, n_tiles):
    e = route[:, 0:2].astype(jnp.int32).reshape(-1)
    n_blocks = e.shape[0] // RANK_BLOCK
    onehot = (e[:, None] == jnp.arange(N_EXPERTS, dtype=jnp.int32)[None, :]).astype(BF16)
    onehot = onehot.reshape(n_blocks, RANK_BLOCK, N_EXPERTS)
    ltri = jnp.tril(jnp.ones((RANK_BLOCK, RANK_BLOCK), BF16), -1)
    within = jnp.einsum("ij,bjk->bik", ltri, onehot, preferred_element_type=F32)
    block_counts = jnp.sum(onehot.astype(F32), axis=1)
    before = jnp.cumsum(block_counts, axis=0) - block_counts
    counts = jnp.sum(block_counts, axis=0).astype(jnp.int32)
    padded = ((counts + TM_E - 1) // TM_E) * TM_E
    ends = jnp.cumsum(padded)
    offs = ends - padded
    pos = jnp.sum((within + before[:, None, :] + offs[None, None, :].astype(F32)) * onehot.astype(F32), axis=-1)
    pos = pos.reshape(-1)
    nused = ends[-1] // TM_E
    tile_id = jnp.arange(n_tiles, dtype=jnp.int32)
    te = jnp.sum((tile_id[:, None] * TM_E >= ends[None, :]).astype(jnp.int32), axis=1)
    te_last = jnp.sum((((nused - 1) * TM_E) >= ends).astype(jnp.int32))
    te = jnp.where(tile_id < nused, te, te_last)
    i32 = lambda a: a.astype(jnp.int32)
    return i32(te), i32(nused).reshape(1), i32(pos), i32(counts), i32(offs)


def kernel(x_prompt, x_sample, state_conv, p_prompt, p_sample, g_mix, w_in, ln_v_g, ln_v_b, w_spatial, b_spatial, conv_w, w_br_a, w_br_b, w_out, g_moe, w_router_group, b_router_group, w_router_expert, b_router_expert, w_exp_gate, w_exp_up, w_exp_down, g_ple, w_ple_gate, w_ple_proj, g_final):
    depth = w_in.shape[0]
    assert depth == 1, "single-layer step only"
    nb, seq, _ = x_prompt.shape
    ns, dseq, _ = x_sample.shape
    assert dseq == SAMPLE_SEQ and seq % TM == 0 and (ns * dseq) % TM_NORM == 0 and (nb * seq) % TM_NORM == 0
    assert (nb * seq + ns * dseq) % TM_IN == 0
    tp, ts = nb * seq, ns * dseq
    t = tp + ts
    xp = x_prompt.reshape(tp, D_MODEL)
    xs = x_sample.reshape(ts, D_MODEL)

    row = lambda a: a.reshape(1, -1)
    tril = jnp.tril(jnp.ones((CHUNK, CHUNK), dtype=bool))
    w_sp = jnp.where(tril[None], w_spatial[0], 0.0)
    eye = jnp.eye(CHUNK // dseq, dtype=F32)
    w_sp_s = jnp.einsum("ij,gts->gitjs", eye, w_sp[:, :dseq, :dseq]).reshape(G_A, CHUNK, CHUNK)
    wsp = jnp.stack([w_sp, w_sp_s]).astype(BF16)
    b_p = jnp.repeat(b_spatial[0].T, D_GROUP_A, axis=1)
    b_s = jnp.tile(b_p[:dseq], (CHUNK // dseq, 1))
    bsp = jnp.stack([b_p, b_s])
    lane_pad = LANES - N_GROUPS - N_EXPERTS
    wr = jnp.concatenate([w_router_group[0], w_router_expert[0], jnp.zeros((D_MODEL, lane_pad), F32)], axis=1)
    br = jnp.concatenate([b_router_group[0], b_router_expert[0], jnp.zeros((lane_pad,), F32)]).reshape(1, LANES)
    f = jnp.pad(state_conv[0], ((0, 0), (0, dseq - (CONV_W - 1)), (0, 0))).reshape(ts, D_CONV)

    n = _norm(xp, xs, row(g_mix[0]))
    z, (wa, wb, wo, wpg) = _inproj(n, w_in, [w_br_a, w_br_b, w_out, w_ple_gate])
    h, xm, route, vnp, vns, qp, qs = _mixer(
        z, xp, xs, f, row(ln_v_g[0]), row(ln_v_b[0]), conv_w[0], wsp, bsp, wa, wb, wo,
        row(g_moe[0]), wr.astype(BF16), br, n_batch=nb)

    n_tiles = (2 * t) // TM_E + N_EXPERTS
    te, nused, pos, counts, offs = _route_tables(route, n_tiles)
    xsort = _dispatch(pos, counts, offs, nused, xm, n_tiles=n_tiles)
    ys = _experts(te, nused, xsort, w_exp_gate, w_exp_up, w_exp_down, n_tiles=n_tiles)

    yp, ysm = _tail(pos, ys, h, route, p_prompt[0].reshape(tp, -1), p_sample[0].reshape(ts, -1),
                    row(g_ple[0]), wpg, w_ple_proj[0].astype(BF16), row(g_final))

    y_prompt = yp.reshape(nb, seq, D_MODEL)
    y_sample = ysm.reshape(ns, dseq, D_MODEL)
    conv_state_prompt = qp[:nb][None]
    conv_state_sample = qs.reshape(ns, dseq, D_CONV)[:, dseq - (CONV_W - 1):][None]
    v_rows_prompt = vnp[:nb][None]
    v_rows_sample = vns.reshape(ns, dseq, D_HALF)[None]
    return (y_prompt, y_sample, conv_state_prompt, conv_state_sample, v_rows_prompt, v_rows_sample)
```

```python
import functools

import jax
import jax.numpy as jnp
from jax import lax
from jax.experimental import pallas as pl
from jax.experimental.pallas import tpu as pltpu

D_MODEL = 2048
D_HALF = D_MODEL // 2
D_CONV = D_MODEL // 2
G_A = 4
D_GROUP_A = D_HALF // G_A
CHUNK = 128
CONV_W = 3
N_GROUPS = 4
EXPERTS_PER_GROUP = 8
N_EXPERTS = N_GROUPS * EXPERTS_PER_GROUP
D_EXPERT = D_MODEL // 4
D_IN_TOTAL = 2 * D_HALF + 3 * D_CONV + 2 * D_MODEL
EPS = 1e-6

BF16 = jnp.bfloat16
F32 = jnp.float32

LANES = 128
VMEM_LIMIT = 60 * 1024 * 1024

TM_NORM = 512
TM_IN = 1024
CAST_ROWS = 128
TN_IN = 1536
TM = 256
TM_E = 256
ROWS = D_MODEL // (2 * LANES)
PACKED = jnp.uint32
COPY_TILE = 256
DISPATCH_SLOTS = 3
TAIL_CHUNKS = 8
MERGE_CHUNKS = 8
SAMPLE_SEQ = 8
RANK_BLOCK = 256


def _rms(x, g):
    return x * lax.rsqrt(jnp.mean(x * x, axis=-1, keepdims=True) + EPS) * g


def _sigmoid(x):
    return 1.0 / (1.0 + jnp.exp(-x))


def _store_token_major(ref, x):
    m = x.shape[0]
    for s in range(ROWS):
        lo = x[:, s * LANES:(s + 1) * LANES]
        hi = x[:, (ROWS + s) * LANES:(ROWS + s + 1) * LANES]
        ref[pl.ds(s, m, stride=ROWS), :] = pltpu.pack_elementwise([lo, hi], packed_dtype=BF16)


def _packed_zeros(shape):
    zero = jnp.zeros(shape, F32)
    return pltpu.pack_elementwise([zero, zero], packed_dtype=BF16)


def _unpack_words(words):
    half = lambda idx: [pltpu.unpack_elementwise(w, index=idx, packed_dtype=BF16, unpacked_dtype=F32)
                        for w in words]
    return jnp.concatenate(half(0) + half(1), axis=1)


def _load_token_major(ref, m):
    return _unpack_words([ref[pl.ds(s, m, stride=ROWS), :] for s in range(ROWS)])


def _norm_kernel(xp_ref, xs_ref, g_ref, o_ref, *, n_prompt_tiles):
    i = pl.program_id(0)

    def run(x_ref):
        o_ref[...] = _rms(x_ref[...], g_ref[...]).astype(BF16)

    @pl.when(i < n_prompt_tiles)
    def _():
        run(xp_ref)

    @pl.when(i >= n_prompt_tiles)
    def _():
        run(xs_ref)


def _norm(xp, xs, g):
    tp, ts = xp.shape[0], xs.shape[0]
    npt, nst = tp // TM_NORM, ts // TM_NORM
    return pl.pallas_call(
        functools.partial(_norm_kernel, n_prompt_tiles=npt),
        grid=(npt + nst,),
        in_specs=[
            pl.BlockSpec((TM_NORM, D_MODEL), lambda i: (jnp.minimum(i, npt - 1), 0)),
            pl.BlockSpec((TM_NORM, D_MODEL), lambda i: (jnp.maximum(i - npt, 0), 0)),
            pl.BlockSpec((1, D_MODEL), lambda i: (0, 0)),
        ],
        out_specs=pl.BlockSpec((TM_NORM, D_MODEL), lambda i: (i, 0)),
        out_shape=jax.ShapeDtypeStruct((tp + ts, D_MODEL), BF16),
        compiler_params=pltpu.CompilerParams(
            dimension_semantics=("arbitrary",), vmem_limit_bytes=VMEM_LIMIT),
        name="norm",
    )(xp, xs, g)


def _inproj_kernel(n_ref, w_ref, *refs, n_side):
    side_in, z_ref, side_out = refs[:n_side], refs[n_side], refs[n_side + 1:]

    z_ref[...] = jnp.dot(n_ref[...], w_ref[...].astype(BF16), preferred_element_type=F32).astype(BF16)
    for src, dst in zip(side_in, side_out):
        dst[...] = src[...].astype(BF16)


def _inproj(n, w_in, side_weights):
    t = n.shape[0]
    n_j, n_i = D_IN_TOTAL // TN_IN, t // TM_IN
    starts, start = [], 0
    for w in side_weights:
        assert w.shape[1] % CAST_ROWS == 0
        starts.append(start)
        start += w.shape[1] // CAST_ROWS
    assert start <= n_j * n_i, "not enough grid steps for the weight casts"

    def side_index(k):
        n_blk = side_weights[k].shape[1] // CAST_ROWS
        return lambda j, i: jnp.clip(j * n_i + i - starts[k], 0, n_blk - 1)

    side_in = [pl.BlockSpec((None, CAST_ROWS, w.shape[2]), lambda j, i, f=side_index(k): (0, f(j, i), 0))
               for k, w in enumerate(side_weights)]
    side_out = [pl.BlockSpec((CAST_ROWS, w.shape[2]), lambda j, i, f=side_index(k): (f(j, i), 0))
                for k, w in enumerate(side_weights)]
    outs = pl.pallas_call(
        functools.partial(_inproj_kernel, n_side=len(side_weights)),
        grid=(n_j, n_i),
        in_specs=[
            pl.BlockSpec((TM_IN, D_MODEL), lambda j, i: (i, 0)),
            pl.BlockSpec((None, D_MODEL, TN_IN), lambda j, i: (0, 0, j)),
        ] + side_in,
        out_specs=[pl.BlockSpec((TM_IN, TN_IN), lambda j, i: (i, j))] + side_out,
        out_shape=[jax.ShapeDtypeStruct((t, D_IN_TOTAL), BF16)]
        + [jax.ShapeDtypeStruct(w.shape[1:], BF16) for w in side_weights],
        compiler_params=pltpu.CompilerParams(
            dimension_semantics=("arbitrary", "arbitrary"), vmem_limit_bytes=VMEM_LIMIT),
        name="in_proj",
    )(n, w_in, *side_weights)
    return outs[0], outs[1:]


def _route(lg):
    shape = lg.shape
    lane = lax.broadcasted_iota(jnp.int32, shape, 1)
    lanef = lane.astype(F32)
    neg = jnp.float32(-jnp.inf)
    far = jnp.float32(LANES)
    is_g = lane < N_GROUPS
    lgm = jnp.where(is_g, lg, neg)
    gmax = jnp.max(lgm, axis=-1, keepdims=True)
    gsel = jnp.min(jnp.where(lgm == gmax, lanef, far), axis=-1, keepdims=True)
    p_group = 1.0 / jnp.sum(jnp.where(is_g, jnp.exp(lgm - gmax), 0.0), axis=-1, keepdims=True)
    lane_group = ((lane - N_GROUPS) >> (EXPERTS_PER_GROUP.bit_length() - 1)).astype(F32)
    is_e = lane_group == gsel
    le = jnp.where(is_e, lg, neg)
    m1 = jnp.max(le, axis=-1, keepdims=True)
    i1 = jnp.min(jnp.where(le == m1, lanef, far), axis=-1, keepdims=True)
    le2 = jnp.where(lanef == i1, neg, le)
    m2 = jnp.max(le2, axis=-1, keepdims=True)
    i2 = jnp.min(jnp.where(le2 == m2, lanef, far), axis=-1, keepdims=True)
    e21 = jnp.exp(m2 - m1)
    w1 = 1.0 / (1.0 + e21)
    w2 = e21 * w1
    out = jnp.where(lane == 0, i1 - N_GROUPS,
                    jnp.where(lane == 1, i2 - N_GROUPS,
                              jnp.where(lane == 2, p_group * w1,
                                        jnp.where(lane == 3, p_group * w2, 0.0))))
    return out


def _mixer_kernel(z_ref, xp_ref, xs_ref, f_ref, lng_ref, lnb_ref, cw_ref, wsp_ref, bsp_ref,
                  wa_ref, wb_ref, wo_ref, gm_ref, wr_ref, br_ref,
                  h_ref, xm_ref, route_ref, vnp_ref, vns_ref, qp_ref, qs_ref,
                  qprev_ref, ya_ref, yb_ref, m_ref, *, n_prompt_tiles, tiles_per_seq):
    i = pl.program_id(0)
    is_prompt = i < n_prompt_tiles
    c_u, c_v, c_b, c_c, c_x = (k * D_HALF for k in range(5))
    c_ga = c_x + D_CONV
    c_gb = c_ga + D_MODEL

    v = z_ref[:, c_v:c_v + D_HALF].astype(F32)
    mu = jnp.mean(v, axis=-1, keepdims=True)
    vc = v - mu
    vn = vc * lax.rsqrt(jnp.mean(vc * vc, axis=-1, keepdims=True) + EPS) * lng_ref[...] + lnb_ref[...]
    vnb = vn.astype(BF16)
    for c in range(TM // CHUNK):
        rows = slice(c * CHUNK, (c + 1) * CHUNK)
        for g in range(G_A):
            cols = slice(g * D_GROUP_A, (g + 1) * D_GROUP_A)
            s = jnp.dot(wsp_ref[g], vnb[rows, cols], preferred_element_type=F32) + bsp_ref[:, cols]
            u = z_ref[rows, c_u + g * D_GROUP_A:c_u + (g + 1) * D_GROUP_A].astype(F32)
            ya_ref[rows, cols] = (u * s).astype(BF16)

    q = z_ref[:, c_c:c_c + D_CONV].astype(F32) * z_ref[:, c_x:c_x + D_CONV].astype(F32)
    q1 = pltpu.roll(q, 1, 0)
    q2 = pltpu.roll(q, 2, 0)
    row = lax.broadcasted_iota(jnp.int32, q.shape, 0)
    cw0, cw1, cw2 = cw_ref[0:1, :], cw_ref[1:2, :], cw_ref[2:3, :]

    seq_start = i % tiles_per_seq == 0
    p6 = jnp.where(seq_start, 0.0, qprev_ref[6:7, :])
    p7 = jnp.where(seq_start, 0.0, qprev_ref[7:8, :])
    f = f_ref[...]
    pos_in_seq = jnp.where(is_prompt, row, row & (SAMPLE_SEQ - 1))
    fix1 = jnp.where(is_prompt, jnp.broadcast_to(p7, q.shape), pltpu.roll(f, TM - 1, 0))
    fix2 = jnp.where(is_prompt, jnp.where(row == 0, p6, p7), f)
    a1 = jnp.where(pos_in_seq == 0, fix1, q1)
    a2 = jnp.where(pos_in_seq < 2, fix2, q2)
    yc = cw0 * a2 + cw1 * a1 + cw2 * q
    qprev_ref[...] = q[TM - 8:TM, :]
    qp_ref[...] = q[TM - (CONV_W - 1):TM, :]
    vnp_ref[...] = vn[TM - CHUNK:TM, :]
    qs_ref[...] = q
    vns_ref[...] = vn

    yb_ref[...] = (z_ref[:, c_b:c_b + D_CONV].astype(F32) * yc).astype(BF16)

    width = D_MODEL // MERGE_CHUNKS
    for c in range(MERGE_CHUNKS):
        cols = slice(c * width, (c + 1) * width)
        a = jnp.dot(ya_ref[...], wa_ref[:, cols], preferred_element_type=F32)
        b = jnp.dot(yb_ref[...], wb_ref[:, cols], preferred_element_type=F32)
        ga = z_ref[:, c_ga + c * width:c_ga + (c + 1) * width].astype(F32)
        gb = z_ref[:, c_gb + c * width:c_gb + (c + 1) * width].astype(F32)
        m_ref[:, cols] = (_sigmoid(ga) * a + _sigmoid(gb) * b).astype(BF16)

    x = jnp.where(is_prompt, xp_ref[...], xs_ref[...])
    h = x + jnp.dot(m_ref[...], wo_ref[...], preferred_element_type=F32)
    h_ref[...] = h
    xm = _rms(h, gm_ref[...])
    _store_token_major(xm_ref, xm)
    lg = jnp.dot(xm.astype(BF16), wr_ref[...], preferred_element_type=F32) + br_ref[...]
    route_ref[...] = _route(lg)


def _mixer(z, xp, xs, f, lng, lnb, cw, wsp, bsp, wa, wb, wo, gm, wr, br, *, n_batch):
    tp, ts = xp.shape[0], xs.shape[0]
    t = tp + ts
    npt, nst = tp // TM, ts // TM
    tiles_per_seq = tp // n_batch // TM
    once = pl.Buffered(1)

    def pidx(i):
        return jnp.minimum(i, npt - 1)

    def sidx(i):
        return jnp.maximum(i - npt, 0)

    def seq_or_spare(i):
        return jnp.where(i < npt, i // tiles_per_seq, n_batch)

    const = lambda shape: pl.BlockSpec(shape, lambda i: (0,) * len(shape), pipeline_mode=once)
    in_specs = [
        pl.BlockSpec((TM, D_IN_TOTAL), lambda i: (i, 0)),
        pl.BlockSpec((TM, D_MODEL), lambda i: (pidx(i), 0)),
        pl.BlockSpec((TM, D_MODEL), lambda i: (sidx(i), 0)),
        pl.BlockSpec((TM, D_CONV), lambda i: (sidx(i), 0)),
        const((1, D_HALF)), const((1, D_HALF)), const((CONV_W, D_CONV)),
        pl.BlockSpec((None, G_A, CHUNK, CHUNK), lambda i: (jnp.where(i < npt, 0, 1), 0, 0, 0)),
        pl.BlockSpec((None, CHUNK, D_HALF), lambda i: (jnp.where(i < npt, 0, 1), 0, 0)),
        const((D_HALF, D_MODEL)), const((D_CONV, D_MODEL)), const((D_MODEL, D_MODEL)),
        const((1, D_MODEL)), const((D_MODEL, LANES)), const((1, LANES)),
    ]
    out_specs = [
        pl.BlockSpec((TM, D_MODEL), lambda i: (i, 0)),
        pl.BlockSpec((TM * ROWS, LANES), lambda i: (i, 0)),
        pl.BlockSpec((TM, LANES), lambda i: (i, 0)),
        pl.BlockSpec((None, CHUNK, D_HALF), lambda i: (seq_or_spare(i), 0, 0)),
        pl.BlockSpec((TM, D_HALF), lambda i: (sidx(i), 0)),
        pl.BlockSpec((None, CONV_W - 1, D_CONV), lambda i: (seq_or_spare(i), 0, 0)),
        pl.BlockSpec((TM, D_CONV), lambda i: (sidx(i), 0)),
    ]
    out_shape = [
        jax.ShapeDtypeStruct((t, D_MODEL), F32),
        jax.ShapeDtypeStruct((t * ROWS, LANES), PACKED),
        jax.ShapeDtypeStruct((t, LANES), F32),
        jax.ShapeDtypeStruct((n_batch + 1, CHUNK, D_HALF), F32),
        jax.ShapeDtypeStruct((ts, D_HALF), F32),
        jax.ShapeDtypeStruct((n_batch + 1, CONV_W - 1, D_CONV), F32),
        jax.ShapeDtypeStruct((ts, D_CONV), F32),
    ]
    return pl.pallas_call(
        functools.partial(_mixer_kernel, n_prompt_tiles=npt, tiles_per_seq=tiles_per_seq),
        grid=(npt + nst,),
        in_specs=in_specs,
        out_specs=out_specs,
        out_shape=out_shape,
        scratch_shapes=[
            pltpu.VMEM((8, D_CONV), F32),
            pltpu.VMEM((TM, D_HALF), BF16),
            pltpu.VMEM((TM, D_CONV), BF16),
            pltpu.VMEM((TM, D_MODEL), BF16),
        ],
        compiler_params=pltpu.CompilerParams(
            dimension_semantics=("arbitrary",), vmem_limit_bytes=VMEM_LIMIT),
        name="mixer",
    )(z, xp, xs, f, lng, lnb, cw, wsp, bsp, wa, wb, wo, gm, wr, br)


def _dispatch_kernel(pos_ref, cnt_ref, off_ref, nused_ref, xm_hbm, xs_hbm, inbuf, zbuf, isem, rsem, zsem,
                     *, n_steps, n_tiles):
    zbuf[...] = _packed_zeros(zbuf.shape)
    nused = nused_ref[0]
    pad_bits = TM_E.bit_length() - 1

    def pad_copies(e, fn):
        cnt = cnt_ref[e]
        npad = (TM_E - cnt % TM_E) % TM_E
        start = off_ref[e] + cnt
        for b in range(pad_bits):
            size = 1 << b
            at = start + ((npad >> (b + 1)) << (b + 1))

            @pl.when(((npad >> b) & 1) == 1)
            def _():
                fn(pltpu.make_async_copy(zbuf.at[pl.ds(0, size * ROWS), :],
                                         xs_hbm.at[pl.ds(at * ROWS, size * ROWS), :], zsem))

    def tile_copy(tile):
        return pltpu.make_async_copy(zbuf, xs_hbm.at[pl.ds(tile * TM_E * ROWS, TM_E * ROWS), :], zsem)

    def loop(lo, hi, body):
        lax.fori_loop(lo, hi, lambda k, c: (body(k), c)[1], 0)

    loop(0, N_EXPERTS, lambda e: pad_copies(e, lambda cp: cp.start()))
    loop(nused, n_tiles, lambda tile: tile_copy(tile).start())

    def in_copy(step, slot):
        return pltpu.make_async_copy(xm_hbm.at[pl.ds(step * COPY_TILE * ROWS, COPY_TILE * ROWS), :],
                                     inbuf.at[slot], isem.at[slot])

    def rows_wait(slot):
        for _ in range(2):
            pltpu.make_async_copy(inbuf.at[slot], xs_hbm.at[pl.ds(0, COPY_TILE * ROWS), :], rsem.at[slot]).wait()

    in_copy(0, 0).start()

    def step(s):
        slot = s % DISPATCH_SLOTS
        nxt = (s + 1) % DISPATCH_SLOTS

        @pl.when(s >= DISPATCH_SLOTS - 1)
        def _():
            rows_wait(nxt)

        @pl.when(s + 1 < n_steps)
        def _():
            in_copy(s + 1, nxt).start()

        in_copy(s, slot).wait()
        base = s * COPY_TILE * 2

        def body(r, carry):
            for k in range(2):
                p = pos_ref[base + 2 * r + k]
                pltpu.make_async_copy(inbuf.at[slot, pl.ds(r * ROWS, ROWS), :],
                                      xs_hbm.at[pl.ds(p * ROWS, ROWS), :], rsem.at[slot]).start(priority=k)
            return carry

        lax.fori_loop(0, COPY_TILE, body, 0, unroll=4)

    loop(0, n_steps, step)
    for back in range(DISPATCH_SLOTS - 1, 0, -1):
        rows_wait((n_steps - back) % DISPATCH_SLOTS)
    loop(0, N_EXPERTS, lambda e: pad_copies(e, lambda cp: cp.wait()))
    loop(nused, n_tiles, lambda tile: tile_copy(tile).wait())


def _dispatch(pos, cnt, off, nused, xm, *, n_tiles):
    n_tokens = xm.shape[0] // ROWS
    assert n_tokens % COPY_TILE == 0 and n_tokens // COPY_TILE >= DISPATCH_SLOTS
    smem = pl.BlockSpec(memory_space=pltpu.SMEM)
    return pl.pallas_call(
        functools.partial(_dispatch_kernel, n_steps=n_tokens // COPY_TILE, n_tiles=n_tiles),
        in_specs=[smem, smem, smem, smem, pl.BlockSpec(memory_space=pl.ANY)],
        out_specs=pl.BlockSpec(memory_space=pl.ANY),
        out_shape=jax.ShapeDtypeStruct((n_tiles * TM_E * ROWS, LANES), PACKED),
        scratch_shapes=[
            pltpu.VMEM((DISPATCH_SLOTS, COPY_TILE * ROWS, LANES), PACKED),
            pltpu.VMEM((TM_E * ROWS, LANES), PACKED),
            pltpu.SemaphoreType.DMA((DISPATCH_SLOTS,)),
            pltpu.SemaphoreType.DMA((DISPATCH_SLOTS,)),
            pltpu.SemaphoreType.DMA(()),
        ],
        compiler_params=pltpu.CompilerParams(vmem_limit_bytes=VMEM_LIMIT),
        name="dispatch",
    )(pos, cnt, off, nused, xm)


def _expert_kernel(te_ref, nused_ref, xs_ref, wg_hbm, wu_hbm, wd_hbm, y_ref,
                   wgf, wuf, wdf, wsem, seen_ref):
    i = pl.program_id(0)
    nused = nused_ref[0]

    def weight_copies(e, slot):
        return [pltpu.make_async_copy(src.at[0, e], dst.at[slot], wsem.at[slot, k])
                for k, (src, dst) in enumerate(((wg_hbm, wgf), (wu_hbm, wuf), (wd_hbm, wdf)))]

    @pl.when(i == 0)
    def _():
        seen_ref[0] = 0
        for cp in weight_copies(te_ref[0], 0):
            cp.start()

    @pl.when(i < nused)
    def _():
        e = te_ref[i]
        new_expert = jnp.logical_or(i == 0, e != te_ref[jnp.maximum(i - 1, 0)])

        @pl.when(new_expert)
        def _():
            slot = seen_ref[0] % 2
            seen_ref[0] = seen_ref[0] + 1
            seen_ref[1] = slot
            nxt = lax.while_loop(lambda j: jnp.logical_and(j < nused, te_ref[jnp.minimum(j, nused - 1)] == e),
                                 lambda j: j + 1, i + 1)

            @pl.when(nxt < nused)
            def _():
                for cp in weight_copies(te_ref[jnp.minimum(nxt, nused - 1)], 1 - slot):
                    cp.start(priority=1)

            for cp in weight_copies(e, slot):
                cp.wait()

        cur = seen_ref[1]
        x = _load_token_major(xs_ref, TM_E).astype(BF16)
        g = jnp.dot(x, wgf[cur].astype(BF16), preferred_element_type=F32)
        u = jnp.dot(x, wuf[cur].astype(BF16), preferred_element_type=F32)
        hid = (g * _sigmoid(g)) * u
        _store_token_major(y_ref, jnp.dot(hid.astype(BF16), wdf[cur].astype(BF16), preferred_element_type=F32))

    @pl.when(i >= nused)
    def _():
        y_ref[...] = _packed_zeros(y_ref.shape)


def _experts(te, nused, xs, wg, wu, wd, *, n_tiles):
    grid_spec = pltpu.PrefetchScalarGridSpec(
        num_scalar_prefetch=2,
        grid=(n_tiles,),
        in_specs=[
            pl.BlockSpec((TM_E * ROWS, LANES), lambda i, te, nu: (jnp.minimum(i, nu[0] - 1), 0)),
            pl.BlockSpec(memory_space=pl.ANY),
            pl.BlockSpec(memory_space=pl.ANY),
            pl.BlockSpec(memory_space=pl.ANY),
        ],
        out_specs=pl.BlockSpec((TM_E * ROWS, LANES), lambda i, te, nu: (i, 0)),
        scratch_shapes=[
            pltpu.VMEM((2, D_MODEL, D_EXPERT), F32),
            pltpu.VMEM((2, D_MODEL, D_EXPERT), F32),
            pltpu.VMEM((2, D_EXPERT, D_MODEL), F32),
            pltpu.SemaphoreType.DMA((2, 3)),
            pltpu.SMEM((2,), jnp.int32),
        ],
    )
    return pl.pallas_call(
        _expert_kernel,
        grid_spec=grid_spec,
        out_shape=jax.ShapeDtypeStruct((n_tiles * TM_E * ROWS, LANES), PACKED),
        compiler_params=pltpu.CompilerParams(
            dimension_semantics=("arbitrary",), vmem_limit_bytes=VMEM_LIMIT),
        name="experts",
    )(te, nused, xs, wg, wu, wd)


def _tail_kernel(pos_ref, ys_hbm, h_ref, route_ref, pp_ref, ps_ref, gp_ref, wpg_ref, wpp_ref, gf_ref,
                 yp_ref, ysm_ref, ybuf, sem, h2_ref, a_ref, proj_ref, *, n_prompt_tiles, n_tiles):
    i = pl.program_id(0)
    slot = i % 2
    is_prompt = i < n_prompt_tiles

    def issue(tile, dst_slot):
        base = tile * TM * 2

        def body(r, carry):
            for k in range(2):
                p = pos_ref[base + 2 * r + k]
                pltpu.make_async_copy(ys_hbm.at[pl.ds(p * ROWS, ROWS), :],
                                      ybuf.at[dst_slot, k, pl.ds(r * ROWS, ROWS), :],
                                      sem.at[dst_slot]).start()
            return carry

        lax.fori_loop(0, TM, body, 0, unroll=4)

    def slot_wait(s):
        for k in range(2):
            pltpu.make_async_copy(ys_hbm.at[pl.ds(0, TM * ROWS), :], ybuf.at[s, k], sem.at[s]).wait()

    def row_sumsq(acc):
        return jnp.sum(acc, axis=-1, keepdims=True) * (1.0 / D_MODEL)

    @pl.when(i == 0)
    def _():
        issue(0, 0)

    slot_wait(slot)

    w0 = jnp.broadcast_to(route_ref[:, 2:3], (TM, LANES))
    w1 = jnp.broadcast_to(route_ref[:, 3:4], (TM, LANES))
    acc = jnp.zeros((TM, LANES), F32)
    for s in range(ROWS):
        word0 = ybuf[slot, 0, pl.ds(s, TM, stride=ROWS), :]
        word1 = ybuf[slot, 1, pl.ds(s, TM, stride=ROWS), :]
        for idx in range(2):
            cols = slice((idx * ROWS + s) * LANES, (idx * ROWS + s + 1) * LANES)
            y0 = pltpu.unpack_elementwise(word0, index=idx, packed_dtype=BF16, unpacked_dtype=F32)
            y1 = pltpu.unpack_elementwise(word1, index=idx, packed_dtype=BF16, unpacked_dtype=F32)
            h2 = h_ref[:, cols] + (w0 * y0 + w1 * y1)
            h2_ref[:, cols] = h2
            acc = acc + h2 * h2
    r = lax.rsqrt(row_sumsq(acc) + EPS)
    a_ref[...] = (h2_ref[...] * r * gp_ref[...]).astype(BF16)
    p = jnp.where(is_prompt, pp_ref[...], ps_ref[...]).astype(BF16)
    proj_ref[...] = jnp.dot(p, wpp_ref[...], preferred_element_type=F32)

    nxt_base = jnp.minimum(i + 1, n_tiles - 1) * TM * 2
    rows_per_chunk = TM // TAIL_CHUNKS
    width = D_MODEL // TAIL_CHUNKS
    acc = jnp.zeros((TM, LANES), F32)
    for c in range(TAIL_CHUNKS):
        for rr in range(rows_per_chunk):
            row = c * rows_per_chunk + rr
            for k in range(2):
                p_row = pos_ref[nxt_base + 2 * row + k]
                pltpu.make_async_copy(ys_hbm.at[pl.ds(p_row * ROWS, ROWS), :],
                                      ybuf.at[1 - slot, k, pl.ds(row * ROWS, ROWS), :],
                                      sem.at[1 - slot]).start(priority=k)
        cols = slice(c * width, (c + 1) * width)
        gate = _sigmoid(jnp.dot(a_ref[...], wpg_ref[:, cols], preferred_element_type=F32))
        h3 = h2_ref[:, cols] + gate * proj_ref[:, cols]
        h2_ref[:, cols] = h3
        for j in range(width // LANES):
            part = h3[:, j * LANES:(j + 1) * LANES]
            acc = acc + part * part
    scale = lax.rsqrt(row_sumsq(acc) + EPS)

    @pl.when(is_prompt)
    def _():
        yp_ref[...] = h2_ref[...] * scale * gf_ref[...]

    @pl.when(jnp.logical_not(is_prompt))
    def _():
        ysm_ref[...] = h2_ref[...] * scale * gf_ref[...]

    @pl.when(i == n_tiles - 1)
    def _():
        slot_wait(1 - slot)


def _tail(pos, ys, h, route, pp, ps, gp, wpg, wpp, gf):
    tp, ts = pp.shape[0], ps.shape[0]
    npt, nst = tp // TM, ts // TM
    ple = pp.shape[1]
    once = pl.Buffered(1)

    def pidx(i):
        return jnp.minimum(i, npt - 1)

    def sidx(i):
        return jnp.maximum(i - npt, 0)

    grid_spec = pltpu.PrefetchScalarGridSpec(
        num_scalar_prefetch=1,
        grid=(npt + nst,),
        in_specs=[
            pl.BlockSpec(memory_space=pl.ANY),
            pl.BlockSpec((TM, D_MODEL), lambda i, pos: (i, 0)),
            pl.BlockSpec((TM, LANES), lambda i, pos: (i, 0)),
            pl.BlockSpec((TM, ple), lambda i, pos: (pidx(i), 0)),
            pl.BlockSpec((TM, ple), lambda i, pos: (sidx(i), 0)),
            pl.BlockSpec((1, D_MODEL), lambda i, pos: (0, 0), pipeline_mode=once),
            pl.BlockSpec((D_MODEL, D_MODEL), lambda i, pos: (0, 0), pipeline_mode=once),
            pl.BlockSpec((ple, D_MODEL), lambda i, pos: (0, 0), pipeline_mode=once),
            pl.BlockSpec((1, D_MODEL), lambda i, pos: (0, 0), pipeline_mode=once),
        ],
        out_specs=[
            pl.BlockSpec((TM, D_MODEL), lambda i, pos: (pidx(i), 0)),
            pl.BlockSpec((TM, D_MODEL), lambda i, pos: (sidx(i), 0)),
        ],
        scratch_shapes=[
            pltpu.VMEM((2, 2, TM * ROWS, LANES), PACKED),
            pltpu.SemaphoreType.DMA((2,)),
            pltpu.VMEM((TM, D_MODEL), F32),
            pltpu.VMEM((TM, D_MODEL), BF16),
            pltpu.VMEM((TM, D_MODEL), F32),
        ],
    )
    return pl.pallas_call(
        functools.partial(_tail_kernel, n_prompt_tiles=npt, n_tiles=npt + nst),
        grid_spec=grid_spec,
        out_shape=[jax.ShapeDtypeStruct((tp, D_MODEL), F32), jax.ShapeDtypeStruct((ts, D_MODEL), F32)],
        compiler_params=pltpu.CompilerParams(
            dimension_semantics=("arbitrary",), vmem_limit_bytes=VMEM_LIMIT),
        name="tail",
    )(pos, ys, h, route, pp, ps, gp, wpg, wpp, gf)


def _route_tables(route, n_tiles):
    e = route[:, 0:2].astype(jnp.int32).reshape(-1)
    n_blocks = e.shape[0] // RANK_BLOCK
    onehot = (e[:, None] == jnp.arange(N_EXPERTS, dtype=jnp.int32)[None, :]).astype(BF16)
    onehot = onehot.reshape(n_blocks, RANK_BLOCK, N_EXPERTS)
    ltri = jnp.tril(jnp.ones((RANK_BLOCK, RANK_BLOCK), BF16), -1)
    within = jnp.einsum("ij,bjk->bik", ltri, onehot, preferred_element_type=F32)
    block_counts = jnp.sum(onehot.astype(F32), axis=1)
    before = jnp.cumsum(block_counts, axis=0) - block_counts
    counts = jnp.sum(block_counts, axis=0).astype(jnp.int32)
    padded = ((counts + TM_E - 1) // TM_E) * TM_E
    ends = jnp.cumsum(padded)
    offs = ends - padded
    pos = jnp.sum((within + before[:, None, :] + offs[None, None, :].astype(F32)) * onehot.astype(F32), axis=-1)
    pos = pos.reshape(-1)
    nused = ends[-1] // TM_E
    tile_id = jnp.arange(n_tiles, dtype=jnp.int32)
    te = jnp.sum((tile_id[:, None] * TM_E >= ends[None, :]).astype(jnp.int32), axis=1)
    te_last = jnp.sum((((nused - 1) * TM_E) >= ends).astype(jnp.int32))
    te = jnp.where(tile_id < nused, te, te_last)
    i32 = lambda a: a.astype(jnp.int32)
    return i32(te), i32(nused).reshape(1), i32(pos), i32(counts), i32(offs)


def kernel(x_prompt, x_sample, state_conv, p_prompt, p_sample, g_mix, w_in, ln_v_g, ln_v_b, w_spatial, b_spatial, conv_w, w_br_a, w_br_b, w_out, g_moe, w_router_group, b_router_group, w_router_expert, b_router_expert, w_exp_gate, w_exp_up, w_exp_down, g_ple, w_ple_gate, w_ple_proj, g_final):
    depth = w_in.shape[0]
    assert depth == 1, "single-layer step only"
    nb, seq, _ = x_prompt.shape
    ns, dseq, _ = x_sample.shape
    assert dseq == SAMPLE_SEQ and seq % TM == 0 and (ns * dseq) % TM_NORM == 0 and (nb * seq) % TM_NORM == 0
    assert (nb * seq + ns * dseq) % TM_IN == 0
    tp, ts = nb * seq, ns * dseq
    t = tp + ts
    xp = x_prompt.reshape(tp, D_MODEL)
    xs = x_sample.reshape(ts, D_MODEL)

    row = lambda a: a.reshape(1, -1)
    tril = jnp.tril(jnp.ones((CHUNK, CHUNK), dtype=bool))
    w_sp = jnp.where(tril[None], w_spatial[0], 0.0)
    eye = jnp.eye(CHUNK // dseq, dtype=F32)
    w_sp_s = jnp.einsum("ij,gts->gitjs", eye, w_sp[:, :dseq, :dseq]).reshape(G_A, CHUNK, CHUNK)
    wsp = jnp.stack([w_sp, w_sp_s]).astype(BF16)
    b_p = jnp.repeat(b_spatial[0].T, D_GROUP_A, axis=1)
    b_s = jnp.tile(b_p[:dseq], (CHUNK // dseq, 1))
    bsp = jnp.stack([b_p, b_s])
    lane_pad = LANES - N_GROUPS - N_EXPERTS
    wr = jnp.concatenate([w_router_group[0], w_router_expert[0], jnp.zeros((D_MODEL, lane_pad), F32)], axis=1)
    br = jnp.concatenate([b_router_group[0], b_router_expert[0], jnp.zeros((lane_pad,), F32)]).reshape(1, LANES)
    f = jnp.pad(state_conv[0], ((0, 0), (0, dseq - (CONV_W - 1)), (0, 0))).reshape(ts, D_CONV)

    n = _norm(xp, xs, row(g_mix[0]))
    z, (wa, wb, wo, wpg) = _inproj(n, w_in, [w_br_a, w_br_b, w_out, w_ple_gate])
    h, xm, route, vnp, vns, qp, qs = _mixer(
        z, xp, xs, f, row(ln_v_g[0]), row(ln_v_b[0]), conv_w[0], wsp, bsp, wa, wb, wo,
        row(g_moe[0]), wr.astype(BF16), br, n_batch=nb)

    n_tiles = (2 * t) // TM_E + N_EXPERTS
    te, nused, pos, counts, offs = _route_tables(route, n_tiles)
    xsort = _dispatch(pos, counts, offs, nused, xm, n_tiles=n_tiles)
    ys = _experts(te, nused, xsort, w_exp_gate, w_exp_up, w_exp_down, n_tiles=n_tiles)

    yp, ysm = _tail(pos, ys, h, route, p_prompt[0].reshape(tp, -1), p_sample[0].reshape(ts, -1),
                    row(g_ple[0]), wpg, w_ple_proj[0].astype(BF16), row(g_final))

    y_prompt = yp.reshape(nb, seq, D_MODEL)
    y_sample = ysm.reshape(ns, dseq, D_MODEL)
    conv_state_prompt = qp[:nb][None]
    conv_state_sample = qs.reshape(ns, dseq, D_CONV)[:, dseq - (CONV_W - 1):][None]
    v_rows_prompt = vnp[:nb][None]
    v_rows_sample = vns.reshape(ns, dseq, D_HALF)[None]
    return (y_prompt, y_sample, conv_state_prompt, conv_state_sample, v_rows_prompt, v_rows_sample)
```
